```python
import math
import jax, jax.numpy as jnp
from jax import lax
import numpy as np

D_MODEL = 1024
BATCH = 4
SEQ = 8192
DEPTH = 4

N_MIXERS = 2
N_RET_LAYERS = (DEPTH + 1) // 2
N_DSA_LAYERS = DEPTH // 2
D_FF = 2816
RMS_EPS = 1e-6

RET_HEADS = 4
RET_DK = D_MODEL // RET_HEADS
RET_DV = 2 * RET_DK
RET_CHUNK = 128
ROT_BASE = 10000.0
RET_IN = 2 * RET_HEADS * RET_DK + 2 * RET_HEADS * RET_DV

DSA_HEADS = 8
DSA_HEAD_DIM = D_MODEL // DSA_HEADS
DSA_Q_RANK = 256
DSA_KV_RANK = 256
IDX_HEADS = 8
IDX_DIM = 64
TOPK_MAX = 256
Q_BLOCK = 128
DSA_IN = DSA_Q_RANK + DSA_KV_RANK + IDX_DIM + IDX_HEADS

kernel_name = 'hybrid_retention_dsa_macaron'


def rmsnorm(x, g):
    xf = x.astype(jnp.float32)
    y = xf * lax.rsqrt(jnp.mean(xf * xf, axis=-1, keepdims=True) + RMS_EPS)
    return (y * g.astype(jnp.float32)).astype(x.dtype)


def layernorm(x, g, b):
    xf = x.astype(jnp.float32)
    mu = jnp.mean(xf, axis=-1, keepdims=True)
    var = jnp.mean(jnp.square(xf - mu), axis=-1, keepdims=True)
    y = (xf - mu) * lax.rsqrt(var + RMS_EPS)
    return (y * g.astype(jnp.float32) + b.astype(jnp.float32)).astype(x.dtype)


def swiglu(h, w_in, w_out):
    gate, up = jnp.split(h @ w_in, 2, axis=-1)
    return (jax.nn.silu(gate) * up) @ w_out


def rotate_pairs(t, cos, sin):
    tp = t.astype(jnp.float32).reshape(t.shape[:-1] + (t.shape[-1] // 2, 2))
    t0, t1 = tp[..., 0], tp[..., 1]
    c = cos[None, :, None, :]
    s = sin[None, :, None, :]
    out = jnp.stack([t0 * c - t1 * s, t1 * c + t0 * s], axis=-1)
    return out.reshape(t.shape).astype(t.dtype)


def retention(h, w_in, w_out):
    B, S, _ = h.shape
    H, dk, dv, C = RET_HEADS, RET_DK, RET_DV, RET_CHUNK
    n_chunks = S // C
    q, k, v, g = jnp.split(h @ w_in, [H * dk, 2 * H * dk, 2 * H * dk + H * dv], axis=-1)
    q = q.reshape(B, S, H, dk)
    k = k.reshape(B, S, H, dk)
    v = v.reshape(B, S, H, dv)
    pos = jnp.arange(S, dtype=jnp.float32)
    freq = 1.0 / (ROT_BASE ** jnp.linspace(0.0, 1.0, dk // 2, dtype=jnp.float32))
    ang = pos[:, None] * freq[None, :]
    cos, sin = jnp.cos(ang), jnp.sin(ang)
    q = rotate_pairs(q, cos, sin)
    k = rotate_pairs(k, cos, sin) * (dk ** -0.5)
    log_gamma = jnp.log(1.0 - 2.0 ** (-5.0 - jnp.arange(H, dtype=jnp.float32)))
    idx = jnp.arange(C, dtype=jnp.float32)
    diff = idx[:, None] - idx[None, :]
    inner_decay = jnp.where(diff[None] >= 0, jnp.exp(jnp.maximum(diff, 0.0)[None] * log_gamma[:, None, None]), 0.0)
    q_decay = jnp.exp((idx[None, :] + 1.0) * log_gamma[:, None])
    k_decay = jnp.exp((C - 1.0 - idx[None, :]) * log_gamma[:, None])
    chunk_decay = jnp.exp(C * log_gamma)

    def to_chunks(t):
        return t.reshape(B, n_chunks, C, H, t.shape[-1]).transpose(1, 0, 3, 2, 4)

    def step(state, inp):
        qc, kc, vc = inp
        scores = jnp.einsum('bhid,bhjd->bhij', qc, kc) * inner_decay[None]
        inner = jnp.einsum('bhij,bhje->bhie', scores, vc)
        cross = jnp.einsum('bhid,bhde->bhie', qc, state) * q_decay[None, :, :, None]
        new_state = state * chunk_decay[None, :, None, None] + jnp.einsum(
            'bhjd,bhje->bhde', kc * k_decay[None, :, :, None], vc)
        return new_state, inner + cross

    state0 = jnp.zeros((B, H, dk, dv), jnp.float32)
    _, out = lax.scan(step, state0, (to_chunks(q), to_chunks(k), to_chunks(v)))
    out = out.transpose(1, 0, 3, 2, 4).reshape(B, S, H, dv).astype(jnp.float32)
    out = out * lax.rsqrt(jnp.mean(out * out, axis=-1, keepdims=True) + RMS_EPS)
    out = out.reshape(B, S, H * dv).astype(h.dtype)
    return (jax.nn.silu(g) * out) @ w_out


def dsa_attention(h, w_in, q_norm, kv_norm, w_uq, w_qidx, kidx_g, kidx_b, w_uk, w_uv, w_out):
    B, S, _ = h.shape
    H, dh, HI, DI = DSA_HEADS, DSA_HEAD_DIM, IDX_HEADS, IDX_DIM
    n_blocks = S // Q_BLOCK
    top_k = min(TOPK_MAX, S // 4)
    c_q, c_kv, k_idx, w_idx = jnp.split(
        h @ w_in, [DSA_Q_RANK, DSA_Q_RANK + DSA_KV_RANK, DSA_Q_RANK + DSA_KV_RANK + IDX_DIM], axis=-1)
    c_q = rmsnorm(c_q, q_norm)
    c_kv = rmsnorm(c_kv, kv_norm)
    k_idx = layernorm(k_idx, kidx_g, kidx_b)
    w_idx = w_idx * (HI ** -0.5)
    key_pos = jnp.arange(S)
    batch_ix = jnp.arange(B)[:, None, None]

    def block(args):
        blk, cq_b, w_b = args
        q_pos = blk * Q_BLOCK + jnp.arange(Q_BLOCK)
        q_idx = (cq_b @ w_qidx).reshape(B, Q_BLOCK, HI, DI)
        dots = jnp.einsum('bthd,bsd->btsh', q_idx, k_idx) * (DI ** -0.5)
        score = jnp.einsum('btsh,bth->bts', jax.nn.relu(dots), w_b).astype(jnp.float32)
        causal = key_pos[None, :] <= q_pos[:, None]
        score = jnp.where(causal[None], score, -jnp.inf)
        _, sel = lax.top_k(score, top_k)
        valid = sel <= q_pos[None, :, None]
        c_sel = c_kv[batch_ix, sel]
        q = (cq_b @ w_uq).reshape(B, Q_BLOCK, H, dh)
        q_lat = jnp.einsum('bthd,rhd->bthr', q, w_uk)
        logits = jnp.einsum('bthr,btkr->bthk', q_lat, c_sel).astype(jnp.float32) * (dh ** -0.5)
        logits = jnp.where(valid[:, :, None, :], logits, -jnp.inf)
        p = jax.nn.softmax(logits, axis=-1).astype(c_sel.dtype)
        o_lat = jnp.einsum('bthk,btkr->bthr', p, c_sel)
        o = jnp.einsum('bthr,rhd->bthd', o_lat, w_uv)
        return o.reshape(B, Q_BLOCK, H * dh)

    cq_blocks = c_q.reshape(B, n_blocks, Q_BLOCK, DSA_Q_RANK).transpose(1, 0, 2, 3)
    w_blocks = w_idx.reshape(B, n_blocks, Q_BLOCK, HI).transpose(1, 0, 2, 3)
    out = lax.map(block, (jnp.arange(n_blocks), cq_blocks, w_blocks))
    out = out.transpose(1, 0, 2, 3).reshape(B, S, H * dh)
    return out @ w_out


def setup_inputs(seed: int = 0) -> dict:
    key = jax.random.key(seed)
    ks = iter(jax.random.split(key, 32))

    def w(shape, fan_in):
        return jax.random.normal(next(ks), shape, jnp.float32) * (fan_in ** -0.5)

    def gain(shape):
        return 1.0 + 0.02 * jax.random.normal(next(ks), shape, jnp.float32)

    NR, ND = N_RET_LAYERS, N_DSA_LAYERS
    return {
        'x': jax.random.normal(next(ks), (BATCH, SEQ, D_MODEL), jnp.float32),
        'norm_ffn1': gain((DEPTH, D_MODEL)),
        'w_ffn1_in': w((DEPTH, D_MODEL, 2 * D_FF), D_MODEL),
        'w_ffn1_out': w((DEPTH, D_FF, D_MODEL), D_FF),
        'norm_mix': gain((DEPTH, D_MODEL)),
        'norm_ffn2': gain((DEPTH, D_MODEL)),
        'w_ffn2_in': w((DEPTH, D_MODEL, 2 * D_FF), D_MODEL),
        'w_ffn2_out': w((DEPTH, D_FF, D_MODEL), D_FF),
        'ret_w_in': w((NR, D_MODEL, RET_IN), D_MODEL),
        'ret_w_out': w((NR, RET_HEADS * RET_DV, D_MODEL), RET_HEADS * RET_DV),
        'dsa_w_in': w((ND, D_MODEL, DSA_IN), D_MODEL),
        'dsa_q_norm': gain((ND, DSA_Q_RANK)),
        'dsa_kv_norm': gain((ND, DSA_KV_RANK)),
        'dsa_w_uq': w((ND, DSA_Q_RANK, DSA_HEADS * DSA_HEAD_DIM), DSA_Q_RANK),
        'dsa_w_qidx': w((ND, DSA_Q_RANK, IDX_HEADS * IDX_DIM), DSA_Q_RANK),
        'dsa_kidx_g': gain((ND, IDX_DIM)),
        'dsa_kidx_b': 0.02 * jax.random.normal(next(ks), (ND, IDX_DIM), jnp.float32),
        'dsa_w_uk': w((ND, DSA_KV_RANK, DSA_HEADS, DSA_HEAD_DIM), DSA_KV_RANK),
        'dsa_w_uv': w((ND, DSA_KV_RANK, DSA_HEADS, DSA_HEAD_DIM), DSA_KV_RANK),
        'dsa_w_out': w((ND, DSA_HEADS * DSA_HEAD_DIM, D_MODEL), DSA_HEADS * DSA_HEAD_DIM),
        'final_norm': gain((D_MODEL,)),
    }


def reference(x, norm_ffn1, w_ffn1_in, w_ffn1_out, norm_mix, norm_ffn2, w_ffn2_in, w_ffn2_out,
              ret_w_in, ret_w_out, dsa_w_in, dsa_q_norm, dsa_kv_norm, dsa_w_uq, dsa_w_qidx,
              dsa_kidx_g, dsa_kidx_b, dsa_w_uk, dsa_w_uv, dsa_w_out, final_norm):
    for layer in range(DEPTH):
        x = x + 0.5 * swiglu(rmsnorm(x, norm_ffn1[layer]), w_ffn1_in[layer], w_ffn1_out[layer])
        h = rmsnorm(x, norm_mix[layer])
        j = layer // N_MIXERS
        if layer % N_MIXERS == 0:
            x = x + retention(h, ret_w_in[j], ret_w_out[j])
        else:
            x = x + dsa_attention(h, dsa_w_in[j], dsa_q_norm[j], dsa_kv_norm[j], dsa_w_uq[j],
                                  dsa_w_qidx[j], dsa_kidx_g[j], dsa_kidx_b[j], dsa_w_uk[j],
                                  dsa_w_uv[j], dsa_w_out[j])
        x = x + 0.5 * swiglu(rmsnorm(x, norm_ffn2[layer]), w_ffn2_in[layer], w_ffn2_out[layer])
    return rmsnorm(x, final_norm)
```

```python
import functools
import math

import jax
import jax.numpy as jnp
import numpy as np
from jax import lax
from jax.experimental import pallas as pl
from jax.experimental.pallas import tpu as pltpu

F32 = jnp.float32
BF16 = jnp.bfloat16

D_MODEL = 1024
DEPTH = 4
D_FF = 2816
RMS_EPS = 1e-6

RET_HEADS = 4
RET_DK = D_MODEL // RET_HEADS
RET_DV = 2 * RET_DK
RET_CHUNK = 128
ROT_BASE = 10000.0

DSA_HEADS = 8
DSA_HEAD_DIM = D_MODEL // DSA_HEADS
DSA_Q_RANK = 256
DSA_KV_RANK = 256
IDX_HEADS = 8
IDX_DIM = 64
TOPK_MAX = 256

LANES = 128
VMEM_LIMIT_BYTES = 56 * 1024 * 1024

FFN_ROWS = 512
FFN_CHUNK = 256
RET_ROWS = 512
PROJ_ROWS = 512
ATT_Q = 128
ATT_K = 512
MASKED = -1e30
BISECT_CAP = 400


def _resident(shape):
    nd = len(shape)
    return pl.BlockSpec(shape, lambda *_: (0,) * nd, pipeline_mode=pl.Buffered(1))


def _params(n_grid):
    return pltpu.CompilerParams(
        dimension_semantics=("arbitrary",) * n_grid, vmem_limit_bytes=VMEM_LIMIT_BYTES)


def _rms(x, g):
    y = x * lax.rsqrt(jnp.mean(x * x, axis=-1, keepdims=True) + RMS_EPS)
    return y * g


def _dot(a, b):
    return jnp.dot(a, b, preferred_element_type=F32)


def _dot_nt(a, b):
    return lax.dot_general(a, b, (((1,), (1,)), ((), ())), preferred_element_type=F32)


def _dot_tn(a, b):
    return lax.dot_general(a, b, (((0,), (0,)), ((), ())), preferred_element_type=F32)


def _ffn_kernel(x_ref, g_ref, wg_ref, wu_ref, wo_ref, *rest, final):
    if final:
        fg_ref, o_ref, acc_ref = rest
    else:
        o_ref, acc_ref = rest
    x = x_ref[...]
    h = _rms(x, g_ref[...]).astype(BF16)
    acc_ref[...] = jnp.zeros_like(acc_ref)

    def body(c, carry):
        gate = _dot(h, wg_ref[c])
        up = _dot(h, wu_ref[c])
        a = (gate * jax.nn.sigmoid(gate) * up).astype(BF16)
        acc_ref[...] += _dot(a, wo_ref[c])
        return carry

    lax.fori_loop(0, wg_ref.shape[0], body, 0)
    y = x + 0.5 * acc_ref[...]
    if final:
        y = _rms(y, fg_ref[...])
    o_ref[...] = y


def _ffn(x2, g, w_in, w_out, final_g=None):
    n, d = x2.shape
    nc = D_FF // FFN_CHUNK
    wg = w_in[:, :D_FF].reshape(d, nc, FFN_CHUNK).transpose(1, 0, 2).astype(BF16)
    wu = w_in[:, D_FF:].reshape(d, nc, FFN_CHUNK).transpose(1, 0, 2).astype(BF16)
    wo = w_out.reshape(nc, FFN_CHUNK, d).astype(BF16)
    row = pl.BlockSpec((FFN_ROWS, d), lambda i: (i, 0))
    ins = [x2, g.reshape(1, d), wg, wu, wo]
    specs = [row, _resident((1, d)), _resident(wg.shape), _resident(wu.shape), _resident(wo.shape)]
    if final_g is not None:
        ins.append(final_g.reshape(1, d))
        specs.append(_resident((1, d)))
    return pl.pallas_call(
        functools.partial(_ffn_kernel, final=final_g is not None),
        grid=(n // FFN_ROWS,),
        in_specs=specs,
        out_specs=row,
        out_shape=jax.ShapeDtypeStruct((n, d), F32),
        scratch_shapes=[pltpu.VMEM((FFN_ROWS, d), F32)],
        compiler_params=_params(1),
        name="ffn_final" if final_g is not None else "ffn",
    )(*ins)


def _ret_kernel(x_ref, g_ref, win_ref, wout_ref, cos_ref, sin_ref, inner_ref, qdec_ref, kdec_ref,
                cdec_ref, o_ref, q_s, k_s, v_s, gate_s, y_s, state_s):
    H, dk, dv, C = RET_HEADS, RET_DK, RET_DV, RET_CHUNK
    half = dk // 2

    @pl.when(pl.program_id(1) == 0)
    def _():
        state_s[...] = jnp.zeros_like(state_s)

    x = x_ref[...]
    h = _rms(x, g_ref[...]).astype(BF16)
    cos = cos_ref[...]
    sin = sin_ref[...]
    q = _dot(h, win_ref[:, 0:H * dk])
    k = _dot(h, win_ref[:, H * dk:2 * H * dk])
    for hh in range(H):
        q0 = q[:, hh * dk:hh * dk + half]
        q1 = q[:, hh * dk + half:(hh + 1) * dk]
        q_s[:, hh * dk:hh * dk + half] = (q0 * cos - q1 * sin).astype(BF16)
        q_s[:, hh * dk + half:(hh + 1) * dk] = (q1 * cos + q0 * sin).astype(BF16)
        k0 = k[:, hh * dk:hh * dk + half]
        k1 = k[:, hh * dk + half:(hh + 1) * dk]
        k_s[:, hh * dk:hh * dk + half] = (k0 * cos - k1 * sin) * (dk ** -0.5)
        k_s[:, hh * dk + half:(hh + 1) * dk] = (k1 * cos + k0 * sin) * (dk ** -0.5)
    v_s[...] = _dot(h, win_ref[:, 2 * H * dk:2 * H * dk + H * dv]).astype(BF16)
    gate_s[...] = _dot(h, win_ref[:, 2 * H * dk + H * dv:])

    def chunk(c, carry):
        r0 = pl.multiple_of(c * C, C)
        rows = pl.ds(r0, C)
        y = jnp.zeros((C, D_MODEL), F32)
        for hh in range(H):
            qc = q_s[rows, hh * dk:(hh + 1) * dk]
            kc = k_s[rows, hh * dk:(hh + 1) * dk]
            vc = v_s[rows, hh * dv:(hh + 1) * dv]
            st = state_s[hh]
            scores = _dot_nt(qc, kc.astype(BF16)) * inner_ref[hh]
            inner = _dot(scores.astype(BF16), vc)
            cross = _dot(qc, st.astype(BF16)) * qdec_ref[hh]
            kd = (kc * kdec_ref[hh]).astype(BF16)
            state_s[hh] = st * cdec_ref[hh] + _dot_tn(kd, vc)
            out = inner + cross
            out = out * lax.rsqrt(jnp.mean(out * out, axis=-1, keepdims=True) + RMS_EPS)
            gt = gate_s[rows, hh * dv:(hh + 1) * dv]
            z = (gt * jax.nn.sigmoid(gt) * out).astype(BF16)
            y = y + _dot(z, wout_ref[hh * dv:(hh + 1) * dv, :])
        y_s[rows, :] = y
        return carry

    lax.fori_loop(0, x.shape[0] // C, chunk, 0)
    o_ref[...] = x + y_s[...]


def _ret_tables():
    H, C, dk = RET_HEADS, RET_CHUNK, RET_DK
    log_gamma = jnp.log(1.0 - 2.0 ** (-5.0 - jnp.arange(H, dtype=F32)))
    idx = jnp.arange(C, dtype=F32)
    diff = idx[:, None] - idx[None, :]
    inner = jnp.where(diff[None] >= 0,
                      jnp.exp(jnp.maximum(diff, 0.0)[None] * log_gamma[:, None, None]), 0.0)
    qdec = jnp.exp((idx[None, :] + 1.0) * log_gamma[:, None])[:, :, None]
    kdec = jnp.exp((C - 1.0 - idx[None, :]) * log_gamma[:, None])[:, :, None]
    cdec = jnp.exp(C * log_gamma)[:, None, None]
    return inner, qdec, kdec, cdec


def _rot_tables(s):
    pos = jnp.arange(s, dtype=F32)
    freq = 1.0 / (ROT_BASE ** jnp.linspace(0.0, 1.0, RET_DK // 2, dtype=F32))
    ang = pos[:, None] * freq[None, :]
    return jnp.cos(ang), jnp.sin(ang)


def _pair_split_columns(w, heads, dim):
    d = w.shape[0]
    w = w.reshape(d, heads, dim // 2, 2)
    return jnp.concatenate([w[..., 0], w[..., 1]], axis=-1).reshape(d, heads * dim)


def _retention(x, g, w_in, w_out):
    b, s, d = x.shape
    H, dk, dv, C = RET_HEADS, RET_DK, RET_DV, RET_CHUNK
    tb = min(RET_ROWS, s)
    wq = _pair_split_columns(w_in[:, :H * dk], H, dk)
    wk = _pair_split_columns(w_in[:, H * dk:2 * H * dk], H, dk)
    win = jnp.concatenate([wq, wk, w_in[:, 2 * H * dk:]], axis=1).astype(BF16)
    wout = w_out.astype(BF16)
    cos, sin = _rot_tables(s)
    inner, qdec, kdec, cdec = _ret_tables()
    row = pl.BlockSpec((None, tb, d), lambda bi, i: (bi, i, 0))
    rot = pl.BlockSpec((tb, dk // 2), lambda bi, i: (i, 0))
    return pl.pallas_call(
        _ret_kernel,
        grid=(b, s // tb),
        in_specs=[row, _resident((1, d)), _resident(win.shape), _resident(wout.shape), rot, rot,
                  _resident(inner.shape), _resident(qdec.shape), _resident(kdec.shape),
                  _resident(cdec.shape)],
        out_specs=row,
        out_shape=jax.ShapeDtypeStruct((b, s, d), F32),
        scratch_shapes=[
            pltpu.VMEM((tb, H * dk), BF16),
            pltpu.VMEM((tb, H * dk), F32),
            pltpu.VMEM((tb, H * dv), BF16),
            pltpu.VMEM((tb, H * dv), F32),
            pltpu.VMEM((tb, d), F32),
            pltpu.VMEM((H, dk, dv), F32),
        ],
        compiler_params=_params(2),
        name="retention",
    )(x, g.reshape(1, d), win, wout, cos, sin, inner, qdec, kdec, cdec)


def _dsa_proj_kernel(x_ref, g_ref, win_ref, qn_ref, kvn_ref, kg_ref, kb_ref, wqi_ref, wuq_ref,
                     wuk_ref, qidx_ref, w_ref, qlat_ref, kidx_ref, ckv_ref):
    RQ, RKV, HI = DSA_Q_RANK, DSA_KV_RANK, IDX_HEADS
    h = _rms(x_ref[...], g_ref[...]).astype(BF16)
    proj = _dot(h, win_ref[...])
    c_q = _rms(proj[:, :RQ], qn_ref[...])
    c_kv = _rms(proj[:, RQ:RQ + RKV], kvn_ref[...])
    kx = proj[:, RQ + RKV:RQ + RKV + LANES]
    lane = lax.broadcasted_iota(jnp.int32, kx.shape, 1)
    mu = jnp.sum(kx, axis=-1, keepdims=True) * (1.0 / IDX_DIM)
    cen = jnp.where(lane < IDX_DIM, kx - mu, 0.0)
    var = jnp.sum(cen * cen, axis=-1, keepdims=True) * (1.0 / IDX_DIM)
    k_idx = cen * lax.rsqrt(var + RMS_EPS) * kg_ref[...] + kb_ref[...]
    w_ref[...] = proj[:, RQ + RKV + LANES:] * (HI ** -0.5)
    kidx_ref[...] = k_idx.astype(BF16)
    ckv_ref[...] = c_kv.astype(BF16)
    cq = c_q.astype(BF16)
    qidx_ref[...] = (_dot(cq, wqi_ref[...]) * (IDX_DIM ** -0.5)).astype(BF16)
    q = _dot(cq, wuq_ref[...]).astype(BF16)
    dh = DSA_HEAD_DIM
    for hh in range(DSA_HEADS):
        qlat_ref[:, hh * RKV:(hh + 1) * RKV] = _dot(q[:, hh * dh:(hh + 1) * dh], wuk_ref[hh]).astype(BF16)


def _dsa_proj(x2, g, w_in, q_norm, kv_norm, w_uq, w_qidx, kidx_g, kidx_b, w_uk):
    n, d = x2.shape
    RQ, RKV, HI, DI, H, dh = DSA_Q_RANK, DSA_KV_RANK, IDX_HEADS, IDX_DIM, DSA_HEADS, DSA_HEAD_DIM
    pad_k = jnp.zeros((d, LANES - DI), F32)
    pad_w = jnp.zeros((d, LANES - HI), F32)
    win = jnp.concatenate([w_in[:, :RQ + RKV + DI], pad_k, w_in[:, RQ + RKV + DI:], pad_w],
                          axis=1).astype(BF16)
    kg = jnp.pad(kidx_g, (0, LANES - DI)).reshape(1, LANES)
    kb = jnp.pad(kidx_b, (0, LANES - DI)).reshape(1, LANES)
    wqi = jnp.pad(w_qidx.reshape(RQ, HI, DI), ((0, 0), (0, 0), (0, LANES - DI)))
    wqi = wqi.reshape(RQ, HI * LANES).astype(BF16)
    wuq = w_uq.astype(BF16)
    wuk = w_uk.transpose(1, 2, 0).astype(BF16)
    tm = PROJ_ROWS
    row = lambda w: pl.BlockSpec((tm, w), lambda i: (i, 0))
    return pl.pallas_call(
        _dsa_proj_kernel,
        grid=(n // tm,),
        in_specs=[row(d), _resident((1, d)), _resident(win.shape), _resident((1, RQ)),
                  _resident((1, RKV)), _resident((1, LANES)), _resident((1, LANES)),
                  _resident(wqi.shape), _resident(wuq.shape), _resident(wuk.shape)],
        out_specs=[row(HI * LANES), row(LANES), row(H * RKV), row(LANES), row(RKV)],
        out_shape=[
            jax.ShapeDtypeStruct((n, HI * LANES), BF16),
            jax.ShapeDtypeStruct((n, LANES), F32),
            jax.ShapeDtypeStruct((n, H * RKV), BF16),
            jax.ShapeDtypeStruct((n, LANES), BF16),
            jax.ShapeDtypeStruct((n, RKV), BF16),
        ],
        compiler_params=_params(1),
        name="dsa_proj",
    )(x2, g.reshape(1, d), win, q_norm.reshape(1, RQ), kv_norm.reshape(1, RKV), kg, kb, wqi, wuq, wuk)


def _dsa_attn_kernel(x_ref, qidx_ref, w_ref, qlat_ref, kidx_ref, ckv_ref, wuv_ref, wout_ref, o_ref,
                     sc_s, p_s, acc_s, m_s, l_s, lo_s, hi_s, clo_s, done_s, *, top_k):
    TQ, TK = ATT_Q, ATT_K
    HI, H, RKV = IDX_HEADS, DSA_HEADS, DSA_KV_RANK
    i = pl.program_id(1)
    n_tiles = (i * TQ + TQ + TK - 1) // TK
    row = i * TQ + lax.broadcasted_iota(jnp.int32, (TQ, 1), 0)
    col0 = lax.broadcasted_iota(jnp.int32, (TQ, TK), 1)
    neg_inf = jnp.float32(-jnp.inf)

    qi = jnp.concatenate([qidx_ref[:, hh * LANES:(hh + 1) * LANES] for hh in range(HI)], axis=0)
    w = w_ref[...]
    w_cols = [w[:, hh:hh + 1] for hh in range(HI)]

    def score_tile(j, carry):
        c0 = pl.multiple_of(j * TK, TK)
        dots = _dot_nt(qi, kidx_ref[pl.ds(c0, TK), :])
        s = jnp.zeros((TQ, TK), F32)
        for hh in range(HI):
            s = s + jnp.maximum(dots[hh * TQ:(hh + 1) * TQ], 0.0) * w_cols[hh]
        sc_s[:, pl.ds(c0, TK)] = jnp.where(col0 + c0 <= row, s, neg_inf)
        return carry

    lax.fori_loop(0, n_tiles, score_tile, 0)

    def row_reduce(tile_fn, combine, init):
        def body(j, acc):
            c0 = pl.multiple_of(j * TK, TK)
            t = tile_fn(sc_s[:, pl.ds(c0, TK)], c0)
            for u in range(TK // LANES):
                acc = combine(acc, t[:, u * LANES:(u + 1) * LANES])
            return acc
        return lax.fori_loop(0, n_tiles, body, jnp.full((TQ, LANES), init, type(init)))

    def count(pred):
        part = row_reduce(lambda t, c0: jnp.where(pred(t, c0), 1, 0).astype(jnp.int32),
                          jnp.add, np.int32(0))
        return jnp.sum(part, axis=1, keepdims=True)

    n_valid = row + 1
    rmax = jnp.max(row_reduce(lambda t, c0: t, jnp.maximum, np.float32(-np.inf)), axis=1, keepdims=True)
    rmin = jnp.min(row_reduce(lambda t, c0: jnp.where(t == neg_inf, jnp.inf, t), jnp.minimum,
                              np.float32(np.inf)), axis=1, keepdims=True)
    c_ge0 = count(lambda t, c0: t >= 0.0)
    c_gt0 = count(lambda t, c0: t > 0.0)
    all_sel = n_valid <= top_k
    zero_tie = (c_gt0 < top_k) & (c_ge0 >= top_k)
    positive = c_gt0 >= top_k
    lo0 = jnp.where(positive | zero_tie, 0.0, rmin)
    hi0 = jnp.where(positive, 2.0 * rmax, 0.0)
    clo0 = jnp.where(positive | zero_tie, c_ge0, n_valid)
    lo0 = jnp.where(all_sel, jnp.float32(jnp.finfo(jnp.float32).min), lo0)
    clo0 = jnp.where(all_sel, n_valid, clo0)
    done0 = all_sel | zero_tie | (clo0 == top_k)
    lo_s[...] = lo0
    hi_s[...] = hi0
    clo_s[...] = clo0
    done_s[...] = done0.astype(jnp.int32)

    def bisect_cond(carry):
        it, pending = carry
        return (pending > 0) & (it < BISECT_CAP)

    def bisect_body(carry):
        it, _ = carry
        lo = lo_s[...]
        hi = hi_s[...]
        clo = clo_s[...]
        done = done_s[...] > 0
        mid = 0.5 * lo + 0.5 * hi
        collapsed = (mid <= lo) | (mid >= hi)
        cm = count(lambda t, c0: t >= mid)
        move = jnp.logical_not(done | collapsed)
        up = move & (cm >= top_k)
        down = move & (cm < top_k)
        clo = jnp.where(up, cm, clo)
        lo_s[...] = jnp.where(up, mid, lo)
        hi_s[...] = jnp.where(down, mid, hi)
        clo_s[...] = clo
        done = done | collapsed | (clo == top_k)
        done_s[...] = done.astype(jnp.int32)
        return it + 1, jnp.sum(jnp.where(done, 0, 1))

    lax.while_loop(bisect_cond, bisect_body, (jnp.int32(0), jnp.sum(jnp.where(done0, 0, 1))))
    tau = lo_s[...]

    excess = clo_s[...] > top_k

    @pl.when(jnp.sum(jnp.where(excess, 1, 0)) > 0)
    def _():
        need = top_k - count(lambda t, c0: t > tau)
        n_steps = max(1, math.ceil(math.log2(sc_s.shape[1]))) + 1

        def step(_, jb):
            j_lo, j_hi = jb
            j_mid = (j_lo + j_hi) >> 1
            c = count(lambda t, c0: (t == tau) & (col0 + c0 <= j_mid))
            ok = c >= need
            return jnp.where(ok, j_lo, j_mid), jnp.where(ok, j_mid, j_hi)

        _, j_cut = lax.fori_loop(0, n_steps, step,
                                 (jnp.full((TQ, 1), -1, jnp.int32), row))

        def fix(j, carry):
            c0 = pl.multiple_of(j * TK, TK)
            t = sc_s[:, pl.ds(c0, TK)]
            drop = excess & (t == tau) & (col0 + c0 > j_cut)
            sc_s[:, pl.ds(c0, TK)] = jnp.where(drop, neg_inf, t)
            return carry

        lax.fori_loop(0, n_tiles, fix, 0)

    ql = jnp.concatenate([qlat_ref[:, hh * RKV:(hh + 1) * RKV] for hh in range(H)], axis=0)
    scale = DSA_HEAD_DIM ** -0.5
    m_s[...] = jnp.full_like(m_s, MASKED)
    l_s[...] = jnp.zeros_like(l_s)
    acc_s[...] = jnp.zeros_like(acc_s)

    def attn_tile(j, carry):
        c0 = pl.multiple_of(j * TK, TK)
        ckv = ckv_ref[pl.ds(c0, TK), :]
        logits = _dot_nt(ql, ckv) * scale
        sel = sc_s[:, pl.ds(c0, TK)] >= tau
        for hh in range(H):
            rows = slice(hh * TQ, (hh + 1) * TQ)
            lg = jnp.where(sel, logits[rows], MASKED)
            m_old = m_s[rows]
            m_new = jnp.maximum(m_old, jnp.max(lg, axis=1, keepdims=True))
            alpha = jnp.exp(m_old - m_new)
            p = jnp.exp(lg - m_new)
            l_s[rows] = alpha * l_s[rows] + jnp.sum(p, axis=1, keepdims=True)
            m_s[rows] = m_new
            acc_s[rows] = alpha * acc_s[rows]
            p_s[rows] = p.astype(BF16)
        acc_s[...] += _dot(p_s[...], ckv)
        return carry

    lax.fori_loop(0, n_tiles, attn_tile, 0)

    o_lat = (acc_s[...] / l_s[...]).astype(BF16)
    o = jnp.concatenate([_dot(o_lat[hh * TQ:(hh + 1) * TQ], wuv_ref[hh]) for hh in range(H)], axis=1)
    o_ref[...] = x_ref[...] + _dot(o.astype(BF16), wout_ref[...])


def _dsa_attn(x, qidx, w, qlat, kidx, ckv, w_uv, w_out):
    b, s, d = x.shape
    H, RKV, HI = DSA_HEADS, DSA_KV_RANK, IDX_HEADS
    tq = ATT_Q
    top_k = min(TOPK_MAX, s // 4)
    wuv = w_uv.transpose(1, 0, 2).astype(BF16)
    wout = w_out.astype(BF16)
    qrow = lambda width: pl.BlockSpec((None, tq, width), lambda bi, i: (bi, i, 0))
    seq = lambda width: pl.BlockSpec((None, s, width), lambda bi, i: (bi, 0, 0))
    return pl.pallas_call(
        functools.partial(_dsa_attn_kernel, top_k=top_k),
        grid=(b, s // tq),
        in_specs=[qrow(d), qrow(HI * LANES), qrow(LANES), qrow(H * RKV), seq(LANES), seq(RKV),
                  _resident(wuv.shape), _resident(wout.shape)],
        out_specs=qrow(d),
        out_shape=jax.ShapeDtypeStruct((b, s, d), F32),
        scratch_shapes=[
            pltpu.VMEM((tq, s), F32),
            pltpu.VMEM((H * tq, ATT_K), BF16),
            pltpu.VMEM((H * tq, RKV), F32),
            pltpu.VMEM((H * tq, 1), F32),
            pltpu.VMEM((H * tq, 1), F32),
            pltpu.VMEM((tq, 1), F32),
            pltpu.VMEM((tq, 1), F32),
            pltpu.VMEM((tq, 1), jnp.int32),
            pltpu.VMEM((tq, 1), jnp.int32),
        ],
        compiler_params=_params(2),
        name="dsa_attn",
    )(x, qidx.reshape(b, s, -1), w.reshape(b, s, -1), qlat.reshape(b, s, -1),
      kidx.reshape(b, s, -1), ckv.reshape(b, s, -1), wuv, wout)


def _dsa(x, g, w_in, q_norm, kv_norm, w_uq, w_qidx, kidx_g, kidx_b, w_uk, w_uv, w_out):
    b, s, d = x.shape
    qidx, w, qlat, kidx, ckv = _dsa_proj(x.reshape(b * s, d), g, w_in, q_norm, kv_norm, w_uq, w_qidx,
                                         kidx_g, kidx_b, w_uk)
    return _dsa_attn(x, qidx, w, qlat, kidx, ckv, w_uv, w_out)


def kernel(x, norm_ffn1, w_ffn1_in, w_ffn1_out, norm_mix, norm_ffn2, w_ffn2_in, w_ffn2_out, ret_w_in,
           ret_w_out, dsa_w_in, dsa_q_norm, dsa_kv_norm, dsa_w_uq, dsa_w_qidx, dsa_kidx_g, dsa_kidx_b,
           dsa_w_uk, dsa_w_uv, dsa_w_out, final_norm):
    b, s, d = x.shape
    depth = norm_ffn1.shape[0]
    for layer in range(depth):
        x = _ffn(x.reshape(b * s, d), norm_ffn1[layer], w_ffn1_in[layer], w_ffn1_out[layer])
        x = x.reshape(b, s, d)
        j = layer // 2
        if layer % 2 == 0:
            x = _retention(x, norm_mix[layer], ret_w_in[j], ret_w_out[j])
        else:
            x = _dsa(x, norm_mix[layer], dsa_w_in[j], dsa_q_norm[j], dsa_kv_norm[j], dsa_w_uq[j],
                     dsa_w_qidx[j], dsa_kidx_g[j], dsa_kidx_b[j], dsa_w_uk[j], dsa_w_uv[j],
                     dsa_w_out[j])
        last = layer == depth - 1
        x = _ffn(x.reshape(b * s, d), norm_ffn2[layer], w_ffn2_in[layer], w_ffn2_out[layer],
                 final_g=final_norm if last else None)
        x = x.reshape(b, s, d)
    return x
```

```python
import functools
import math

import jax
import jax.numpy as jnp
import numpy as np
from jax import lax
from jax.experimental import pallas as pl
from jax.experimental.pallas import tpu as pltpu

F32 = jnp.float32
BF16 = jnp.bfloat16

D_MODEL = 1024
DEPTH = 4
D_FF = 2816
RMS_EPS = 1e-6

RET_HEADS = 4
RET_DK = D_MODEL // RET_HEADS
RET_DV = 2 * RET_DK
RET_CHUNK = 128
ROT_BASE = 10000.0

DSA_HEADS = 8
DSA_HEAD_DIM = D_MODEL // DSA_HEADS
DSA_Q_RANK = 256
DSA_KV_RANK = 256
IDX_HEADS = 8
IDX_DIM = 64
TOPK_MAX = 256

LANES = 128
SUBLANES = 8
BF16_ROWS = 16
VMEM_LIMIT_BYTES = 56 * 1024 * 1024

FFN_ROWS = 512
FFN_CHUNK = 256
RET_ROWS = 512
PROJ_ROWS = 512
ATT_Q = LANES
ATT_K = 512
KV_ROWS = DSA_KV_RANK + BF16_ROWS
MASKED = -1e30
SEARCH_CAP = 400
REDUCE_WAYS = 4


def _resident(shape):
    nd = len(shape)
    return pl.BlockSpec(shape, lambda *_: (0,) * nd, pipeline_mode=pl.Buffered(1))


def _params(n_grid):
    return pltpu.CompilerParams(
        dimension_semantics=("arbitrary",) * n_grid, vmem_limit_bytes=VMEM_LIMIT_BYTES)


def _rms(x, g):
    y = x * lax.rsqrt(jnp.mean(x * x, axis=-1, keepdims=True) + RMS_EPS)
    return y * g


def _dot(a, b):
    return jnp.dot(a, b, preferred_element_type=F32)


def _dot_nt(a, b):
    return lax.dot_general(a, b, (((1,), (1,)), ((), ())), preferred_element_type=F32)


def _dot_tn(a, b):
    return lax.dot_general(a, b, (((0,), (0,)), ((), ())), preferred_element_type=F32)


def _ffn_kernel(x_ref, g_ref, wg_ref, wu_ref, wo_ref, *rest, final):
    if final:
        fg_ref, o_ref, acc_ref = rest
    else:
        o_ref, acc_ref = rest
    x = x_ref[...]
    h = _rms(x, g_ref[...]).astype(BF16)
    acc_ref[...] = jnp.zeros_like(acc_ref)

    def body(c, carry):
        gate = _dot(h, wg_ref[c])
        up = _dot(h, wu_ref[c])
        a = (gate * jax.nn.sigmoid(gate) * up).astype(BF16)
        acc_ref[...] += _dot(a, wo_ref[c])
        return carry

    lax.fori_loop(0, wg_ref.shape[0], body, 0)
    y = x + 0.5 * acc_ref[...]
    if final:
        y = _rms(y, fg_ref[...])
    o_ref[...] = y


def _ffn(x2, g, w_in, w_out, final_g=None):
    n, d = x2.shape
    nc = D_FF // FFN_CHUNK
    wg = w_in[:, :D_FF].reshape(d, nc, FFN_CHUNK).transpose(1, 0, 2).astype(BF16)
    wu = w_in[:, D_FF:].reshape(d, nc, FFN_CHUNK).transpose(1, 0, 2).astype(BF16)
    wo = w_out.reshape(nc, FFN_CHUNK, d).astype(BF16)
    row = pl.BlockSpec((FFN_ROWS, d), lambda i: (i, 0))
    ins = [x2, g.reshape(1, d), wg, wu, wo]
    specs = [row, _resident((1, d)), _resident(wg.shape), _resident(wu.shape), _resident(wo.shape)]
    if final_g is not None:
        ins.append(final_g.reshape(1, d))
        specs.append(_resident((1, d)))
    return pl.pallas_call(
        functools.partial(_ffn_kernel, final=final_g is not None),
        grid=(n // FFN_ROWS,),
        in_specs=specs,
        out_specs=row,
        out_shape=jax.ShapeDtypeStruct((n, d), F32),
        scratch_shapes=[pltpu.VMEM((FFN_ROWS, d), F32)],
        compiler_params=_params(1),
        name="ffn_final" if final_g is not None else "ffn",
    )(*ins)


def _ret_kernel(x_ref, g_ref, win_ref, wout_ref, cos_ref, sin_ref, inner_ref, qdec_ref, kdec_ref,
                cdec_ref, o_ref, q_s, k_s, v_s, gate_s, y_s, state_s):
    H, dk, dv, C = RET_HEADS, RET_DK, RET_DV, RET_CHUNK
    half = dk // 2

    @pl.when(pl.program_id(1) == 0)
    def _():
        state_s[...] = jnp.zeros_like(state_s)

    x = x_ref[...]
    h = _rms(x, g_ref[...]).astype(BF16)
    cos = cos_ref[...]
    sin = sin_ref[...]
    q = _dot(h, win_ref[:, 0:H * dk])
    k = _dot(h, win_ref[:, H * dk:2 * H * dk])
    for hh in range(H):
        q0 = q[:, hh * dk:hh * dk + half]
        q1 = q[:, hh * dk + half:(hh + 1) * dk]
        q_s[:, hh * dk:hh * dk + half] = (q0 * cos - q1 * sin).astype(BF16)
        q_s[:, hh * dk + half:(hh + 1) * dk] = (q1 * cos + q0 * sin).astype(BF16)
        k0 = k[:, hh * dk:hh * dk + half]
        k1 = k[:, hh * dk + half:(hh + 1) * dk]
        k_s[:, hh * dk:hh * dk + half] = (k0 * cos - k1 * sin) * (dk ** -0.5)
        k_s[:, hh * dk + half:(hh + 1) * dk] = (k1 * cos + k0 * sin) * (dk ** -0.5)
    v_s[...] = _dot(h, win_ref[:, 2 * H * dk:2 * H * dk + H * dv]).astype(BF16)
    gate_s[...] = _dot(h, win_ref[:, 2 * H * dk + H * dv:])

    def chunk(c, carry):
        r0 = pl.multiple_of(c * C, C)
        rows = pl.ds(r0, C)
        y = jnp.zeros((C, D_MODEL), F32)
        for hh in range(H):
            qc = q_s[rows, hh * dk:(hh + 1) * dk]
            kc = k_s[rows, hh * dk:(hh + 1) * dk]
            vc = v_s[rows, hh * dv:(hh + 1) * dv]
            st = state_s[hh]
            scores = _dot_nt(qc, kc.astype(BF16)) * inner_ref[hh]
            inner = _dot(scores.astype(BF16), vc)
            cross = _dot(qc, st.astype(BF16)) * qdec_ref[hh]
            kd = (kc * kdec_ref[hh]).astype(BF16)
            state_s[hh] = st * cdec_ref[hh] + _dot_tn(kd, vc)
            out = inner + cross
            out = out * lax.rsqrt(jnp.mean(out * out, axis=-1, keepdims=True) + RMS_EPS)
            gt = gate_s[rows, hh * dv:(hh + 1) * dv]
            z = (gt * jax.nn.sigmoid(gt) * out).astype(BF16)
            y = y + _dot(z, wout_ref[hh * dv:(hh + 1) * dv, :])
        y_s[rows, :] = y
        return carry

    lax.fori_loop(0, x.shape[0] // C, chunk, 0)
    o_ref[...] = x + y_s[...]


def _ret_tables():
    H, C = RET_HEADS, RET_CHUNK
    log_gamma = jnp.log(1.0 - 2.0 ** (-5.0 - jnp.arange(H, dtype=F32)))
    idx = jnp.arange(C, dtype=F32)
    diff = idx[:, None] - idx[None, :]
    inner = jnp.where(diff[None] >= 0,
                      jnp.exp(jnp.maximum(diff, 0.0)[None] * log_gamma[:, None, None]), 0.0)
    qdec = jnp.exp((idx[None, :] + 1.0) * log_gamma[:, None])[:, :, None]
    kdec = jnp.exp((C - 1.0 - idx[None, :]) * log_gamma[:, None])[:, :, None]
    cdec = jnp.exp(C * log_gamma)[:, None, None]
    return inner, qdec, kdec, cdec


def _rot_tables(s):
    pos = jnp.arange(s, dtype=F32)
    freq = 1.0 / (ROT_BASE ** jnp.linspace(0.0, 1.0, RET_DK // 2, dtype=F32))
    ang = pos[:, None] * freq[None, :]
    return jnp.cos(ang), jnp.sin(ang)


def _pair_split_columns(w, heads, dim):
    d = w.shape[0]
    w = w.reshape(d, heads, dim // 2, 2)
    return jnp.concatenate([w[..., 0], w[..., 1]], axis=-1).reshape(d, heads * dim)


def _retention(x, g, w_in, w_out):
    b, s, d = x.shape
    H, dk, dv = RET_HEADS, RET_DK, RET_DV
    tb = min(RET_ROWS, s)
    wq = _pair_split_columns(w_in[:, :H * dk], H, dk)
    wk = _pair_split_columns(w_in[:, H * dk:2 * H * dk], H, dk)
    win = jnp.concatenate([wq, wk, w_in[:, 2 * H * dk:]], axis=1).astype(BF16)
    wout = w_out.astype(BF16)
    cos, sin = _rot_tables(s)
    inner, qdec, kdec, cdec = _ret_tables()
    row = pl.BlockSpec((None, tb, d), lambda bi, i: (bi, i, 0))
    rot = pl.BlockSpec((tb, dk // 2), lambda bi, i: (i, 0))
    return pl.pallas_call(
        _ret_kernel,
        grid=(b, s // tb),
        in_specs=[row, _resident((1, d)), _resident(win.shape), _resident(wout.shape), rot, rot,
                  _resident(inner.shape), _resident(qdec.shape), _resident(kdec.shape),
                  _resident(cdec.shape)],
        out_specs=row,
        out_shape=jax.ShapeDtypeStruct((b, s, d), F32),
        scratch_shapes=[
            pltpu.VMEM((tb, H * dk), BF16),
            pltpu.VMEM((tb, H * dk), F32),
            pltpu.VMEM((tb, H * dv), BF16),
            pltpu.VMEM((tb, H * dv), F32),
            pltpu.VMEM((tb, d), F32),
            pltpu.VMEM((H, dk, dv), F32),
        ],
        compiler_params=_params(2),
        name="retention",
    )(x, g.reshape(1, d), win, wout, cos, sin, inner, qdec, kdec, cdec)


def _dsa_proj_kernel(x_ref, g_ref, win_ref, winw_ref, qn_ref, kvn_ref, kg_ref, kb_ref, wqi_ref,
                     wuq_ref, wuk_ref, qi_ref, wt_ref, ql_ref, kidx_ref, ckv_ref, ckvt_ref):
    RQ, RKV, HI, H, dh, TQ = DSA_Q_RANK, DSA_KV_RANK, IDX_HEADS, DSA_HEADS, DSA_HEAD_DIM, ATT_Q
    n_blk = x_ref.shape[0] // TQ
    h = _rms(x_ref[...], g_ref[...]).astype(BF16)
    proj = _dot(h, win_ref[...])
    c_q = _rms(proj[:, :RQ], qn_ref[...])
    c_kv = _rms(proj[:, RQ:RQ + RKV], kvn_ref[...])
    kx = proj[:, RQ + RKV:RQ + RKV + LANES]
    lane = lax.broadcasted_iota(jnp.int32, kx.shape, 1)
    mu = jnp.sum(kx, axis=-1, keepdims=True) * (1.0 / IDX_DIM)
    cen = jnp.where(lane < IDX_DIM, kx - mu, 0.0)
    var = jnp.sum(cen * cen, axis=-1, keepdims=True) * (1.0 / IDX_DIM)
    kidx_ref[...] = (cen * lax.rsqrt(var + RMS_EPS) * kg_ref[...] + kb_ref[...]).astype(BF16)
    ckv_ref[...] = c_kv.astype(BF16)
    ckvt_ref[0:RKV, :] = c_kv.T.astype(BF16)
    ones_row = lax.broadcasted_iota(jnp.int32, (BF16_ROWS, x_ref.shape[0]), 0) == 0
    ckvt_ref[RKV:, :] = jnp.where(ones_row, 1.0, 0.0).astype(BF16)
    w_t = _dot_nt(winw_ref[...], h) * (HI ** -0.5)
    cq = c_q.astype(BF16)
    qi_t = (_dot_nt(wqi_ref[...], cq) * (IDX_DIM ** -0.5)).astype(BF16)
    q = _dot(cq, wuq_ref[...]).astype(BF16)
    ql_t = [_dot_nt(wuk_ref[hh], q[:, hh * dh:(hh + 1) * dh]).astype(BF16) for hh in range(H)]
    for u in range(n_blk):
        cols = slice(u * TQ, (u + 1) * TQ)
        wt_ref[u] = w_t[0:HI, cols]
        for hh in range(HI):
            qi_ref[u, :, hh * TQ:(hh + 1) * TQ] = qi_t[hh * LANES:(hh + 1) * LANES, cols]
        for hh in range(H):
            ql_ref[u, :, hh * TQ:(hh + 1) * TQ] = ql_t[hh][:, cols]


def _dsa_proj(x2, g, w_in, q_norm, kv_norm, w_uq, w_qidx, kidx_g, kidx_b, w_uk):
    n, d = x2.shape
    RQ, RKV, HI, DI, H, TQ = DSA_Q_RANK, DSA_KV_RANK, IDX_HEADS, IDX_DIM, DSA_HEADS, ATT_Q
    pad_k = jnp.zeros((d, LANES - DI), F32)
    win = jnp.concatenate([w_in[:, :RQ + RKV + DI], pad_k], axis=1).astype(BF16)
    winw = jnp.pad(w_in[:, RQ + RKV + DI:].T, ((0, BF16_ROWS - HI), (0, 0))).astype(BF16)
    kg = jnp.pad(kidx_g, (0, LANES - DI)).reshape(1, LANES)
    kb = jnp.pad(kidx_b, (0, LANES - DI)).reshape(1, LANES)
    wqi = jnp.pad(w_qidx.reshape(RQ, HI, DI), ((0, 0), (0, 0), (0, LANES - DI)))
    wqi = wqi.reshape(RQ, HI * LANES).T.astype(BF16)
    wuq = w_uq.astype(BF16)
    wuk = w_uk.transpose(1, 0, 2).astype(BF16)
    tm = min(PROJ_ROWS, n)
    nb = tm // TQ
    row = lambda w: pl.BlockSpec((tm, w), lambda i: (i, 0))
    blk = lambda r, c: pl.BlockSpec((nb, r, c), lambda i: (i, 0, 0))
    return pl.pallas_call(
        _dsa_proj_kernel,
        grid=(n // tm,),
        in_specs=[row(d), _resident((1, d)), _resident(win.shape), _resident(winw.shape),
                  _resident((1, RQ)), _resident((1, RKV)), _resident((1, LANES)),
                  _resident((1, LANES)), _resident(wqi.shape), _resident(wuq.shape),
                  _resident(wuk.shape)],
        out_specs=[blk(LANES, HI * TQ), blk(HI, TQ), blk(RKV, H * TQ), row(LANES), row(RKV),
                   pl.BlockSpec((KV_ROWS, tm), lambda i: (0, i))],
        out_shape=[
            jax.ShapeDtypeStruct((n // TQ, LANES, HI * TQ), BF16),
            jax.ShapeDtypeStruct((n // TQ, HI, TQ), F32),
            jax.ShapeDtypeStruct((n // TQ, RKV, H * TQ), BF16),
            jax.ShapeDtypeStruct((n, LANES), BF16),
            jax.ShapeDtypeStruct((n, RKV), BF16),
            jax.ShapeDtypeStruct((KV_ROWS, n), BF16),
        ],
        compiler_params=_params(1),
        name="dsa_proj",
    )(x2, g.reshape(1, d), win, winw, q_norm.reshape(1, RQ), kv_norm.reshape(1, RKV), kg, kb, wqi,
      wuq, wuk)


def _fold(t, op):
    groups = t.shape[0] // SUBLANES
    group = lambda r: t[r * SUBLANES:(r + 1) * SUBLANES, :]
    acc = [group(w) for w in range(REDUCE_WAYS)]
    for r in range(REDUCE_WAYS, groups, REDUCE_WAYS):
        acc = [op(acc[w], group(r + w)) for w in range(REDUCE_WAYS)]
    while len(acc) > 1:
        acc = [op(acc[2 * w], acc[2 * w + 1]) for w in range(len(acc) // 2)]
    return acc[0]


def _dsa_attn_kernel(x_ref, qi_ref, wt_ref, ql_ref, kidx_ref, ckv_ref, ckvt_ref, wuvt_ref,
                     woutt_ref, o_ref, sc_s, p_s, acc_s, m_s, *, top_k):
    TQ, TK = ATT_Q, ATT_K
    HI, H, RKV = IDX_HEADS, DSA_HEADS, DSA_KV_RANK
    i = pl.program_id(1)
    n_tiles = (i * TQ + TQ + TK - 1) // TK
    qpos = i * TQ + lax.broadcasted_iota(jnp.int32, (1, TQ), 1)
    key0 = lax.broadcasted_iota(jnp.int32, (TK, 1), 0)
    neg_inf = jnp.float32(-jnp.inf)
    kf = jnp.float32(top_k)

    def tile_rows(j):
        return pl.ds(pl.multiple_of(j * TK, TK), TK)

    w_t = wt_ref[...]

    def score_tile(j, carry):
        rmax, rmin, c_ge0, c_gt0 = carry
        dots = _dot(kidx_ref[tile_rows(j), :], qi_ref[...])
        s = jnp.maximum(dots[:, 0:TQ], 0.0) * w_t[0:1, :]
        for hh in range(1, HI):
            s = s + jnp.maximum(dots[:, hh * TQ:(hh + 1) * TQ], 0.0) * w_t[hh:hh + 1, :]
        causal = key0 + j * TK <= qpos
        s = jnp.where(causal, s, neg_inf)
        sc_s[tile_rows(j), :] = s
        rmax = jnp.maximum(rmax, _fold(s, jnp.maximum))
        rmin = jnp.minimum(rmin, _fold(jnp.where(causal, s, jnp.inf), jnp.minimum))
        c_ge0 = c_ge0 + _fold(jnp.where(s >= 0.0, 1.0, 0.0), jnp.add)
        c_gt0 = c_gt0 + _fold(jnp.where(s > 0.0, 1.0, 0.0), jnp.add)
        return rmax, rmin, c_ge0, c_gt0

    part = lambda v: jnp.full((SUBLANES, TQ), v, F32)
    rmax, rmin, c_ge0, c_gt0 = lax.fori_loop(
        0, n_tiles, score_tile, (part(-jnp.inf), part(jnp.inf), part(0.0), part(0.0)))
    rmax = jnp.max(rmax, axis=0, keepdims=True)
    rmin = jnp.min(rmin, axis=0, keepdims=True)
    c_ge0 = jnp.sum(c_ge0, axis=0, keepdims=True)
    c_gt0 = jnp.sum(c_gt0, axis=0, keepdims=True)

    def count(pred):
        def body(j, acc):
            return acc + _fold(jnp.where(pred(sc_s[tile_rows(j), :], j), 1.0, 0.0), jnp.add)
        return jnp.sum(lax.fori_loop(0, n_tiles, body, part(0.0)), axis=0, keepdims=True)

    n_valid = (qpos + 1).astype(F32)
    all_sel = n_valid <= kf
    zero_tie = (c_gt0 < kf) & (c_ge0 >= kf)
    positive = c_gt0 >= kf
    lo = jnp.where(positive | zero_tie, 0.0, rmin)
    hi = jnp.where(positive, 2.0 * rmax, 0.0)
    clo = jnp.where(positive | zero_tie, c_ge0, n_valid)
    chi = jnp.where(positive, 0.0, c_ge0)
    lo = jnp.where(all_sel, jnp.float32(jnp.finfo(jnp.float32).min), lo)
    clo = jnp.where(all_sel, n_valid, clo)
    done = all_sel | zero_tie | (clo == kf)
    one = jnp.ones((1, TQ), F32)
    log_target = math.log(top_k + 0.5)

    def pending(done):
        return jnp.sum(jnp.where(done, 0, 1))

    def search_cond(c):
        return (c[1] > 0) & (c[0] < SEARCH_CAP)

    def search_body(c):
        it, _, lo, hi, clo, chi, wl, wh, side, done_f = c
        done = done_f > 0.0
        mid = 0.5 * lo + 0.5 * hi
        fa = (jnp.log(clo) - log_target) * wl
        fb = (log_target - jnp.log(jnp.maximum(chi, 0.5))) * wh
        cand = lo + (hi - lo) * (fa / (fa + fb))
        cand = jnp.where((cand > lo) & (cand < hi), cand, mid)
        collapsed = (cand <= lo) | (cand >= hi)
        cm = count(lambda t, j: t >= cand)
        move = jnp.logical_not(done | collapsed)
        up = move & (cm >= kf)
        down = move & (cm < kf)
        wh = jnp.where(up, jnp.where(side > 0.0, 0.5 * wh, 1.0), jnp.where(down, 1.0, wh))
        wl = jnp.where(down, jnp.where(side < 0.0, 0.5 * wl, 1.0), jnp.where(up, 1.0, wl))
        side = jnp.where(up, 1.0, jnp.where(down, -1.0, side))
        lo = jnp.where(up, cand, lo)
        clo = jnp.where(up, cm, clo)
        hi = jnp.where(down, cand, hi)
        chi = jnp.where(down, cm, chi)
        done = done | collapsed | (clo == kf)
        return it + 1, pending(done), lo, hi, clo, chi, wl, wh, side, jnp.where(done, 1.0, 0.0)

    res = lax.while_loop(search_cond, search_body,
                         (jnp.int32(0), pending(done), lo, hi, clo, chi, one, one, 0.0 * one,
                          jnp.where(done, 1.0, 0.0)))
    tau, clo = res[2], res[4]

    excess = clo > kf

    @pl.when(jnp.sum(jnp.where(excess, 1, 0)) > 0)
    def _():
        need = kf - count(lambda t, j: t > tau)
        n_steps = max(1, math.ceil(math.log2(sc_s.shape[0]))) + 1

        def step(_, jb):
            j_lo, j_hi = jb
            j_mid = (j_lo + j_hi) >> 1
            c = count(lambda t, j: (t == tau) & (key0 + j * TK <= j_mid))
            ok = c >= need
            return jnp.where(ok, j_lo, j_mid), jnp.where(ok, j_mid, j_hi)

        _, j_cut = lax.fori_loop(0, n_steps, step, (jnp.full((1, TQ), -1, jnp.int32), qpos))

        def fix(j, carry):
            t = sc_s[tile_rows(j), :]
            drop = excess & (t == tau) & (key0 + j * TK > j_cut)
            sc_s[tile_rows(j), :] = jnp.where(drop, neg_inf, t)
            return carry

        lax.fori_loop(0, n_tiles, fix, 0)

    c_exp = (DSA_HEAD_DIM ** -0.5) * math.log2(math.e)
    m_s[...] = jnp.full_like(m_s, MASKED)
    acc_s[...] = jnp.zeros_like(acc_s)

    def attn_tile(j, carry):
        logits = _dot(ckv_ref[tile_rows(j), :], ql_ref[...])
        sel = sc_s[tile_rows(j), :] >= tau
        alphas = []
        for hh in range(H):
            cols = slice(hh * TQ, (hh + 1) * TQ)
            lg = jnp.where(sel, logits[:, cols], MASKED)
            m_old = m_s[hh:hh + 1, :]
            m_new = jnp.maximum(m_old, jnp.max(_fold(lg, jnp.maximum), axis=0, keepdims=True))
            m_s[hh:hh + 1, :] = m_new
            alphas.append(jnp.exp2((m_old - m_new) * c_exp))
            p_s[:, cols] = jnp.exp2((lg - m_new) * c_exp).astype(BF16)
        alpha = jnp.concatenate(alphas, axis=1)
        cols_j = pl.ds(pl.multiple_of(j * TK, TK), TK)
        acc_s[...] = acc_s[...] * alpha + _dot(ckvt_ref[:, cols_j], p_s[...])
        return carry

    lax.fori_loop(0, n_tiles, attn_tile, 0)

    o_lat = (acc_s[0:RKV, :] / acc_s[RKV:RKV + 1, :]).astype(BF16)
    o_t = jnp.concatenate(
        [_dot(wuvt_ref[hh], o_lat[:, hh * TQ:(hh + 1) * TQ]) for hh in range(H)], axis=0)
    out_t = _dot(woutt_ref[...], o_t.astype(BF16))
    o_ref[...] = x_ref[...] + out_t.T


def _dsa_attn(x, qi, wt, ql, kidx, ckv, ckvt, w_uv, w_out):
    b, s, d = x.shape
    H, RKV, HI, TQ = DSA_HEADS, DSA_KV_RANK, IDX_HEADS, ATT_Q
    nq = s // TQ
    top_k = min(TOPK_MAX, s // 4)
    wuvt = w_uv.transpose(1, 2, 0).astype(BF16)
    woutt = w_out.T.astype(BF16)
    qrow = pl.BlockSpec((None, TQ, d), lambda bi, i: (bi, i, 0))
    qblk = lambda r, c: pl.BlockSpec((None, r, c), lambda bi, i: (bi * nq + i, 0, 0))
    seq = lambda width: pl.BlockSpec((s, width), lambda bi, i: (bi, 0))
    return pl.pallas_call(
        functools.partial(_dsa_attn_kernel, top_k=top_k),
        grid=(b, nq),
        in_specs=[qrow, qblk(LANES, HI * TQ), qblk(HI, TQ), qblk(RKV, H * TQ), seq(LANES), seq(RKV),
                  pl.BlockSpec((KV_ROWS, s), lambda bi, i: (0, bi)),
                  _resident(wuvt.shape), _resident(woutt.shape)],
        out_specs=qrow,
        out_shape=jax.ShapeDtypeStruct((b, s, d), F32),
        scratch_shapes=[
            pltpu.VMEM((s, TQ), F32),
            pltpu.VMEM((ATT_K, H * TQ), BF16),
            pltpu.VMEM((KV_ROWS, H * TQ), F32),
            pltpu.VMEM((H, TQ), F32),
        ],
        compiler_params=_params(2),
        name="dsa_attn",
    )(x, qi, wt, ql, kidx, ckv, ckvt, wuvt, woutt)


def _dsa(x, g, w_in, q_norm, kv_norm, w_uq, w_qidx, kidx_g, kidx_b, w_uk, w_uv, w_out):
    b, s, d = x.shape
    qi, wt, ql, kidx, ckv, ckvt = _dsa_proj(x.reshape(b * s, d), g, w_in, q_norm, kv_norm, w_uq,
                                            w_qidx, kidx_g, kidx_b, w_uk)
    return _dsa_attn(x, qi, wt, ql, kidx, ckv, ckvt, w_uv, w_out)


def kernel(x, norm_ffn1, w_ffn1_in, w_ffn1_out, norm_mix, norm_ffn2, w_ffn2_in, w_ffn2_out, ret_w_in,
           ret_w_out, dsa_w_in, dsa_q_norm, dsa_kv_norm, dsa_w_uq, dsa_w_qidx, dsa_kidx_g, dsa_kidx_b,
           dsa_w_uk, dsa_w_uv, dsa_w_out, final_norm):
    b, s, d = x.shape
    depth = norm_ffn1.shape[0]
    for layer in range(depth):
        x = _ffn(x.reshape(b * s, d), norm_ffn1[layer], w_ffn1_in[layer], w_ffn1_out[layer])
        x = x.reshape(b, s, d)
        j = layer // 2
        if layer % 2 == 0:
            x = _retention(x, norm_mix[layer], ret_w_in[j], ret_w_out[j])
        else:
            x = _dsa(x, norm_mix[layer], dsa_w_in[j], dsa_q_norm[j], dsa_kv_norm[j], dsa_w_uq[j],
                     dsa_w_qidx[j], dsa_kidx_g[j], dsa_kidx_b[j], dsa_w_uk[j], dsa_w_uv[j],
                     dsa_w_out[j])
        last = layer == depth - 1
        x = _ffn(x.reshape(b * s, d), norm_ffn2[layer], w_ffn2_in[layer], w_ffn2_out[layer],
                 final_g=final_norm if last else None)
        x = x.reshape(b, s, d)
    return x
```

```python
import functools
import math

import jax
import jax.numpy as jnp
import numpy as np
from jax import lax
from jax.experimental import pallas as pl
from jax.experimental.pallas import tpu as pltpu

F32 = jnp.float32
BF16 = jnp.bfloat16

D_MODEL = 1024
DEPTH = 4
D_FF = 2816
RMS_EPS = 1e-6

RET_HEADS = 4
RET_DK = D_MODEL // RET_HEADS
RET_DV = 2 * RET_DK
RET_CHUNK = 128
ROT_BASE = 10000.0

DSA_HEADS = 8
DSA_HEAD_DIM = D_MODEL // DSA_HEADS
DSA_Q_RANK = 256
DSA_KV_RANK = 256
IDX_HEADS = 8
IDX_DIM = 64
TOPK_MAX = 256

LANES = 128
SUBLANES = 8
BF16_ROWS = 16
VMEM_LIMIT_BYTES = 56 * 1024 * 1024

FFN_ROWS = 512
FFN_CHUNK = 256
RET_ROWS = 512
PROJ_ROWS = 512
ATT_Q = LANES
ATT_K = 512
KV_ROWS = DSA_KV_RANK + BF16_ROWS
MASKED = -1e30
SEARCH_CAP = 400
SEARCH_UNCHECKED = 10
DENOM_FLOOR = 2.0 ** -80
REDUCE_WAYS = 4


def _resident(shape):
    nd = len(shape)
    return pl.BlockSpec(shape, lambda *_: (0,) * nd, pipeline_mode=pl.Buffered(1))


def _params(n_grid, flags=None):
    return pltpu.CompilerParams(
        dimension_semantics=("arbitrary",) * n_grid, vmem_limit_bytes=VMEM_LIMIT_BYTES, flags=flags)


def _rms(x, g):
    y = x * lax.rsqrt(jnp.mean(x * x, axis=-1, keepdims=True) + RMS_EPS)
    return y * g


def _dot(a, b):
    return jnp.dot(a, b, preferred_element_type=F32)


def _dot_nt(a, b):
    return lax.dot_general(a, b, (((1,), (1,)), ((), ())), preferred_element_type=F32)


def _dot_tn(a, b):
    return lax.dot_general(a, b, (((0,), (0,)), ((), ())), preferred_element_type=F32)


def _ffn_kernel(x_ref, g_ref, wg_ref, wu_ref, wo_ref, *rest, final):
    if final:
        fg_ref, o_ref, acc_ref = rest
    else:
        o_ref, acc_ref = rest
    x = x_ref[...]
    h = _rms(x, g_ref[...]).astype(BF16)
    acc_ref[...] = jnp.zeros_like(acc_ref)

    def body(c, carry):
        gate = _dot(h, wg_ref[c])
        up = _dot(h, wu_ref[c])
        a = (gate * jax.nn.sigmoid(gate) * up).astype(BF16)
        acc_ref[...] += _dot(a, wo_ref[c])
        return carry

    lax.fori_loop(0, wg_ref.shape[0], body, 0)
    y = x + 0.5 * acc_ref[...]
    if final:
        y = _rms(y, fg_ref[...])
    o_ref[...] = y


def _ffn(x2, g, w_in, w_out, final_g=None):
    n, d = x2.shape
    nc = D_FF // FFN_CHUNK
    wg = w_in[:, :D_FF].reshape(d, nc, FFN_CHUNK).transpose(1, 0, 2).astype(BF16)
    wu = w_in[:, D_FF:].reshape(d, nc, FFN_CHUNK).transpose(1, 0, 2).astype(BF16)
    wo = w_out.reshape(nc, FFN_CHUNK, d).astype(BF16)
    row = pl.BlockSpec((FFN_ROWS, d), lambda i: (i, 0))
    ins = [x2, g.reshape(1, d), wg, wu, wo]
    specs = [row, _resident((1, d)), _resident(wg.shape), _resident(wu.shape), _resident(wo.shape)]
    if final_g is not None:
        ins.append(final_g.reshape(1, d))
        specs.append(_resident((1, d)))
    return pl.pallas_call(
        functools.partial(_ffn_kernel, final=final_g is not None),
        grid=(n // FFN_ROWS,),
        in_specs=specs,
        out_specs=row,
        out_shape=jax.ShapeDtypeStruct((n, d), F32),
        scratch_shapes=[pltpu.VMEM((FFN_ROWS, d), F32)],
        compiler_params=_params(1),
        name="ffn_final" if final_g is not None else "ffn",
    )(*ins)


def _ret_kernel(x_ref, g_ref, win_ref, wout_ref, cos_ref, sin_ref, inner_ref, qdec_ref, kdec_ref,
                cdec_ref, o_ref, q_s, k_s, v_s, gate_s, y_s, state_s):
    H, dk, dv, C = RET_HEADS, RET_DK, RET_DV, RET_CHUNK
    half = dk // 2

    @pl.when(pl.program_id(1) == 0)
    def _():
        state_s[...] = jnp.zeros_like(state_s)

    x = x_ref[...]
    h = _rms(x, g_ref[...]).astype(BF16)
    cos = cos_ref[...]
    sin = sin_ref[...]
    q = _dot(h, win_ref[:, 0:H * dk])
    k = _dot(h, win_ref[:, H * dk:2 * H * dk])
    for hh in range(H):
        q0 = q[:, hh * dk:hh * dk + half]
        q1 = q[:, hh * dk + half:(hh + 1) * dk]
        q_s[:, hh * dk:hh * dk + half] = (q0 * cos - q1 * sin).astype(BF16)
        q_s[:, hh * dk + half:(hh + 1) * dk] = (q1 * cos + q0 * sin).astype(BF16)
        k0 = k[:, hh * dk:hh * dk + half]
        k1 = k[:, hh * dk + half:(hh + 1) * dk]
        k_s[:, hh * dk:hh * dk + half] = (k0 * cos - k1 * sin) * (dk ** -0.5)
        k_s[:, hh * dk + half:(hh + 1) * dk] = (k1 * cos + k0 * sin) * (dk ** -0.5)
    v_s[...] = _dot(h, win_ref[:, 2 * H * dk:2 * H * dk + H * dv]).astype(BF16)
    gate_s[...] = _dot(h, win_ref[:, 2 * H * dk + H * dv:])

    def chunk(c, carry):
        r0 = pl.multiple_of(c * C, C)
        rows = pl.ds(r0, C)
        y = jnp.zeros((C, D_MODEL), F32)
        for hh in range(H):
            qc = q_s[rows, hh * dk:(hh + 1) * dk]
            kc = k_s[rows, hh * dk:(hh + 1) * dk]
            vc = v_s[rows, hh * dv:(hh + 1) * dv]
            st = state_s[hh]
            scores = _dot_nt(qc, kc.astype(BF16)) * inner_ref[hh]
            inner = _dot(scores.astype(BF16), vc)
            cross = _dot(qc, st.astype(BF16)) * qdec_ref[hh]
            kd = (kc * kdec_ref[hh]).astype(BF16)
            state_s[hh] = st * cdec_ref[hh] + _dot_tn(kd, vc)
            out = inner + cross
            out = out * lax.rsqrt(jnp.mean(out * out, axis=-1, keepdims=True) + RMS_EPS)
            gt = gate_s[rows, hh * dv:(hh + 1) * dv]
            z = (gt * jax.nn.sigmoid(gt) * out).astype(BF16)
            y = y + _dot(z, wout_ref[hh * dv:(hh + 1) * dv, :])
        y_s[rows, :] = y
        return carry

    lax.fori_loop(0, x.shape[0] // C, chunk, 0)
    o_ref[...] = x + y_s[...]


def _ret_tables():
    H, C = RET_HEADS, RET_CHUNK
    log_gamma = jnp.log(1.0 - 2.0 ** (-5.0 - jnp.arange(H, dtype=F32)))
    idx = jnp.arange(C, dtype=F32)
    diff = idx[:, None] - idx[None, :]
    inner = jnp.where(diff[None] >= 0,
                      jnp.exp(jnp.maximum(diff, 0.0)[None] * log_gamma[:, None, None]), 0.0)
    qdec = jnp.exp((idx[None, :] + 1.0) * log_gamma[:, None])[:, :, None]
    kdec = jnp.exp((C - 1.0 - idx[None, :]) * log_gamma[:, None])[:, :, None]
    cdec = jnp.exp(C * log_gamma)[:, None, None]
    return inner, qdec, kdec, cdec


def _rot_tables(s):
    pos = jnp.arange(s, dtype=F32)
    freq = 1.0 / (ROT_BASE ** jnp.linspace(0.0, 1.0, RET_DK // 2, dtype=F32))
    ang = pos[:, None] * freq[None, :]
    return jnp.cos(ang), jnp.sin(ang)


def _pair_split_columns(w, heads, dim):
    d = w.shape[0]
    w = w.reshape(d, heads, dim // 2, 2)
    return jnp.concatenate([w[..., 0], w[..., 1]], axis=-1).reshape(d, heads * dim)


def _retention(x, g, w_in, w_out):
    b, s, d = x.shape
    H, dk, dv = RET_HEADS, RET_DK, RET_DV
    tb = min(RET_ROWS, s)
    wq = _pair_split_columns(w_in[:, :H * dk], H, dk)
    wk = _pair_split_columns(w_in[:, H * dk:2 * H * dk], H, dk)
    win = jnp.concatenate([wq, wk, w_in[:, 2 * H * dk:]], axis=1).astype(BF16)
    wout = w_out.astype(BF16)
    cos, sin = _rot_tables(s)
    inner, qdec, kdec, cdec = _ret_tables()
    row = pl.BlockSpec((None, tb, d), lambda bi, i: (bi, i, 0))
    rot = pl.BlockSpec((tb, dk // 2), lambda bi, i: (i, 0))
    return pl.pallas_call(
        _ret_kernel,
        grid=(b, s // tb),
        in_specs=[row, _resident((1, d)), _resident(win.shape), _resident(wout.shape), rot, rot,
                  _resident(inner.shape), _resident(qdec.shape), _resident(kdec.shape),
                  _resident(cdec.shape)],
        out_specs=row,
        out_shape=jax.ShapeDtypeStruct((b, s, d), F32),
        scratch_shapes=[
            pltpu.VMEM((tb, H * dk), BF16),
            pltpu.VMEM((tb, H * dk), F32),
            pltpu.VMEM((tb, H * dv), BF16),
            pltpu.VMEM((tb, H * dv), F32),
            pltpu.VMEM((tb, d), F32),
            pltpu.VMEM((H, dk, dv), F32),
        ],
        compiler_params=_params(2),
        name="retention",
    )(x, g.reshape(1, d), win, wout, cos, sin, inner, qdec, kdec, cdec)


def _dsa_proj_kernel(x_ref, g_ref, win_ref, winw_ref, qn_ref, kvn_ref, kg_ref, kb_ref, wqi_ref,
                     wuq_ref, wuk_ref, qi_ref, wt_ref, ql_ref, kidx_ref, ckv_ref, ckvt_ref):
    RQ, RKV, HI, H, dh, TQ = DSA_Q_RANK, DSA_KV_RANK, IDX_HEADS, DSA_HEADS, DSA_HEAD_DIM, ATT_Q
    n_blk = x_ref.shape[0] // TQ
    h = _rms(x_ref[...], g_ref[...]).astype(BF16)
    proj = _dot(h, win_ref[...])
    c_q = _rms(proj[:, :RQ], qn_ref[...])
    c_kv = _rms(proj[:, RQ:RQ + RKV], kvn_ref[...])
    kx = proj[:, RQ + RKV:RQ + RKV + LANES]
    lane = lax.broadcasted_iota(jnp.int32, kx.shape, 1)
    mu = jnp.sum(kx, axis=-1, keepdims=True) * (1.0 / IDX_DIM)
    cen = jnp.where(lane < IDX_DIM, kx - mu, 0.0)
    var = jnp.sum(cen * cen, axis=-1, keepdims=True) * (1.0 / IDX_DIM)
    kidx_ref[...] = (cen * lax.rsqrt(var + RMS_EPS) * kg_ref[...] + kb_ref[...]).astype(BF16)
    ckv_ref[...] = c_kv.astype(BF16)
    ckvt_ref[0:RKV, :] = c_kv.T.astype(BF16)
    ones_row = lax.broadcasted_iota(jnp.int32, (BF16_ROWS, x_ref.shape[0]), 0) == 0
    ckvt_ref[RKV:, :] = jnp.where(ones_row, 1.0, 0.0).astype(BF16)
    w_t = _dot_nt(winw_ref[...], h) * (HI ** -0.5)
    cq = c_q.astype(BF16)
    qi_t = (_dot_nt(wqi_ref[...], cq) * (IDX_DIM ** -0.5)).astype(BF16)
    q = _dot(cq, wuq_ref[...]).astype(BF16)
    ql_t = [_dot_nt(wuk_ref[hh], q[:, hh * dh:(hh + 1) * dh]).astype(BF16) for hh in range(H)]
    for u in range(n_blk):
        cols = slice(u * TQ, (u + 1) * TQ)
        wt_ref[u] = w_t[0:HI, cols]
        for hh in range(HI):
            qi_ref[u, :, hh * TQ:(hh + 1) * TQ] = qi_t[hh * LANES:(hh + 1) * LANES, cols]
        for hh in range(H):
            ql_ref[u, :, hh * TQ:(hh + 1) * TQ] = ql_t[hh][:, cols]


def _dsa_proj(x2, g, w_in, q_norm, kv_norm, w_uq, w_qidx, kidx_g, kidx_b, w_uk):
    n, d = x2.shape
    RQ, RKV, HI, DI, H, TQ = DSA_Q_RANK, DSA_KV_RANK, IDX_HEADS, IDX_DIM, DSA_HEADS, ATT_Q
    pad_k = jnp.zeros((d, LANES - DI), F32)
    win = jnp.concatenate([w_in[:, :RQ + RKV + DI], pad_k], axis=1).astype(BF16)
    winw = jnp.pad(w_in[:, RQ + RKV + DI:].T, ((0, BF16_ROWS - HI), (0, 0))).astype(BF16)
    kg = jnp.pad(kidx_g, (0, LANES - DI)).reshape(1, LANES)
    kb = jnp.pad(kidx_b, (0, LANES - DI)).reshape(1, LANES)
    wqi = jnp.pad(w_qidx.reshape(RQ, HI, DI), ((0, 0), (0, 0), (0, LANES - DI)))
    wqi = wqi.reshape(RQ, HI * LANES).T.astype(BF16)
    wuq = w_uq.astype(BF16)
    wuk = w_uk.transpose(1, 0, 2).astype(BF16)
    tm = min(PROJ_ROWS, n)
    nb = tm // TQ
    row = lambda w: pl.BlockSpec((tm, w), lambda i: (i, 0))
    blk = lambda r, c: pl.BlockSpec((nb, r, c), lambda i: (i, 0, 0))
    return pl.pallas_call(
        _dsa_proj_kernel,
        grid=(n // tm,),
        in_specs=[row(d), _resident((1, d)), _resident(win.shape), _resident(winw.shape),
                  _resident((1, RQ)), _resident((1, RKV)), _resident((1, LANES)),
                  _resident((1, LANES)), _resident(wqi.shape), _resident(wuq.shape),
                  _resident(wuk.shape)],
        out_specs=[blk(LANES, HI * TQ), blk(HI, TQ), blk(RKV, H * TQ), row(LANES), row(RKV),
                   pl.BlockSpec((KV_ROWS, tm), lambda i: (0, i))],
        out_shape=[
            jax.ShapeDtypeStruct((n // TQ, LANES, HI * TQ), BF16),
            jax.ShapeDtypeStruct((n // TQ, HI, TQ), F32),
            jax.ShapeDtypeStruct((n // TQ, RKV, H * TQ), BF16),
            jax.ShapeDtypeStruct((n, LANES), BF16),
            jax.ShapeDtypeStruct((n, RKV), BF16),
            jax.ShapeDtypeStruct((KV_ROWS, n), BF16),
        ],
        compiler_params=_params(1),
        name="dsa_proj",
    )(x2, g.reshape(1, d), win, winw, q_norm.reshape(1, RQ), kv_norm.reshape(1, RKV), kg, kb, wqi,
      wuq, wuk)


def _fold(t, op):
    groups = t.shape[0] // SUBLANES
    group = lambda r: t[r * SUBLANES:(r + 1) * SUBLANES, :]
    acc = [group(w) for w in range(REDUCE_WAYS)]
    for r in range(REDUCE_WAYS, groups, REDUCE_WAYS):
        acc = [op(acc[w], group(r + w)) for w in range(REDUCE_WAYS)]
    while len(acc) > 1:
        acc = [op(acc[2 * w], acc[2 * w + 1]) for w in range(len(acc) // 2)]
    return acc[0]


def _dsa_attn_kernel(x_ref, qi_ref, wt_ref, ql_ref, kidx_ref, ckv_ref, ckvt_ref, wuvt_ref,
                     woutt_ref, cmax_ref, o_ref, sc_s, lg_a, lg_b, p_a, p_b, bias_s, acc_s, m_s, *, top_k):
    TQ, TK, TH = ATT_Q, ATT_K, ATT_K // 2
    HI, H, RKV = IDX_HEADS, DSA_HEADS, DSA_KV_RANK
    i = pl.program_id(1)
    n_tiles = (i * TQ + TQ + TK - 1) // TK
    qpos = i * TQ + lax.broadcasted_iota(jnp.int32, (1, TQ), 1)
    key0 = lax.broadcasted_iota(jnp.int32, (TK, 1), 0)
    keyh = lax.broadcasted_iota(jnp.int32, (TH, 1), 0)
    neg_inf = jnp.float32(-jnp.inf)
    kf = jnp.float32(top_k)

    def tile_rows(j):
        return pl.ds(pl.multiple_of(j * TK, TK), TK)

    w_t = wt_ref[...]
    last_half = 2 * n_tiles - 1

    def half_rows(jh):
        return pl.ds(pl.multiple_of(jh * TH, TH), TH)

    def score_dots(jh, buf):
        buf[...] = _dot(kidx_ref[half_rows(jh), :], qi_ref[...])

    def score_reduce(jh, buf, carry):
        rmax, rmin, c_ge0, c_gt0 = carry
        s = jnp.maximum(buf[:, 0:TQ], 0.0) * w_t[0:1, :]
        for hh in range(1, HI):
            s = s + jnp.maximum(buf[:, hh * TQ:(hh + 1) * TQ], 0.0) * w_t[hh:hh + 1, :]
        causal = keyh + jh * TH <= qpos
        s = jnp.where(causal, s, neg_inf)
        sc_s[half_rows(jh), :] = s
        rmax = jnp.maximum(rmax, _fold(s, jnp.maximum))
        rmin = jnp.minimum(rmin, _fold(jnp.where(causal, s, jnp.inf), jnp.minimum))
        c_ge0 = c_ge0 + _fold(jnp.where(s >= 0.0, 1.0, 0.0), jnp.add)
        c_gt0 = c_gt0 + _fold(jnp.where(s > 0.0, 1.0, 0.0), jnp.add)
        return rmax, rmin, c_ge0, c_gt0

    def score_pair(j, carry):
        score_dots(2 * j + 1, lg_b)
        carry = score_reduce(2 * j, lg_a, carry)
        score_dots(jnp.minimum(2 * j + 2, last_half), lg_a)
        return score_reduce(2 * j + 1, lg_b, carry)

    part = lambda v: jnp.full((SUBLANES, TQ), v, F32)
    score_dots(0, lg_a)
    rmax, rmin, c_ge0, c_gt0 = lax.fori_loop(
        0, n_tiles, score_pair, (part(-jnp.inf), part(jnp.inf), part(0.0), part(0.0)))
    rmax = jnp.max(rmax, axis=0, keepdims=True)
    rmin = jnp.min(rmin, axis=0, keepdims=True)
    c_ge0 = jnp.sum(c_ge0, axis=0, keepdims=True)
    c_gt0 = jnp.sum(c_gt0, axis=0, keepdims=True)

    def count(pred):
        def body(j, acc):
            return acc + _fold(jnp.where(pred(sc_s[tile_rows(j), :], j), 1.0, 0.0), jnp.add)
        return jnp.sum(lax.fori_loop(0, n_tiles, body, part(0.0)), axis=0, keepdims=True)

    n_valid = (qpos + 1).astype(F32)
    all_sel = n_valid <= kf
    zero_tie = (c_gt0 < kf) & (c_ge0 >= kf)
    positive = c_gt0 >= kf
    lo = jnp.where(positive | zero_tie, 0.0, rmin)
    hi = jnp.where(positive, 2.0 * rmax, 0.0)
    clo = jnp.where(positive | zero_tie, c_ge0, n_valid)
    chi = jnp.where(positive, 0.0, c_ge0)
    lo = jnp.where(all_sel, jnp.float32(jnp.finfo(jnp.float32).min), lo)
    clo = jnp.where(all_sel, n_valid, clo)
    done = all_sel | zero_tie | (clo == kf)
    one = jnp.ones((1, TQ), F32)
    log_target = math.log(top_k + 0.5)

    def pending(done):
        return jnp.sum(jnp.where(done, 0, 1))

    def search_cond(c):
        return (c[1] > 0) & (c[0] < SEARCH_CAP)

    def search_step(state):
        lo, hi, clo, chi, wl, wh, side, done_f = state
        done = done_f > 0.0
        mid = 0.5 * lo + 0.5 * hi
        fa = (jnp.log(clo) - log_target) * wl
        fb = (log_target - jnp.log(jnp.maximum(chi, 0.5))) * wh
        cand = lo + (hi - lo) * (fa / (fa + fb))
        cand = jnp.where((cand > lo) & (cand < hi), cand, mid)
        collapsed = (cand <= lo) | (cand >= hi)
        cm = count(lambda t, j: t >= cand)
        move = jnp.logical_not(done | collapsed)
        up = move & (cm >= kf)
        down = move & (cm < kf)
        wh = jnp.where(up, jnp.where(side > 0.0, 0.5 * wh, 1.0), jnp.where(down, 1.0, wh))
        wl = jnp.where(down, jnp.where(side < 0.0, 0.5 * wl, 1.0), jnp.where(up, 1.0, wl))
        side = jnp.where(up, 1.0, jnp.where(down, -1.0, side))
        lo = jnp.where(up, cand, lo)
        clo = jnp.where(up, cm, clo)
        hi = jnp.where(down, cand, hi)
        chi = jnp.where(down, cm, chi)
        done = done | collapsed | (clo == kf)
        return lo, hi, clo, chi, wl, wh, side, jnp.where(done, 1.0, 0.0)

    def search_body(c):
        state = search_step(c[2:])
        return (c[0] + 1, pending(state[-1] > 0.0)) + state

    state = (lo, hi, clo, chi, one, one, 0.0 * one, jnp.where(done, 1.0, 0.0))
    state = lax.fori_loop(0, SEARCH_UNCHECKED, lambda _, s: search_step(s), state)
    res = lax.while_loop(search_cond, search_body,
                         (jnp.int32(0), pending(state[-1] > 0.0)) + state)
    tau, clo = res[2], res[4]

    excess = clo > kf

    @pl.when(jnp.sum(jnp.where(excess, 1, 0)) > 0)
    def _():
        need = kf - count(lambda t, j: t > tau)
        n_steps = max(1, math.ceil(math.log2(sc_s.shape[0]))) + 1

        def step(_, jb):
            j_lo, j_hi = jb
            j_mid = (j_lo + j_hi) >> 1
            c = count(lambda t, j: (t == tau) & (key0 + j * TK <= j_mid))
            ok = c >= need
            return jnp.where(ok, j_lo, j_mid), jnp.where(ok, j_mid, j_hi)

        _, j_cut = lax.fori_loop(0, n_steps, step, (jnp.full((1, TQ), -1, jnp.int32), qpos))

        def fix(j, carry):
            t = sc_s[tile_rows(j), :]
            drop = excess & (t == tau) & (key0 + j * TK > j_cut)
            sc_s[tile_rows(j), :] = jnp.where(drop, neg_inf, t)
            return carry

        lax.fori_loop(0, n_tiles, fix, 0)

    c_exp = (DSA_HEAD_DIM ** -0.5) * math.log2(math.e)
    def attn_logits(jh, buf):
        buf[...] = _dot(ckv_ref[half_rows(jh), :], ql_ref[...])

    ql_f = ql_ref[...].astype(F32)
    bound = jnp.sqrt(jnp.sum(ql_f * ql_f, axis=0, keepdims=True)) * cmax_ref[...]

    def fast_numerators(jh, buf, p_buf):
        keep = jnp.where(sc_s[half_rows(jh), :] >= tau, 1.0, 0.0).astype(BF16)
        for hh in range(H):
            cols = slice(hh * TQ, (hh + 1) * TQ)
            p = jnp.exp2((buf[:, cols] - bound[:, cols]) * c_exp).astype(BF16)
            p_buf[:, cols] = p * keep

    def fast_values(jh, p_buf):
        cols_j = pl.ds(pl.multiple_of(jh * TH, TH), TH)
        acc_s[...] += _dot(ckvt_ref[:, cols_j], p_buf[...])

    def fast_pair(j, carry):
        attn_logits(2 * j + 1, lg_b)
        fast_numerators(2 * j, lg_a, p_a)
        fast_values(jnp.maximum(2 * j - 1, 0), p_b)
        attn_logits(jnp.minimum(2 * j + 2, last_half), lg_a)
        fast_numerators(2 * j + 1, lg_b, p_b)
        fast_values(2 * j, p_a)
        return carry

    acc_s[...] = jnp.zeros_like(acc_s)
    p_b[...] = jnp.zeros_like(p_b)
    attn_logits(0, lg_a)
    lax.fori_loop(0, n_tiles, fast_pair, 0)
    fast_values(last_half, p_b)
    denom_ok = acc_s[RKV:RKV + 1, :] >= DENOM_FLOOR
    underflowed = jnp.sum(jnp.where(denom_ok, 0, 1)) > 0

    def attn_softmax(jh, buf, p_buf):
        bias_s[...] = jnp.where(sc_s[half_rows(jh), :] >= tau, 0.0, MASKED)
        alphas = []
        for hh in range(H):
            cols = slice(hh * TQ, (hh + 1) * TQ)
            m_old = m_s[hh:hh + 1, :]
            mx = _fold(buf[:, cols] + bias_s[...], jnp.maximum)
            m_new = jnp.maximum(m_old, jnp.max(mx, axis=0, keepdims=True))
            m_s[hh:hh + 1, :] = m_new
            alphas.append(jnp.exp2((m_old - m_new) * c_exp))
            p_buf[:, cols] = jnp.exp2((buf[:, cols] + bias_s[...] - m_new) * c_exp).astype(BF16)
        return jnp.concatenate(alphas, axis=1)

    def attn_values(jh, p_buf, alpha):
        cols_j = pl.ds(pl.multiple_of(jh * TH, TH), TH)
        acc_s[...] = acc_s[...] * alpha + _dot(ckvt_ref[:, cols_j], p_buf[...])

    def attn_pair(j, alpha_b):
        attn_logits(2 * j + 1, lg_b)
        alpha_a = attn_softmax(2 * j, lg_a, p_a)
        attn_values(jnp.maximum(2 * j - 1, 0), p_b, alpha_b)
        attn_logits(jnp.minimum(2 * j + 2, last_half), lg_a)
        alpha_b = attn_softmax(2 * j + 1, lg_b, p_b)
        attn_values(2 * j, p_a, alpha_a)
        return alpha_b

    @pl.when(underflowed)
    def _():
        m_s[...] = jnp.full_like(m_s, MASKED)
        acc_s[...] = jnp.zeros_like(acc_s)
        p_b[...] = jnp.zeros_like(p_b)
        attn_logits(0, lg_a)
        alpha_b = lax.fori_loop(0, n_tiles, attn_pair, jnp.ones((1, H * TQ), F32))
        attn_values(last_half, p_b, alpha_b)

    o_lat = (acc_s[0:RKV, :] / acc_s[RKV:RKV + 1, :]).astype(BF16)
    o_t = jnp.concatenate(
        [_dot(wuvt_ref[hh], o_lat[:, hh * TQ:(hh + 1) * TQ]) for hh in range(H)], axis=0)
    out_t = _dot(woutt_ref[...], o_t.astype(BF16))
    o_ref[...] = x_ref[...] + out_t.T


def _dsa_attn(x, qi, wt, ql, kidx, ckv, ckvt, kv_norm, w_uv, w_out):
    b, s, d = x.shape
    H, RKV, HI, TQ = DSA_HEADS, DSA_KV_RANK, IDX_HEADS, ATT_Q
    nq = s // TQ
    top_k = min(TOPK_MAX, s // 4)
    wuvt = w_uv.transpose(1, 2, 0).astype(BF16)
    woutt = w_out.T.astype(BF16)
    cmax = (math.sqrt(RKV) * jnp.max(jnp.abs(kv_norm))).reshape(1, 1).astype(F32)
    qrow = pl.BlockSpec((None, TQ, d), lambda bi, i: (bi, i, 0))
    qblk = lambda r, c: pl.BlockSpec((None, r, c), lambda bi, i: (bi * nq + i, 0, 0))
    seq = lambda width: pl.BlockSpec((s, width), lambda bi, i: (bi, 0))
    return pl.pallas_call(
        functools.partial(_dsa_attn_kernel, top_k=top_k),
        grid=(b, nq),
        in_specs=[qrow, qblk(LANES, HI * TQ), qblk(HI, TQ), qblk(RKV, H * TQ), seq(LANES), seq(RKV),
                  pl.BlockSpec((KV_ROWS, s), lambda bi, i: (0, bi)),
                  _resident(wuvt.shape), _resident(woutt.shape), _resident((1, 1))],
        out_specs=qrow,
        out_shape=jax.ShapeDtypeStruct((b, s, d), F32),
        scratch_shapes=[
            pltpu.VMEM((s, TQ), F32),
            pltpu.VMEM((ATT_K // 2, H * TQ), F32),
            pltpu.VMEM((ATT_K // 2, H * TQ), F32),
            pltpu.VMEM((ATT_K // 2, H * TQ), BF16),
            pltpu.VMEM((ATT_K // 2, H * TQ), BF16),
            pltpu.VMEM((ATT_K // 2, TQ), F32),
            pltpu.VMEM((KV_ROWS, H * TQ), F32),
            pltpu.VMEM((H, TQ), F32),
        ],
        compiler_params=_params(2),
        name="dsa_attn",
    )(x, qi, wt, ql, kidx, ckv, ckvt, wuvt, woutt, cmax)


def _dsa(x, g, w_in, q_norm, kv_norm, w_uq, w_qidx, kidx_g, kidx_b, w_uk, w_uv, w_out):
    b, s, d = x.shape
    qi, wt, ql, kidx, ckv, ckvt = _dsa_proj(x.reshape(b * s, d), g, w_in, q_norm, kv_norm, w_uq,
                                            w_qidx, kidx_g, kidx_b, w_uk)
    return _dsa_attn(x, qi, wt, ql, kidx, ckv, ckvt, kv_norm, w_uv, w_out)


def kernel(x, norm_ffn1, w_ffn1_in, w_ffn1_out, norm_mix, norm_ffn2, w_ffn2_in, w_ffn2_out, ret_w_in,
           ret_w_out, dsa_w_in, dsa_q_norm, dsa_kv_norm, dsa_w_uq, dsa_w_qidx, dsa_kidx_g, dsa_kidx_b,
           dsa_w_uk, dsa_w_uv, dsa_w_out, final_norm):
    b, s, d = x.shape
    depth = norm_ffn1.shape[0]
    for layer in range(depth):
        x = _ffn(x.reshape(b * s, d), norm_ffn1[layer], w_ffn1_in[layer], w_ffn1_out[layer])
        x = x.reshape(b, s, d)
        j = layer // 2
        if layer % 2 == 0:
            x = _retention(x, norm_mix[layer], ret_w_in[j], ret_w_out[j])
        else:
            x = _dsa(x, norm_mix[layer], dsa_w_in[j], dsa_q_norm[j], dsa_kv_norm[j], dsa_w_uq[j],
                     dsa_w_qidx[j], dsa_kidx_g[j], dsa_kidx_b[j], dsa_w_uk[j], dsa_w_uv[j],
                     dsa_w_out[j])
        last = layer == depth - 1
        x = _ffn(x.reshape(b * s, d), norm_ffn2[layer], w_ffn2_in[layer], w_ffn2_out[layer],
                 final_g=final_norm if last else None)
        x = x.reshape(b, s, d)
    return x
```

```python
import functools
import math

import jax
import jax.numpy as jnp
import numpy as np
from jax import lax
from jax.experimental import pallas as pl
from jax.experimental.pallas import tpu as pltpu

F32 = jnp.float32
BF16 = jnp.bfloat16

D_MODEL = 1024
DEPTH = 4
D_FF = 2816
RMS_EPS = 1e-6

RET_HEADS = 4
RET_DK = D_MODEL // RET_HEADS
RET_DV = 2 * RET_DK
RET_CHUNK = 128
ROT_BASE = 10000.0

DSA_HEADS = 8
DSA_HEAD_DIM = D_MODEL // DSA_HEADS
DSA_Q_RANK = 256
DSA_KV_RANK = 256
IDX_HEADS = 8
IDX_DIM = 64
TOPK_MAX = 256

LANES = 128
SUBLANES = 8
BF16_ROWS = 16
VMEM_LIMIT_BYTES = 56 * 1024 * 1024

FFN_ROWS = 512
FFN_CHUNK = 256
RET_ROWS = 512
PROJ_ROWS = 512
ATT_Q = LANES
ATT_K = 512
KV_ROWS = DSA_KV_RANK + BF16_ROWS
MASKED = -1e30
SEARCH_CAP = 400
SEARCH_UNCHECKED = 10
DENOM_FLOOR = 2.0 ** -80
REDUCE_WAYS = 4


def _resident(shape):
    nd = len(shape)
    return pl.BlockSpec(shape, lambda *_: (0,) * nd, pipeline_mode=pl.Buffered(1))


def _params(n_grid, flags=None):
    return pltpu.CompilerParams(
        dimension_semantics=("arbitrary",) * n_grid, vmem_limit_bytes=VMEM_LIMIT_BYTES, flags=flags)


def _rms(x, g):
    y = x * lax.rsqrt(jnp.mean(x * x, axis=-1, keepdims=True) + RMS_EPS)
    return y * g


def _dot(a, b):
    return jnp.dot(a, b, preferred_element_type=F32)


def _dot_nt(a, b):
    return lax.dot_general(a, b, (((1,), (1,)), ((), ())), preferred_element_type=F32)


def _dot_tn(a, b):
    return lax.dot_general(a, b, (((0,), (0,)), ((), ())), preferred_element_type=F32)


def _ffn_kernel(x_ref, g_ref, wg_ref, wu_ref, wo_ref, *rest, final):
    if final:
        fg_ref, o_ref, acc_ref = rest
    else:
        o_ref, acc_ref = rest
    x = x_ref[...]
    h = _rms(x, g_ref[...]).astype(BF16)
    acc_ref[...] = jnp.zeros_like(acc_ref)

    def body(c, carry):
        gate = _dot(h, wg_ref[c])
        up = _dot(h, wu_ref[c])
        a = (gate * jax.nn.sigmoid(gate) * up).astype(BF16)
        acc_ref[...] += _dot(a, wo_ref[c])
        return carry

    lax.fori_loop(0, wg_ref.shape[0], body, 0, unroll=True)
    y = x + 0.5 * acc_ref[...]
    if final:
        y = _rms(y, fg_ref[...])
    o_ref[...] = y


def _ffn(x2, g, w_in, w_out, final_g=None):
    n, d = x2.shape
    nc = D_FF // FFN_CHUNK
    wg = w_in[:, :D_FF].reshape(d, nc, FFN_CHUNK).transpose(1, 0, 2).astype(BF16)
    wu = w_in[:, D_FF:].reshape(d, nc, FFN_CHUNK).transpose(1, 0, 2).astype(BF16)
    wo = w_out.reshape(nc, FFN_CHUNK, d).astype(BF16)
    row = pl.BlockSpec((FFN_ROWS, d), lambda i: (i, 0))
    ins = [x2, g.reshape(1, d), wg, wu, wo]
    specs = [row, _resident((1, d)), _resident(wg.shape), _resident(wu.shape), _resident(wo.shape)]
    if final_g is not None:
        ins.append(final_g.reshape(1, d))
        specs.append(_resident((1, d)))
    return pl.pallas_call(
        functools.partial(_ffn_kernel, final=final_g is not None),
        grid=(n // FFN_ROWS,),
        in_specs=specs,
        out_specs=row,
        out_shape=jax.ShapeDtypeStruct((n, d), F32),
        scratch_shapes=[pltpu.VMEM((FFN_ROWS, d), F32)],
        compiler_params=_params(1),
        name="ffn_final" if final_g is not None else "ffn",
    )(*ins)


def _ret_kernel(x_ref, g_ref, win_ref, wout_ref, cos_ref, sin_ref, inner_ref, qdec_ref, kdec_ref,
                cdec_ref, o_ref, q_s, k_s, v_s, gate_s, z_s, state_s):
    H, dk, dv, C = RET_HEADS, RET_DK, RET_DV, RET_CHUNK
    half = dk // 2

    @pl.when(pl.program_id(1) == 0)
    def _():
        state_s[...] = jnp.zeros_like(state_s)

    x = x_ref[...]
    h = _rms(x, g_ref[...]).astype(BF16)
    cos = cos_ref[...]
    sin = sin_ref[...]
    q = _dot(h, win_ref[:, 0:H * dk])
    k = _dot(h, win_ref[:, H * dk:2 * H * dk])
    for hh in range(H):
        q0 = q[:, hh * dk:hh * dk + half]
        q1 = q[:, hh * dk + half:(hh + 1) * dk]
        q_s[:, hh * dk:hh * dk + half] = (q0 * cos - q1 * sin).astype(BF16)
        q_s[:, hh * dk + half:(hh + 1) * dk] = (q1 * cos + q0 * sin).astype(BF16)
        k0 = k[:, hh * dk:hh * dk + half]
        k1 = k[:, hh * dk + half:(hh + 1) * dk]
        k_s[:, hh * dk:hh * dk + half] = (k0 * cos - k1 * sin) * (dk ** -0.5)
        k_s[:, hh * dk + half:(hh + 1) * dk] = (k1 * cos + k0 * sin) * (dk ** -0.5)
    v_s[...] = _dot(h, win_ref[:, 2 * H * dk:2 * H * dk + H * dv]).astype(BF16)
    gate_s[...] = _dot(h, win_ref[:, 2 * H * dk + H * dv:])

    def chunk(c, carry):
        r0 = pl.multiple_of(c * C, C)
        rows = pl.ds(r0, C)
        for hh in range(H):
            qc = q_s[rows, hh * dk:(hh + 1) * dk]
            kc = k_s[rows, hh * dk:(hh + 1) * dk]
            vc = v_s[rows, hh * dv:(hh + 1) * dv]
            st = state_s[hh]
            scores = _dot_nt(qc, kc.astype(BF16)) * inner_ref[hh]
            inner = _dot(scores.astype(BF16), vc)
            cross = _dot(qc, st.astype(BF16)) * qdec_ref[hh]
            kd = (kc * kdec_ref[hh]).astype(BF16)
            state_s[hh] = st * cdec_ref[hh] + _dot_tn(kd, vc)
            out = inner + cross
            out = out * lax.rsqrt(jnp.mean(out * out, axis=-1, keepdims=True) + RMS_EPS)
            gt = gate_s[rows, hh * dv:(hh + 1) * dv]
            z_s[rows, hh * dv:(hh + 1) * dv] = (gt * jax.nn.sigmoid(gt) * out).astype(BF16)
        return carry

    lax.fori_loop(0, x.shape[0] // C, chunk, 0, unroll=True)
    o_ref[...] = x + _dot(z_s[...], wout_ref[...])


def _ret_tables():
    H, C = RET_HEADS, RET_CHUNK
    log_gamma = jnp.log(1.0 - 2.0 ** (-5.0 - jnp.arange(H, dtype=F32)))
    idx = jnp.arange(C, dtype=F32)
    diff = idx[:, None] - idx[None, :]
    inner = jnp.where(diff[None] >= 0,
                      jnp.exp(jnp.maximum(diff, 0.0)[None] * log_gamma[:, None, None]), 0.0)
    qdec = jnp.exp((idx[None, :] + 1.0) * log_gamma[:, None])[:, :, None]
    kdec = jnp.exp((C - 1.0 - idx[None, :]) * log_gamma[:, None])[:, :, None]
    cdec = jnp.exp(C * log_gamma)[:, None, None]
    return inner, qdec, kdec, cdec


def _rot_tables(s):
    pos = jnp.arange(s, dtype=F32)
    freq = 1.0 / (ROT_BASE ** jnp.linspace(0.0, 1.0, RET_DK // 2, dtype=F32))
    ang = pos[:, None] * freq[None, :]
    return jnp.cos(ang), jnp.sin(ang)


def _pair_split_columns(w, heads, dim):
    d = w.shape[0]
    w = w.reshape(d, heads, dim // 2, 2)
    return jnp.concatenate([w[..., 0], w[..., 1]], axis=-1).reshape(d, heads * dim)


def _retention(x, g, w_in, w_out):
    b, s, d = x.shape
    H, dk, dv = RET_HEADS, RET_DK, RET_DV
    tb = min(RET_ROWS, s)
    wq = _pair_split_columns(w_in[:, :H * dk], H, dk)
    wk = _pair_split_columns(w_in[:, H * dk:2 * H * dk], H, dk)
    win = jnp.concatenate([wq, wk, w_in[:, 2 * H * dk:]], axis=1).astype(BF16)
    wout = w_out.astype(BF16)
    cos, sin = _rot_tables(s)
    inner, qdec, kdec, cdec = _ret_tables()
    row = pl.BlockSpec((None, tb, d), lambda bi, i: (bi, i, 0))
    rot = pl.BlockSpec((tb, dk // 2), lambda bi, i: (i, 0))
    return pl.pallas_call(
        _ret_kernel,
        grid=(b, s // tb),
        in_specs=[row, _resident((1, d)), _resident(win.shape), _resident(wout.shape), rot, rot,
                  _resident(inner.shape), _resident(qdec.shape), _resident(kdec.shape),
                  _resident(cdec.shape)],
        out_specs=row,
        out_shape=jax.ShapeDtypeStruct((b, s, d), F32),
        scratch_shapes=[
            pltpu.VMEM((tb, H * dk), BF16),
            pltpu.VMEM((tb, H * dk), F32),
            pltpu.VMEM((tb, H * dv), BF16),
            pltpu.VMEM((tb, H * dv), F32),
            pltpu.VMEM((tb, H * dv), BF16),
            pltpu.VMEM((H, dk, dv), F32),
        ],
        compiler_params=_params(2),
        name="retention",
    )(x, g.reshape(1, d), win, wout, cos, sin, inner, qdec, kdec, cdec)


def _dsa_proj_kernel(x_ref, g_ref, win_ref, winw_ref, qn_ref, kvn_ref, kg_ref, kb_ref, wqi_ref,
                     wuq_ref, wuk_ref, qi_ref, wt_ref, ql_ref, kidx_ref, ckv_ref, ckvt_ref):
    RQ, RKV, HI, H, dh, TQ = DSA_Q_RANK, DSA_KV_RANK, IDX_HEADS, DSA_HEADS, DSA_HEAD_DIM, ATT_Q
    n_blk = x_ref.shape[0] // TQ
    h = _rms(x_ref[...], g_ref[...]).astype(BF16)
    proj = _dot(h, win_ref[...])
    c_q = _rms(proj[:, :RQ], qn_ref[...])
    c_kv = _rms(proj[:, RQ:RQ + RKV], kvn_ref[...])
    kx = proj[:, RQ + RKV:RQ + RKV + LANES]
    lane = lax.broadcasted_iota(jnp.int32, kx.shape, 1)
    mu = jnp.sum(kx, axis=-1, keepdims=True) * (1.0 / IDX_DIM)
    cen = jnp.where(lane < IDX_DIM, kx - mu, 0.0)
    var = jnp.sum(cen * cen, axis=-1, keepdims=True) * (1.0 / IDX_DIM)
    kidx_ref[...] = (cen * lax.rsqrt(var + RMS_EPS) * kg_ref[...] + kb_ref[...]).astype(BF16)
    ckv_ref[...] = c_kv.astype(BF16)
    ckvt_ref[0:RKV, :] = c_kv.T.astype(BF16)
    ones_row = lax.broadcasted_iota(jnp.int32, (BF16_ROWS, x_ref.shape[0]), 0) == 0
    ckvt_ref[RKV:, :] = jnp.where(ones_row, 1.0, 0.0).astype(BF16)
    w_t = _dot_nt(winw_ref[...], h) * (HI ** -0.5)
    cq = c_q.astype(BF16)
    qi_t = (_dot_nt(wqi_ref[...], cq) * (IDX_DIM ** -0.5)).astype(BF16)
    q = _dot(cq, wuq_ref[...]).astype(BF16)
    ql_t = [_dot_nt(wuk_ref[hh], q[:, hh * dh:(hh + 1) * dh]).astype(BF16) for hh in range(H)]
    for u in range(n_blk):
        cols = slice(u * TQ, (u + 1) * TQ)
        wt_ref[u] = w_t[0:HI, cols]
        for hh in range(HI):
            qi_ref[u, :, hh * TQ:(hh + 1) * TQ] = qi_t[hh * LANES:(hh + 1) * LANES, cols]
        for hh in range(H):
            ql_ref[u, :, hh * TQ:(hh + 1) * TQ] = ql_t[hh][:, cols]


def _dsa_proj(x2, g, w_in, q_norm, kv_norm, w_uq, w_qidx, kidx_g, kidx_b, w_uk):
    n, d = x2.shape
    RQ, RKV, HI, DI, H, TQ = DSA_Q_RANK, DSA_KV_RANK, IDX_HEADS, IDX_DIM, DSA_HEADS, ATT_Q
    pad_k = jnp.zeros((d, LANES - DI), F32)
    win = jnp.concatenate([w_in[:, :RQ + RKV + DI], pad_k], axis=1).astype(BF16)
    winw = jnp.pad(w_in[:, RQ + RKV + DI:].T, ((0, BF16_ROWS - HI), (0, 0))).astype(BF16)
    kg = jnp.pad(kidx_g, (0, LANES - DI)).reshape(1, LANES)
    kb = jnp.pad(kidx_b, (0, LANES - DI)).reshape(1, LANES)
    wqi = jnp.pad(w_qidx.reshape(RQ, HI, DI), ((0, 0), (0, 0), (0, LANES - DI)))
    wqi = wqi.reshape(RQ, HI * LANES).T.astype(BF16)
    wuq = w_uq.astype(BF16)
    wuk = w_uk.transpose(1, 0, 2).astype(BF16)
    tm = min(PROJ_ROWS, n)
    nb = tm // TQ
    row = lambda w: pl.BlockSpec((tm, w), lambda i: (i, 0))
    blk = lambda r, c: pl.BlockSpec((nb, r, c), lambda i: (i, 0, 0))
    return pl.pallas_call(
        _dsa_proj_kernel,
        grid=(n // tm,),
        in_specs=[row(d), _resident((1, d)), _resident(win.shape), _resident(winw.shape),
                  _resident((1, RQ)), _resident((1, RKV)), _resident((1, LANES)),
                  _resident((1, LANES)), _resident(wqi.shape), _resident(wuq.shape),
                  _resident(wuk.shape)],
        out_specs=[blk(LANES, HI * TQ), blk(HI, TQ), blk(RKV, H * TQ), row(LANES), row(RKV),
                   pl.BlockSpec((KV_ROWS, tm), lambda i: (0, i))],
        out_shape=[
            jax.ShapeDtypeStruct((n // TQ, LANES, HI * TQ), BF16),
            jax.ShapeDtypeStruct((n // TQ, HI, TQ), F32),
            jax.ShapeDtypeStruct((n // TQ, RKV, H * TQ), BF16),
            jax.ShapeDtypeStruct((n, LANES), BF16),
            jax.ShapeDtypeStruct((n, RKV), BF16),
            jax.ShapeDtypeStruct((KV_ROWS, n), BF16),
        ],
        compiler_params=_params(1),
        name="dsa_proj",
    )(x2, g.reshape(1, d), win, winw, q_norm.reshape(1, RQ), kv_norm.reshape(1, RKV), kg, kb, wqi,
      wuq, wuk)


def _fold(t, op):
    groups = t.shape[0] // SUBLANES
    group = lambda r: t[r * SUBLANES:(r + 1) * SUBLANES, :]
    acc = [group(w) for w in range(REDUCE_WAYS)]
    for r in range(REDUCE_WAYS, groups, REDUCE_WAYS):
        acc = [op(acc[w], group(r + w)) for w in range(REDUCE_WAYS)]
    while len(acc) > 1:
        acc = [op(acc[2 * w], acc[2 * w + 1]) for w in range(len(acc) // 2)]
    return acc[0]


def _dsa_attn_kernel(x_ref, qi_ref, wt_ref, ql_ref, kidx_ref, ckv_ref, ckvt_ref, wuvt_ref,
                     woutt_ref, cmax_ref, o_ref, sc_s, lg_a, lg_b, p_a, p_b, bias_s, acc_s, m_s, *, top_k):
    TQ, TK, TH = ATT_Q, ATT_K, ATT_K // 2
    HI, H, RKV = IDX_HEADS, DSA_HEADS, DSA_KV_RANK
    i = pl.program_id(1)
    n_tiles = (i * TQ + TQ + TK - 1) // TK
    qpos = i * TQ + lax.broadcasted_iota(jnp.int32, (1, TQ), 1)
    key0 = lax.broadcasted_iota(jnp.int32, (TK, 1), 0)
    keyh = lax.broadcasted_iota(jnp.int32, (TH, 1), 0)
    neg_inf = jnp.float32(-jnp.inf)
    kf = jnp.float32(top_k)

    def tile_rows(j):
        return pl.ds(pl.multiple_of(j * TK, TK), TK)

    w_t = wt_ref[...]
    last_half = 2 * n_tiles - 1

    def half_rows(jh):
        return pl.ds(pl.multiple_of(jh * TH, TH), TH)

    def score_dots(jh, buf):
        buf[...] = _dot(kidx_ref[half_rows(jh), :], qi_ref[...])

    def score_reduce(jh, buf, carry):
        rmax, rmin, c_ge0, c_gt0 = carry
        s = jnp.maximum(buf[:, 0:TQ], 0.0) * w_t[0:1, :]
        for hh in range(1, HI):
            s = s + jnp.maximum(buf[:, hh * TQ:(hh + 1) * TQ], 0.0) * w_t[hh:hh + 1, :]
        causal = keyh + jh * TH <= qpos
        s = jnp.where(causal, s, neg_inf)
        sc_s[half_rows(jh), :] = s
        rmax = jnp.maximum(rmax, _fold(s, jnp.maximum))
        rmin = jnp.minimum(rmin, _fold(jnp.where(causal, s, jnp.inf), jnp.minimum))
        c_ge0 = c_ge0 + _fold(jnp.where(s >= 0.0, 1.0, 0.0), jnp.add)
        c_gt0 = c_gt0 + _fold(jnp.where(s > 0.0, 1.0, 0.0), jnp.add)
        return rmax, rmin, c_ge0, c_gt0

    def score_pair(j, carry):
        score_dots(2 * j + 1, lg_b)
        carry = score_reduce(2 * j, lg_a, carry)
        score_dots(jnp.minimum(2 * j + 2, last_half), lg_a)
        return score_reduce(2 * j + 1, lg_b, carry)

    part = lambda v: jnp.full((SUBLANES, TQ), v, F32)
    score_dots(0, lg_a)
    rmax, rmin, c_ge0, c_gt0 = lax.fori_loop(
        0, n_tiles, score_pair, (part(-jnp.inf), part(jnp.inf), part(0.0), part(0.0)))
    rmax = jnp.max(rmax, axis=0, keepdims=True)
    rmin = jnp.min(rmin, axis=0, keepdims=True)
    c_ge0 = jnp.sum(c_ge0, axis=0, keepdims=True)
    c_gt0 = jnp.sum(c_gt0, axis=0, keepdims=True)

    def count(pred):
        def body(j, acc):
            return acc + _fold(jnp.where(pred(sc_s[tile_rows(j), :], j), 1.0, 0.0), jnp.add)
        return jnp.sum(lax.fori_loop(0, n_tiles, body, part(0.0)), axis=0, keepdims=True)

    n_valid = (qpos + 1).astype(F32)
    all_sel = n_valid <= kf
    zero_tie = (c_gt0 < kf) & (c_ge0 >= kf)
    positive = c_gt0 >= kf
    lo = jnp.where(positive | zero_tie, 0.0, rmin)
    hi = jnp.where(positive, 2.0 * rmax, 0.0)
    clo = jnp.where(positive | zero_tie, c_ge0, n_valid)
    chi = jnp.where(positive, 0.0, c_ge0)
    lo = jnp.where(all_sel, jnp.float32(jnp.finfo(jnp.float32).min), lo)
    clo = jnp.where(all_sel, n_valid, clo)
    done = all_sel | zero_tie | (clo == kf)
    one = jnp.ones((1, TQ), F32)
    log_target = math.log(top_k + 0.5)

    def pending(done):
        return jnp.sum(jnp.where(done, 0, 1))

    def search_cond(c):
        return (c[1] > 0) & (c[0] < SEARCH_CAP)

    def search_step(state):
        lo, hi, clo, chi, wl, wh, side, done_f = state
        done = done_f > 0.0
        mid = 0.5 * lo + 0.5 * hi
        fa = (jnp.log(clo) - log_target) * wl
        fb = (log_target - jnp.log(jnp.maximum(chi, 0.5))) * wh
        cand = lo + (hi - lo) * (fa / (fa + fb))
        cand = jnp.where((cand > lo) & (cand < hi), cand, mid)
        collapsed = (cand <= lo) | (cand >= hi)
        cm = count(lambda t, j: t >= cand)
        move = jnp.logical_not(done | collapsed)
        up = move & (cm >= kf)
        down = move & (cm < kf)
        wh = jnp.where(up, jnp.where(side > 0.0, 0.5 * wh, 1.0), jnp.where(down, 1.0, wh))
        wl = jnp.where(down, jnp.where(side < 0.0, 0.5 * wl, 1.0), jnp.where(up, 1.0, wl))
        side = jnp.where(up, 1.0, jnp.where(down, -1.0, side))
        lo = jnp.where(up, cand, lo)
        clo = jnp.where(up, cm, clo)
        hi = jnp.where(down, cand, hi)
        chi = jnp.where(down, cm, chi)
        done = done | collapsed | (clo == kf)
        return lo, hi, clo, chi, wl, wh, side, jnp.where(done, 1.0, 0.0)

    def search_body(c):
        state = search_step(c[2:])
        return (c[0] + 1, pending(state[-1] > 0.0)) + state

    state = (lo, hi, clo, chi, one, one, 0.0 * one, jnp.where(done, 1.0, 0.0))
    state = lax.fori_loop(0, SEARCH_UNCHECKED, lambda _, s: search_step(s), state)
    res = lax.while_loop(search_cond, search_body,
                         (jnp.int32(0), pending(state[-1] > 0.0)) + state)
    tau, clo = res[2], res[4]

    excess = clo > kf

    @pl.when(jnp.sum(jnp.where(excess, 1, 0)) > 0)
    def _():
        need = kf - count(lambda t, j: t > tau)
        n_steps = max(1, math.ceil(math.log2(sc_s.shape[0]))) + 1

        def step(_, jb):
            j_lo, j_hi = jb
            j_mid = (j_lo + j_hi) >> 1
            c = count(lambda t, j: (t == tau) & (key0 + j * TK <= j_mid))
            ok = c >= need
            return jnp.where(ok, j_lo, j_mid), jnp.where(ok, j_mid, j_hi)

        _, j_cut = lax.fori_loop(0, n_steps, step, (jnp.full((1, TQ), -1, jnp.int32), qpos))

        def fix(j, carry):
            t = sc_s[tile_rows(j), :]
            drop = excess & (t == tau) & (key0 + j * TK > j_cut)
            sc_s[tile_rows(j), :] = jnp.where(drop, neg_inf, t)
            return carry

        lax.fori_loop(0, n_tiles, fix, 0)

    c_exp = (DSA_HEAD_DIM ** -0.5) * math.log2(math.e)
    def attn_logits(jh, buf):
        buf[...] = _dot(ckv_ref[half_rows(jh), :], ql_ref[...])

    ql_f = ql_ref[...].astype(F32)
    bound = jnp.sqrt(jnp.sum(ql_f * ql_f, axis=0, keepdims=True)) * cmax_ref[...]

    def fast_numerators(jh, buf, p_buf):
        keep = jnp.where(sc_s[half_rows(jh), :] >= tau, 1.0, 0.0).astype(BF16)
        for hh in range(H):
            cols = slice(hh * TQ, (hh + 1) * TQ)
            p = jnp.exp2((buf[:, cols] - bound[:, cols]) * c_exp).astype(BF16)
            p_buf[:, cols] = p * keep

    def fast_values(jh, p_buf):
        cols_j = pl.ds(pl.multiple_of(jh * TH, TH), TH)
        acc_s[...] += _dot(ckvt_ref[:, cols_j], p_buf[...])

    def fast_pair(j, carry):
        attn_logits(2 * j + 1, lg_b)
        fast_numerators(2 * j, lg_a, p_a)
        fast_values(jnp.maximum(2 * j - 1, 0), p_b)
        attn_logits(jnp.minimum(2 * j + 2, last_half), lg_a)
        fast_numerators(2 * j + 1, lg_b, p_b)
        fast_values(2 * j, p_a)
        return carry

    acc_s[...] = jnp.zeros_like(acc_s)
    p_b[...] = jnp.zeros_like(p_b)
    attn_logits(0, lg_a)
    lax.fori_loop(0, n_tiles, fast_pair, 0)
    fast_values(last_half, p_b)
    denom_ok = acc_s[RKV:RKV + 1, :] >= DENOM_FLOOR
    underflowed = jnp.sum(jnp.where(denom_ok, 0, 1)) > 0

    def attn_softmax(jh, buf, p_buf):
        bias_s[...] = jnp.where(sc_s[half_rows(jh), :] >= tau, 0.0, MASKED)
        alphas = []
        for hh in range(H):
            cols = slice(hh * TQ, (hh + 1) * TQ)
            m_old = m_s[hh:hh + 1, :]
            mx = _fold(buf[:, cols] + bias_s[...], jnp.maximum)
            m_new = jnp.maximum(m_old, jnp.max(mx, axis=0, keepdims=True))
            m_s[hh:hh + 1, :] = m_new
            alphas.append(jnp.exp2((m_old - m_new) * c_exp))
            p_buf[:, cols] = jnp.exp2((buf[:, cols] + bias_s[...] - m_new) * c_exp).astype(BF16)
        return jnp.concatenate(alphas, axis=1)

    def attn_values(jh, p_buf, alpha):
        cols_j = pl.ds(pl.multiple_of(jh * TH, TH), TH)
        acc_s[...] = acc_s[...] * alpha + _dot(ckvt_ref[:, cols_j], p_buf[...])

    def attn_pair(j, alpha_b):
        attn_logits(2 * j + 1, lg_b)
        alpha_a = attn_softmax(2 * j, lg_a, p_a)
        attn_values(jnp.maximum(2 * j - 1, 0), p_b, alpha_b)
        attn_logits(jnp.minimum(2 * j + 2, last_half), lg_a)
        alpha_b = attn_softmax(2 * j + 1, lg_b, p_b)
        attn_values(2 * j, p_a, alpha_a)
        return alpha_b

    @pl.when(underflowed)
    def _():
        m_s[...] = jnp.full_like(m_s, MASKED)
        acc_s[...] = jnp.zeros_like(acc_s)
        p_b[...] = jnp.zeros_like(p_b)
        attn_logits(0, lg_a)
        alpha_b = lax.fori_loop(0, n_tiles, attn_pair, jnp.ones((1, H * TQ), F32))
        attn_values(last_half, p_b, alpha_b)

    o_lat = (acc_s[0:RKV, :] / acc_s[RKV:RKV + 1, :]).astype(BF16)
    o_t = jnp.concatenate(
        [_dot(wuvt_ref[hh], o_lat[:, hh * TQ:(hh + 1) * TQ]) for hh in range(H)], axis=0)
    out_t = _dot(woutt_ref[...], o_t.astype(BF16))
    o_ref[...] = x_ref[...] + out_t.T


def _dsa_attn(x, qi, wt, ql, kidx, ckv, ckvt, kv_norm, w_uv, w_out):
    b, s, d = x.shape
    H, RKV, HI, TQ = DSA_HEADS, DSA_KV_RANK, IDX_HEADS, ATT_Q
    nq = s // TQ
    top_k = min(TOPK_MAX, s // 4)
    wuvt = w_uv.transpose(1, 2, 0).astype(BF16)
    woutt = w_out.T.astype(BF16)
    cmax = (math.sqrt(RKV) * jnp.max(jnp.abs(kv_norm))).reshape(1, 1).astype(F32)
    qrow = pl.BlockSpec((None, TQ, d), lambda bi, i: (bi, i, 0))
    qblk = lambda r, c: pl.BlockSpec((None, r, c), lambda bi, i: (bi * nq + i, 0, 0))
    seq = lambda width: pl.BlockSpec((s, width), lambda bi, i: (bi, 0))
    return pl.pallas_call(
        functools.partial(_dsa_attn_kernel, top_k=top_k),
        grid=(b, nq),
        in_specs=[qrow, qblk(LANES, HI * TQ), qblk(HI, TQ), qblk(RKV, H * TQ), seq(LANES), seq(RKV),
                  pl.BlockSpec((KV_ROWS, s), lambda bi, i: (0, bi)),
                  _resident(wuvt.shape), _resident(woutt.shape), _resident((1, 1))],
        out_specs=qrow,
        out_shape=jax.ShapeDtypeStruct((b, s, d), F32),
        scratch_shapes=[
            pltpu.VMEM((s, TQ), F32),
            pltpu.VMEM((ATT_K // 2, H * TQ), F32),
            pltpu.VMEM((ATT_K // 2, H * TQ), F32),
            pltpu.VMEM((ATT_K // 2, H * TQ), BF16),
            pltpu.VMEM((ATT_K // 2, H * TQ), BF16),
            pltpu.VMEM((ATT_K // 2, TQ), F32),
            pltpu.VMEM((KV_ROWS, H * TQ), F32),
            pltpu.VMEM((H, TQ), F32),
        ],
        compiler_params=_params(2),
        name="dsa_attn",
    )(x, qi, wt, ql, kidx, ckv, ckvt, wuvt, woutt, cmax)


def _dsa(x, g, w_in, q_norm, kv_norm, w_uq, w_qidx, kidx_g, kidx_b, w_uk, w_uv, w_out):
    b, s, d = x.shape
    qi, wt, ql, kidx, ckv, ckvt = _dsa_proj(x.reshape(b * s, d), g, w_in, q_norm, kv_norm, w_uq,
                                            w_qidx, kidx_g, kidx_b, w_uk)
    return _dsa_attn(x, qi, wt, ql, kidx, ckv, ckvt, kv_norm, w_uv, w_out)


def kernel(x, norm_ffn1, w_ffn1_in, w_ffn1_out, norm_mix, norm_ffn2, w_ffn2_in, w_ffn2_out, ret_w_in,
           ret_w_out, dsa_w_in, dsa_q_norm, dsa_kv_norm, dsa_w_uq, dsa_w_qidx, dsa_kidx_g, dsa_kidx_b,
           dsa_w_uk, dsa_w_uv, dsa_w_out, final_norm):
    b, s, d = x.shape
    depth = norm_ffn1.shape[0]
    for layer in range(depth):
        x = _ffn(x.reshape(b * s, d), norm_ffn1[layer], w_ffn1_in[layer], w_ffn1_out[layer])
        x = x.reshape(b, s, d)
        j = layer // 2
        if layer % 2 == 0:
            x = _retention(x, norm_mix[layer], ret_w_in[j], ret_w_out[j])
        else:
            x = _dsa(x, norm_mix[layer], dsa_w_in[j], dsa_q_norm[j], dsa_kv_norm[j], dsa_w_uq[j],
                     dsa_w_qidx[j], dsa_kidx_g[j], dsa_kidx_b[j], dsa_w_uk[j], dsa_w_uv[j],
                     dsa_w_out[j])
        last = layer == depth - 1
        x = _ffn(x.reshape(b * s, d), norm_ffn2[layer], w_ffn2_in[layer], w_ffn2_out[layer],
                 final_g=final_norm if last else None)
        x = x.reshape(b, s, d)
    return x
```

```python
import functools
import math

import jax
import jax.numpy as jnp
import numpy as np
from jax import lax
from jax.experimental import pallas as pl
from jax.experimental.pallas import tpu as pltpu

F32 = jnp.float32
BF16 = jnp.bfloat16

D_MODEL = 1024
DEPTH = 4
D_FF = 2816
RMS_EPS = 1e-6

RET_HEADS = 4
RET_DK = D_MODEL // RET_HEADS
RET_DV = 2 * RET_DK
RET_CHUNK = 128
ROT_BASE = 10000.0

DSA_HEADS = 8
DSA_HEAD_DIM = D_MODEL // DSA_HEADS
DSA_Q_RANK = 256
DSA_KV_RANK = 256
IDX_HEADS = 8
IDX_DIM = 64
TOPK_MAX = 256

LANES = 128
SUBLANES = 8
BF16_ROWS = 16
VMEM_LIMIT_BYTES = 56 * 1024 * 1024

FFN_ROWS = 512
FFN_CHUNK = 256
RET_ROWS = 512
PROJ_ROWS = 512
ATT_Q = LANES
ATT_K = 512
KV_ROWS = DSA_KV_RANK + BF16_ROWS
MASKED = -1e30
SEARCH_CAP = 400
SEARCH_UNCHECKED = 9
FINISH_RANKS = 4
DENOM_FLOOR = 2.0 ** -80
REDUCE_WAYS = 4


def _resident(shape):
    nd = len(shape)
    return pl.BlockSpec(shape, lambda *_: (0,) * nd, pipeline_mode=pl.Buffered(1))


def _params(n_grid, flags=None):
    return pltpu.CompilerParams(
        dimension_semantics=("arbitrary",) * n_grid, vmem_limit_bytes=VMEM_LIMIT_BYTES, flags=flags)


def _rms(x, g):
    y = x * lax.rsqrt(jnp.mean(x * x, axis=-1, keepdims=True) + RMS_EPS)
    return y * g


def _dot(a, b):
    return jnp.dot(a, b, preferred_element_type=F32)


def _dot_nt(a, b):
    return lax.dot_general(a, b, (((1,), (1,)), ((), ())), preferred_element_type=F32)


def _dot_tn(a, b):
    return lax.dot_general(a, b, (((0,), (0,)), ((), ())), preferred_element_type=F32)


def _ffn_kernel(x_ref, g_ref, wg_ref, wu_ref, wo_ref, *rest, final):
    if final:
        fg_ref, o_ref, acc_ref = rest
    else:
        o_ref, acc_ref = rest
    x = x_ref[...]
    h = _rms(x, g_ref[...]).astype(BF16)
    acc_ref[...] = jnp.zeros_like(acc_ref)

    def body(c, carry):
        gate = _dot(h, wg_ref[c])
        up = _dot(h, wu_ref[c])
        a = (gate * jax.nn.sigmoid(gate) * up).astype(BF16)
        acc_ref[...] += _dot(a, wo_ref[c])
        return carry

    lax.fori_loop(0, wg_ref.shape[0], body, 0, unroll=True)
    y = x + 0.5 * acc_ref[...]
    if final:
        y = _rms(y, fg_ref[...])
    o_ref[...] = y


def _ffn(x2, g, w_in, w_out, final_g=None):
    n, d = x2.shape
    nc = D_FF // FFN_CHUNK
    wg = w_in[:, :D_FF].reshape(d, nc, FFN_CHUNK).transpose(1, 0, 2).astype(BF16)
    wu = w_in[:, D_FF:].reshape(d, nc, FFN_CHUNK).transpose(1, 0, 2).astype(BF16)
    wo = w_out.reshape(nc, FFN_CHUNK, d).astype(BF16)
    row = pl.BlockSpec((FFN_ROWS, d), lambda i: (i, 0))
    ins = [x2, g.reshape(1, d), wg, wu, wo]
    specs = [row, _resident((1, d)), _resident(wg.shape), _resident(wu.shape), _resident(wo.shape)]
    if final_g is not None:
        ins.append(final_g.reshape(1, d))
        specs.append(_resident((1, d)))
    return pl.pallas_call(
        functools.partial(_ffn_kernel, final=final_g is not None),
        grid=(n // FFN_ROWS,),
        in_specs=specs,
        out_specs=row,
        out_shape=jax.ShapeDtypeStruct((n, d), F32),
        scratch_shapes=[pltpu.VMEM((FFN_ROWS, d), F32)],
        compiler_params=_params(1),
        name="ffn_final" if final_g is not None else "ffn",
    )(*ins)


def _ret_kernel(x_ref, g_ref, win_ref, wout_ref, cos_ref, sin_ref, inner_ref, qdec_ref, kdec_ref,
                cdec_ref, o_ref, q_s, k_s, v_s, gate_s, z_s, state_s):
    H, dk, dv, C = RET_HEADS, RET_DK, RET_DV, RET_CHUNK
    half = dk // 2

    @pl.when(pl.program_id(1) == 0)
    def _():
        state_s[...] = jnp.zeros_like(state_s)

    x = x_ref[...]
    h = _rms(x, g_ref[...]).astype(BF16)
    cos = cos_ref[...]
    sin = sin_ref[...]
    q = _dot(h, win_ref[:, 0:H * dk])
    k = _dot(h, win_ref[:, H * dk:2 * H * dk])
    for hh in range(H):
        q0 = q[:, hh * dk:hh * dk + half]
        q1 = q[:, hh * dk + half:(hh + 1) * dk]
        q_s[:, hh * dk:hh * dk + half] = (q0 * cos - q1 * sin).astype(BF16)
        q_s[:, hh * dk + half:(hh + 1) * dk] = (q1 * cos + q0 * sin).astype(BF16)
        k0 = k[:, hh * dk:hh * dk + half]
        k1 = k[:, hh * dk + half:(hh + 1) * dk]
        k_s[:, hh * dk:hh * dk + half] = (k0 * cos - k1 * sin) * (dk ** -0.5)
        k_s[:, hh * dk + half:(hh + 1) * dk] = (k1 * cos + k0 * sin) * (dk ** -0.5)
    v_s[...] = _dot(h, win_ref[:, 2 * H * dk:2 * H * dk + H * dv]).astype(BF16)
    gate_s[...] = _dot(h, win_ref[:, 2 * H * dk + H * dv:])

    def chunk(c, carry):
        r0 = pl.multiple_of(c * C, C)
        rows = pl.ds(r0, C)
        for hh in range(H):
            qc = q_s[rows, hh * dk:(hh + 1) * dk]
            kc = k_s[rows, hh * dk:(hh + 1) * dk]
            vc = v_s[rows, hh * dv:(hh + 1) * dv]
            st = state_s[hh]
            scores = _dot_nt(qc, kc.astype(BF16)) * inner_ref[hh]
            inner = _dot(scores.astype(BF16), vc)
            cross = _dot(qc, st.astype(BF16)) * qdec_ref[hh]
            kd = (kc * kdec_ref[hh]).astype(BF16)
            state_s[hh] = st * cdec_ref[hh] + _dot_tn(kd, vc)
            out = inner + cross
            out = out * lax.rsqrt(jnp.mean(out * out, axis=-1, keepdims=True) + RMS_EPS)
            gt = gate_s[rows, hh * dv:(hh + 1) * dv]
            z_s[rows, hh * dv:(hh + 1) * dv] = (gt * jax.nn.sigmoid(gt) * out).astype(BF16)
        return carry

    lax.fori_loop(0, x.shape[0] // C, chunk, 0, unroll=True)
    o_ref[...] = x + _dot(z_s[...], wout_ref[...])


def _ret_tables():
    H, C = RET_HEADS, RET_CHUNK
    log_gamma = jnp.log(1.0 - 2.0 ** (-5.0 - jnp.arange(H, dtype=F32)))
    idx = jnp.arange(C, dtype=F32)
    diff = idx[:, None] - idx[None, :]
    inner = jnp.where(diff[None] >= 0,
                      jnp.exp(jnp.maximum(diff, 0.0)[None] * log_gamma[:, None, None]), 0.0)
    qdec = jnp.exp((idx[None, :] + 1.0) * log_gamma[:, None])[:, :, None]
    kdec = jnp.exp((C - 1.0 - idx[None, :]) * log_gamma[:, None])[:, :, None]
    cdec = jnp.exp(C * log_gamma)[:, None, None]
    return inner, qdec, kdec, cdec


def _rot_tables(s):
    pos = jnp.arange(s, dtype=F32)
    freq = 1.0 / (ROT_BASE ** jnp.linspace(0.0, 1.0, RET_DK // 2, dtype=F32))
    ang = pos[:, None] * freq[None, :]
    return jnp.cos(ang), jnp.sin(ang)


def _pair_split_columns(w, heads, dim):
    d = w.shape[0]
    w = w.reshape(d, heads, dim // 2, 2)
    return jnp.concatenate([w[..., 0], w[..., 1]], axis=-1).reshape(d, heads * dim)


def _retention(x, g, w_in, w_out):
    b, s, d = x.shape
    H, dk, dv = RET_HEADS, RET_DK, RET_DV
    tb = min(RET_ROWS, s)
    wq = _pair_split_columns(w_in[:, :H * dk], H, dk)
    wk = _pair_split_columns(w_in[:, H * dk:2 * H * dk], H, dk)
    win = jnp.concatenate([wq, wk, w_in[:, 2 * H * dk:]], axis=1).astype(BF16)
    wout = w_out.astype(BF16)
    cos, sin = _rot_tables(s)
    inner, qdec, kdec, cdec = _ret_tables()
    row = pl.BlockSpec((None, tb, d), lambda bi, i: (bi, i, 0))
    rot = pl.BlockSpec((tb, dk // 2), lambda bi, i: (i, 0))
    return pl.pallas_call(
        _ret_kernel,
        grid=(b, s // tb),
        in_specs=[row, _resident((1, d)), _resident(win.shape), _resident(wout.shape), rot, rot,
                  _resident(inner.shape), _resident(qdec.shape), _resident(kdec.shape),
                  _resident(cdec.shape)],
        out_specs=row,
        out_shape=jax.ShapeDtypeStruct((b, s, d), F32),
        scratch_shapes=[
            pltpu.VMEM((tb, H * dk), BF16),
            pltpu.VMEM((tb, H * dk), F32),
            pltpu.VMEM((tb, H * dv), BF16),
            pltpu.VMEM((tb, H * dv), F32),
            pltpu.VMEM((tb, H * dv), BF16),
            pltpu.VMEM((H, dk, dv), F32),
        ],
        compiler_params=_params(2),
        name="retention",
    )(x, g.reshape(1, d), win, wout, cos, sin, inner, qdec, kdec, cdec)


def _dsa_proj_kernel(x_ref, g_ref, win_ref, winw_ref, qn_ref, kvn_ref, kg_ref, kb_ref, wqi_ref,
                     wuq_ref, wuk_ref, qi_ref, wt_ref, ql_ref, kidx_ref, ckv_ref, ckvt_ref):
    RQ, RKV, HI, H, dh, TQ = DSA_Q_RANK, DSA_KV_RANK, IDX_HEADS, DSA_HEADS, DSA_HEAD_DIM, ATT_Q
    n_blk = x_ref.shape[0] // TQ
    h = _rms(x_ref[...], g_ref[...]).astype(BF16)
    proj = _dot(h, win_ref[...])
    c_q = _rms(proj[:, :RQ], qn_ref[...])
    c_kv = _rms(proj[:, RQ:RQ + RKV], kvn_ref[...])
    kx = proj[:, RQ + RKV:RQ + RKV + LANES]
    lane = lax.broadcasted_iota(jnp.int32, kx.shape, 1)
    mu = jnp.sum(kx, axis=-1, keepdims=True) * (1.0 / IDX_DIM)
    cen = jnp.where(lane < IDX_DIM, kx - mu, 0.0)
    var = jnp.sum(cen * cen, axis=-1, keepdims=True) * (1.0 / IDX_DIM)
    kidx_ref[...] = (cen * lax.rsqrt(var + RMS_EPS) * kg_ref[...] + kb_ref[...]).astype(BF16)
    ckv_ref[...] = c_kv.astype(BF16)
    ckvt_ref[0:RKV, :] = c_kv.T.astype(BF16)
    ones_row = lax.broadcasted_iota(jnp.int32, (BF16_ROWS, x_ref.shape[0]), 0) == 0
    ckvt_ref[RKV:, :] = jnp.where(ones_row, 1.0, 0.0).astype(BF16)
    w_t = _dot_nt(winw_ref[...], h) * (HI ** -0.5)
    cq = c_q.astype(BF16)
    qi_t = (_dot_nt(wqi_ref[...], cq) * (IDX_DIM ** -0.5)).astype(BF16)
    q = _dot(cq, wuq_ref[...]).astype(BF16)
    ql_t = [_dot_nt(wuk_ref[hh], q[:, hh * dh:(hh + 1) * dh]).astype(BF16) for hh in range(H)]
    for u in range(n_blk):
        cols = slice(u * TQ, (u + 1) * TQ)
        wt_ref[u] = w_t[0:HI, cols]
        for hh in range(HI):
            qi_ref[u, :, hh * TQ:(hh + 1) * TQ] = qi_t[hh * LANES:(hh + 1) * LANES, cols]
        for hh in range(H):
            ql_ref[u, :, hh * TQ:(hh + 1) * TQ] = ql_t[hh][:, cols]


def _dsa_proj(x2, g, w_in, q_norm, kv_norm, w_uq, w_qidx, kidx_g, kidx_b, w_uk):
    n, d = x2.shape
    RQ, RKV, HI, DI, H, TQ = DSA_Q_RANK, DSA_KV_RANK, IDX_HEADS, IDX_DIM, DSA_HEADS, ATT_Q
    pad_k = jnp.zeros((d, LANES - DI), F32)
    win = jnp.concatenate([w_in[:, :RQ + RKV + DI], pad_k], axis=1).astype(BF16)
    winw = jnp.pad(w_in[:, RQ + RKV + DI:].T, ((0, BF16_ROWS - HI), (0, 0))).astype(BF16)
    kg = jnp.pad(kidx_g, (0, LANES - DI)).reshape(1, LANES)
    kb = jnp.pad(kidx_b, (0, LANES - DI)).reshape(1, LANES)
    wqi = jnp.pad(w_qidx.reshape(RQ, HI, DI), ((0, 0), (0, 0), (0, LANES - DI)))
    wqi = wqi.reshape(RQ, HI * LANES).T.astype(BF16)
    wuq = w_uq.astype(BF16)
    wuk = w_uk.transpose(1, 0, 2).astype(BF16)
    tm = min(PROJ_ROWS, n)
    nb = tm // TQ
    row = lambda w: pl.BlockSpec((tm, w), lambda i: (i, 0))
    blk = lambda r, c: pl.BlockSpec((nb, r, c), lambda i: (i, 0, 0))
    return pl.pallas_call(
        _dsa_proj_kernel,
        grid=(n // tm,),
        in_specs=[row(d), _resident((1, d)), _resident(win.shape), _resident(winw.shape),
                  _resident((1, RQ)), _resident((1, RKV)), _resident((1, LANES)),
                  _resident((1, LANES)), _resident(wqi.shape), _resident(wuq.shape),
                  _resident(wuk.shape)],
        out_specs=[blk(LANES, HI * TQ), blk(HI, TQ), blk(RKV, H * TQ), row(LANES), row(RKV),
                   pl.BlockSpec((KV_ROWS, tm), lambda i: (0, i))],
        out_shape=[
            jax.ShapeDtypeStruct((n // TQ, LANES, HI * TQ), BF16),
            jax.ShapeDtypeStruct((n // TQ, HI, TQ), F32),
            jax.ShapeDtypeStruct((n // TQ, RKV, H * TQ), BF16),
            jax.ShapeDtypeStruct((n, LANES), BF16),
            jax.ShapeDtypeStruct((n, RKV), BF16),
            jax.ShapeDtypeStruct((KV_ROWS, n), BF16),
        ],
        compiler_params=_params(1),
        name="dsa_proj",
    )(x2, g.reshape(1, d), win, winw, q_norm.reshape(1, RQ), kv_norm.reshape(1, RKV), kg, kb, wqi,
      wuq, wuk)


def _fold(t, op):
    groups = t.shape[0] // SUBLANES
    group = lambda r: t[r * SUBLANES:(r + 1) * SUBLANES, :]
    acc = [group(w) for w in range(REDUCE_WAYS)]
    for r in range(REDUCE_WAYS, groups, REDUCE_WAYS):
        acc = [op(acc[w], group(r + w)) for w in range(REDUCE_WAYS)]
    while len(acc) > 1:
        acc = [op(acc[2 * w], acc[2 * w + 1]) for w in range(len(acc) // 2)]
    return acc[0]


def _dsa_attn_kernel(x_ref, qi_ref, wt_ref, ql_ref, kidx_ref, ckv_ref, ckvt_ref, wuvt_ref,
                     woutt_ref, cmax_ref, o_ref, sc_s, lg_a, lg_b, p_a, p_b, bias_s, acc_s, m_s, *, top_k):
    TQ, TK, TH = ATT_Q, ATT_K, ATT_K // 2
    HI, H, RKV = IDX_HEADS, DSA_HEADS, DSA_KV_RANK
    i = pl.program_id(1)
    n_tiles = (i * TQ + TQ + TK - 1) // TK
    qpos = i * TQ + lax.broadcasted_iota(jnp.int32, (1, TQ), 1)
    key0 = lax.broadcasted_iota(jnp.int32, (TK, 1), 0)
    keyh = lax.broadcasted_iota(jnp.int32, (TH, 1), 0)
    neg_inf = jnp.float32(-jnp.inf)
    kf = jnp.float32(top_k)

    def tile_rows(j):
        return pl.ds(pl.multiple_of(j * TK, TK), TK)

    w_t = wt_ref[...]
    last_half = 2 * n_tiles - 1

    def half_rows(jh):
        return pl.ds(pl.multiple_of(jh * TH, TH), TH)

    def score_dots(jh, buf):
        buf[...] = _dot(kidx_ref[half_rows(jh), :], qi_ref[...])

    def score_reduce(jh, buf, carry):
        rmax, rmin, c_ge0, c_gt0 = carry
        s = jnp.maximum(buf[:, 0:TQ], 0.0) * w_t[0:1, :]
        for hh in range(1, HI):
            s = s + jnp.maximum(buf[:, hh * TQ:(hh + 1) * TQ], 0.0) * w_t[hh:hh + 1, :]
        causal = keyh + jh * TH <= qpos
        s = jnp.where(causal, s, neg_inf)
        sc_s[half_rows(jh), :] = s
        rmax = jnp.maximum(rmax, _fold(s, jnp.maximum))
        rmin = jnp.minimum(rmin, _fold(jnp.where(causal, s, jnp.inf), jnp.minimum))
        c_ge0 = c_ge0 + _fold(jnp.where(s >= 0.0, 1.0, 0.0), jnp.add)
        c_gt0 = c_gt0 + _fold(jnp.where(s > 0.0, 1.0, 0.0), jnp.add)
        return rmax, rmin, c_ge0, c_gt0

    def score_pair(j, carry):
        score_dots(2 * j + 1, lg_b)
        carry = score_reduce(2 * j, lg_a, carry)
        score_dots(jnp.minimum(2 * j + 2, last_half), lg_a)
        return score_reduce(2 * j + 1, lg_b, carry)

    part = lambda v: jnp.full((SUBLANES, TQ), v, F32)
    score_dots(0, lg_a)
    rmax, rmin, c_ge0, c_gt0 = lax.fori_loop(
        0, n_tiles, score_pair, (part(-jnp.inf), part(jnp.inf), part(0.0), part(0.0)))
    rmax = jnp.max(rmax, axis=0, keepdims=True)
    rmin = jnp.min(rmin, axis=0, keepdims=True)
    c_ge0 = jnp.sum(c_ge0, axis=0, keepdims=True)
    c_gt0 = jnp.sum(c_gt0, axis=0, keepdims=True)

    def count(pred):
        def body(j, acc):
            return acc + _fold(jnp.where(pred(sc_s[tile_rows(j), :], j), 1.0, 0.0), jnp.add)
        return jnp.sum(lax.fori_loop(0, n_tiles, body, part(0.0)), axis=0, keepdims=True)

    n_valid = (qpos + 1).astype(F32)
    all_sel = n_valid <= kf
    zero_tie = (c_gt0 < kf) & (c_ge0 >= kf)
    positive = c_gt0 >= kf
    lo = jnp.where(positive | zero_tie, 0.0, rmin)
    hi = jnp.where(positive, 2.0 * rmax, 0.0)
    clo = jnp.where(positive | zero_tie, c_ge0, n_valid)
    chi = jnp.where(positive, 0.0, c_ge0)
    lo = jnp.where(all_sel, jnp.float32(jnp.finfo(jnp.float32).min), lo)
    clo = jnp.where(all_sel, n_valid, clo)
    done = all_sel | zero_tie | (clo == kf)
    one = jnp.ones((1, TQ), F32)
    log_target = math.log(top_k + 0.5)

    def pending(done):
        return jnp.sum(jnp.where(done, 0, 1))

    def search_cond(c):
        return (c[1] > 0) & (c[0] < SEARCH_CAP)

    def search_step(state):
        lo, hi, clo, chi, wl, wh, side, done_f = state
        done = done_f > 0.0
        mid = 0.5 * lo + 0.5 * hi
        fa = (jnp.log(clo) - log_target) * wl
        fb = (log_target - jnp.log(jnp.maximum(chi, 0.5))) * wh
        cand = lo + (hi - lo) * (fa / (fa + fb))
        cand = jnp.where((cand > lo) & (cand < hi), cand, mid)
        collapsed = (cand <= lo) | (cand >= hi)
        cm = count(lambda t, j: t >= cand)
        move = jnp.logical_not(done | collapsed)
        up = move & (cm >= kf)
        down = move & (cm < kf)
        wh = jnp.where(up, jnp.where(side > 0.0, 0.5 * wh, 1.0), jnp.where(down, 1.0, wh))
        wl = jnp.where(down, jnp.where(side < 0.0, 0.5 * wl, 1.0), jnp.where(up, 1.0, wl))
        side = jnp.where(up, 1.0, jnp.where(down, -1.0, side))
        lo = jnp.where(up, cand, lo)
        clo = jnp.where(up, cm, clo)
        hi = jnp.where(down, cand, hi)
        chi = jnp.where(down, cm, chi)
        done = done | collapsed | (clo == kf)
        return lo, hi, clo, chi, wl, wh, side, jnp.where(done, 1.0, 0.0)

    def search_body(c):
        state = search_step(c[2:])
        return (c[0] + 1, pending(state[-1] > 0.0)) + state

    state = (lo, hi, clo, chi, one, one, 0.0 * one, jnp.where(done, 1.0, 0.0))
    state = lax.fori_loop(0, SEARCH_UNCHECKED, lambda _, s: search_step(s), state)

    lo, hi, clo, chi, wl, wh, side, done_f = state
    inf = jnp.float32(jnp.inf)

    def insert(ranks, v):
        out = []
        for r in ranks:
            out.append(jnp.minimum(r, v))
            v = jnp.maximum(r, v)
        return out

    def smallest_body(j, ranks):
        t = sc_s[tile_rows(j), :]
        ranks = list(ranks)
        for g in range(TK // SUBLANES):
            v = t[g * SUBLANES:(g + 1) * SUBLANES, :]
            w = g % 2
            ranks[w] = tuple(insert(ranks[w], jnp.where(v >= lo, v, inf)))
        return tuple(ranks)

    empty = tuple(jnp.full((SUBLANES, TQ), inf, F32) for _ in range(FINISH_RANKS))
    ranks = lax.fori_loop(0, n_tiles, smallest_body, (empty, empty))
    final = [jnp.full((1, TQ), inf, F32) for _ in range(FINISH_RANKS)]
    for chain in ranks:
        for r in chain:
            for sub in range(SUBLANES):
                final = insert(final, r[sub:sub + 1, :])
    extra = clo - kf
    kth = final[0]
    below = -inf
    for e in range(1, FINISH_RANKS):
        kth = jnp.where(extra >= e, final[e], kth)
        below = jnp.where(extra == e, final[e - 1], below)
    near = (done_f <= 0.0) & (extra < FINISH_RANKS)
    lo = jnp.where(near, kth, lo)
    clo = jnp.where(near, jnp.where(below == kth, kf + 1.0, kf), clo)
    done_f = jnp.where(near, 1.0, done_f)
    state = (lo, hi, clo, chi, wl, wh, side, done_f)

    res = lax.while_loop(search_cond, search_body,
                         (jnp.int32(0), pending(state[-1] > 0.0)) + state)
    tau, clo = res[2], res[4]

    excess = clo > kf

    @pl.when(jnp.sum(jnp.where(excess, 1, 0)) > 0)
    def _():
        need = kf - count(lambda t, j: t > tau)
        n_steps = max(1, math.ceil(math.log2(sc_s.shape[0]))) + 1

        def step(_, jb):
            j_lo, j_hi = jb
            j_mid = (j_lo + j_hi) >> 1
            c = count(lambda t, j: (t == tau) & (key0 + j * TK <= j_mid))
            ok = c >= need
            return jnp.where(ok, j_lo, j_mid), jnp.where(ok, j_mid, j_hi)

        _, j_cut = lax.fori_loop(0, n_steps, step, (jnp.full((1, TQ), -1, jnp.int32), qpos))

        def fix(j, carry):
            t = sc_s[tile_rows(j), :]
            drop = excess & (t == tau) & (key0 + j * TK > j_cut)
            sc_s[tile_rows(j), :] = jnp.where(drop, neg_inf, t)
            return carry

        lax.fori_loop(0, n_tiles, fix, 0)

    c_exp = (DSA_HEAD_DIM ** -0.5) * math.log2(math.e)
    def attn_logits(jh, buf):
        buf[...] = _dot(ckv_ref[half_rows(jh), :], ql_ref[...])

    ql_f = ql_ref[...].astype(F32)
    bound = jnp.sqrt(jnp.sum(ql_f * ql_f, axis=0, keepdims=True)) * cmax_ref[...]

    def fast_numerators(jh, buf, p_buf):
        keep = jnp.where(sc_s[half_rows(jh), :] >= tau, 1.0, 0.0).astype(BF16)
        for hh in range(H):
            cols = slice(hh * TQ, (hh + 1) * TQ)
            p = jnp.exp2((buf[:, cols] - bound[:, cols]) * c_exp).astype(BF16)
            p_buf[:, cols] = p * keep

    def fast_values(jh, p_buf):
        cols_j = pl.ds(pl.multiple_of(jh * TH, TH), TH)
        acc_s[...] += _dot(ckvt_ref[:, cols_j], p_buf[...])

    def fast_pair(j, carry):
        attn_logits(2 * j + 1, lg_b)
        fast_numerators(2 * j, lg_a, p_a)
        fast_values(jnp.maximum(2 * j - 1, 0), p_b)
        attn_logits(jnp.minimum(2 * j + 2, last_half), lg_a)
        fast_numerators(2 * j + 1, lg_b, p_b)
        fast_values(2 * j, p_a)
        return carry

    acc_s[...] = jnp.zeros_like(acc_s)
    p_b[...] = jnp.zeros_like(p_b)
    attn_logits(0, lg_a)
    lax.fori_loop(0, n_tiles, fast_pair, 0)
    fast_values(last_half, p_b)
    denom_ok = acc_s[RKV:RKV + 1, :] >= DENOM_FLOOR
    underflowed = jnp.sum(jnp.where(denom_ok, 0, 1)) > 0

    def attn_softmax(jh, buf, p_buf):
        bias_s[...] = jnp.where(sc_s[half_rows(jh), :] >= tau, 0.0, MASKED)
        alphas = []
        for hh in range(H):
            cols = slice(hh * TQ, (hh + 1) * TQ)
            m_old = m_s[hh:hh + 1, :]
            mx = _fold(buf[:, cols] + bias_s[...], jnp.maximum)
            m_new = jnp.maximum(m_old, jnp.max(mx, axis=0, keepdims=True))
            m_s[hh:hh + 1, :] = m_new
            alphas.append(jnp.exp2((m_old - m_new) * c_exp))
            p_buf[:, cols] = jnp.exp2((buf[:, cols] + bias_s[...] - m_new) * c_exp).astype(BF16)
        return jnp.concatenate(alphas, axis=1)

    def attn_values(jh, p_buf, alpha):
        cols_j = pl.ds(pl.multiple_of(jh * TH, TH), TH)
        acc_s[...] = acc_s[...] * alpha + _dot(ckvt_ref[:, cols_j], p_buf[...])

    def attn_pair(j, alpha_b):
        attn_logits(2 * j + 1, lg_b)
        alpha_a = attn_softmax(2 * j, lg_a, p_a)
        attn_values(jnp.maximum(2 * j - 1, 0), p_b, alpha_b)
        attn_logits(jnp.minimum(2 * j + 2, last_half), lg_a)
        alpha_b = attn_softmax(2 * j + 1, lg_b, p_b)
        attn_values(2 * j, p_a, alpha_a)
        return alpha_b

    @pl.when(underflowed)
    def _():
        m_s[...] = jnp.full_like(m_s, MASKED)
        acc_s[...] = jnp.zeros_like(acc_s)
        p_b[...] = jnp.zeros_like(p_b)
        attn_logits(0, lg_a)
        alpha_b = lax.fori_loop(0, n_tiles, attn_pair, jnp.ones((1, H * TQ), F32))
        attn_values(last_half, p_b, alpha_b)

    o_lat = (acc_s[0:RKV, :] / acc_s[RKV:RKV + 1, :]).astype(BF16)
    o_t = jnp.concatenate(
        [_dot(wuvt_ref[hh], o_lat[:, hh * TQ:(hh + 1) * TQ]) for hh in range(H)], axis=0)
    out_t = _dot(woutt_ref[...], o_t.astype(BF16))
    o_ref[...] = x_ref[...] + out_t.T


def _dsa_attn(x, qi, wt, ql, kidx, ckv, ckvt, kv_norm, w_uv, w_out):
    b, s, d = x.shape
    H, RKV, HI, TQ = DSA_HEADS, DSA_KV_RANK, IDX_HEADS, ATT_Q
    nq = s // TQ
    top_k = min(TOPK_MAX, s // 4)
    wuvt = w_uv.transpose(1, 2, 0).astype(BF16)
    woutt = w_out.T.astype(BF16)
    cmax = (math.sqrt(RKV) * jnp.max(jnp.abs(kv_norm))).reshape(1, 1).astype(F32)
    qrow = pl.BlockSpec((None, TQ, d), lambda bi, i: (bi, i, 0))
    qblk = lambda r, c: pl.BlockSpec((None, r, c), lambda bi, i: (bi * nq + i, 0, 0))
    seq = lambda width: pl.BlockSpec((s, width), lambda bi, i: (bi, 0))
    return pl.pallas_call(
        functools.partial(_dsa_attn_kernel, top_k=top_k),
        grid=(b, nq),
        in_specs=[qrow, qblk(LANES, HI * TQ), qblk(HI, TQ), qblk(RKV, H * TQ), seq(LANES), seq(RKV),
                  pl.BlockSpec((KV_ROWS, s), lambda bi, i: (0, bi)),
                  _resident(wuvt.shape), _resident(woutt.shape), _resident((1, 1))],
        out_specs=qrow,
        out_shape=jax.ShapeDtypeStruct((b, s, d), F32),
        scratch_shapes=[
            pltpu.VMEM((s, TQ), F32),
            pltpu.VMEM((ATT_K // 2, H * TQ), F32),
            pltpu.VMEM((ATT_K // 2, H * TQ), F32),
            pltpu.VMEM((ATT_K // 2, H * TQ), BF16),
            pltpu.VMEM((ATT_K // 2, H * TQ), BF16),
            pltpu.VMEM((ATT_K // 2, TQ), F32),
            pltpu.VMEM((KV_ROWS, H * TQ), F32),
            pltpu.VMEM((H, TQ), F32),
        ],
        compiler_params=_params(2),
        name="dsa_attn",
    )(x, qi, wt, ql, kidx, ckv, ckvt, wuvt, woutt, cmax)


def _dsa(x, g, w_in, q_norm, kv_norm, w_uq, w_qidx, kidx_g, kidx_b, w_uk, w_uv, w_out):
    b, s, d = x.shape
    qi, wt, ql, kidx, ckv, ckvt = _dsa_proj(x.reshape(b * s, d), g, w_in, q_norm, kv_norm, w_uq,
                                            w_qidx, kidx_g, kidx_b, w_uk)
    return _dsa_attn(x, qi, wt, ql, kidx, ckv, ckvt, kv_norm, w_uv, w_out)


def kernel(x, norm_ffn1, w_ffn1_in, w_ffn1_out, norm_mix, norm_ffn2, w_ffn2_in, w_ffn2_out, ret_w_in,
           ret_w_out, dsa_w_in, dsa_q_norm, dsa_kv_norm, dsa_w_uq, dsa_w_qidx, dsa_kidx_g, dsa_kidx_b,
           dsa_w_uk, dsa_w_uv, dsa_w_out, final_norm):
    b, s, d = x.shape
    depth = norm_ffn1.shape[0]
    for layer in range(depth):
        x = _ffn(x.reshape(b * s, d), norm_ffn1[layer], w_ffn1_in[layer], w_ffn1_out[layer])
        x = x.reshape(b, s, d)
        j = layer // 2
        if layer % 2 == 0:
            x = _retention(x, norm_mix[layer], ret_w_in[j], ret_w_out[j])
        else:
            x = _dsa(x, norm_mix[layer], dsa_w_in[j], dsa_q_norm[j], dsa_kv_norm[j], dsa_w_uq[j],
                     dsa_w_qidx[j], dsa_kidx_g[j], dsa_kidx_b[j], dsa_w_uk[j], dsa_w_uv[j],
                     dsa_w_out[j])
        last = layer == depth - 1
        x = _ffn(x.reshape(b * s, d), norm_ffn2[layer], w_ffn2_in[layer], w_ffn2_out[layer],
                 final_g=final_norm if last else None)
        x = x.reshape(b, s, d)
    return x
```

```python
import functools
import math

import jax
import jax.numpy as jnp
import numpy as np
from jax import lax
from jax.experimental import pallas as pl
from jax.experimental.pallas import tpu as pltpu

F32 = jnp.float32
BF16 = jnp.bfloat16

D_MODEL = 1024
DEPTH = 4
D_FF = 2816
RMS_EPS = 1e-6

RET_HEADS = 4
RET_DK = D_MODEL // RET_HEADS
RET_DV = 2 * RET_DK
RET_CHUNK = 128
ROT_BASE = 10000.0

DSA_HEADS = 8
DSA_HEAD_DIM = D_MODEL // DSA_HEADS
DSA_Q_RANK = 256
DSA_KV_RANK = 256
IDX_HEADS = 8
IDX_DIM = 64
TOPK_MAX = 256

LANES = 128
SUBLANES = 8
BF16_ROWS = 16
VMEM_LIMIT_BYTES = 56 * 1024 * 1024

FFN_ROWS = 512
FFN_CHUNK = 256
RET_ROWS = 512
PROJ_ROWS = 512
ATT_Q = LANES
ATT_K = 512
KV_ROWS = DSA_KV_RANK + BF16_ROWS
MASKED = -1e30
SEARCH_CAP = 400
SEARCH_UNCHECKED = 9
FINISH_RANKS = 4
DENOM_FLOOR = 2.0 ** -80
REDUCE_WAYS = 4
COLUMN_GROUPS = 4


def _resident(shape):
    nd = len(shape)
    return pl.BlockSpec(shape, lambda *_: (0,) * nd, pipeline_mode=pl.Buffered(1))


def _params(n_grid, flags=None):
    return pltpu.CompilerParams(
        dimension_semantics=("arbitrary",) * n_grid, vmem_limit_bytes=VMEM_LIMIT_BYTES, flags=flags)


def _rms(x, g):
    y = x * lax.rsqrt(jnp.mean(x * x, axis=-1, keepdims=True) + RMS_EPS)
    return y * g


def _dot(a, b):
    return jnp.dot(a, b, preferred_element_type=F32)


def _dot_nt(a, b):
    return lax.dot_general(a, b, (((1,), (1,)), ((), ())), preferred_element_type=F32)


def _dot_tn(a, b):
    return lax.dot_general(a, b, (((0,), (0,)), ((), ())), preferred_element_type=F32)


def _ffn_kernel(x_ref, g_ref, win_ref, wo_ref, *rest, final):
    if final:
        fg_ref, o_ref, acc_ref = rest
    else:
        o_ref, acc_ref = rest
    x = x_ref[...]
    h = _rms(x, g_ref[...]).astype(BF16)
    acc_ref[...] = jnp.zeros_like(acc_ref)
    for c in range(D_FF // FFN_CHUNK):
        cols = slice(c * FFN_CHUNK, (c + 1) * FFN_CHUNK)
        gate = _dot(h, win_ref[:, cols])
        up = _dot(h, win_ref[:, D_FF + c * FFN_CHUNK:D_FF + (c + 1) * FFN_CHUNK])
        a = (gate * jax.nn.sigmoid(gate) * up).astype(BF16)
        acc_ref[...] += _dot(a, wo_ref[cols, :])
    y = x + 0.5 * acc_ref[...]
    if final:
        y = _rms(y, fg_ref[...])
    o_ref[...] = y


def _ffn(x2, g, w_in, w_out, final_g=None):
    n, d = x2.shape
    win = w_in.astype(BF16)
    wo = w_out.astype(BF16)
    row = pl.BlockSpec((FFN_ROWS, d), lambda i: (i, 0))
    ins = [x2, g.reshape(1, d), win, wo]
    specs = [row, _resident((1, d)), _resident(win.shape), _resident(wo.shape)]
    if final_g is not None:
        ins.append(final_g.reshape(1, d))
        specs.append(_resident((1, d)))
    return pl.pallas_call(
        functools.partial(_ffn_kernel, final=final_g is not None),
        grid=(n // FFN_ROWS,),
        in_specs=specs,
        out_specs=row,
        out_shape=jax.ShapeDtypeStruct((n, d), F32),
        scratch_shapes=[pltpu.VMEM((FFN_ROWS, d), F32)],
        compiler_params=_params(1),
        name="ffn_final" if final_g is not None else "ffn",
    )(*ins)


def _ret_kernel(x_ref, g_ref, win_ref, wout_ref, cos_ref, sin_ref, inner_ref, qdec_ref, kdec_ref,
                cdec_ref, o_ref, q_s, k_s, v_s, gate_s, z_s, state_s):
    H, dk, dv, C = RET_HEADS, RET_DK, RET_DV, RET_CHUNK
    half = dk // 2

    @pl.when(pl.program_id(1) == 0)
    def _():
        state_s[...] = jnp.zeros_like(state_s)

    x = x_ref[...]
    h = _rms(x, g_ref[...]).astype(BF16)
    cos = cos_ref[...]
    sin = sin_ref[...]
    q = _dot(h, win_ref[:, 0:H * dk])
    k = _dot(h, win_ref[:, H * dk:2 * H * dk])
    for hh in range(H):
        q0 = q[:, hh * dk:hh * dk + half]
        q1 = q[:, hh * dk + half:(hh + 1) * dk]
        q_s[:, hh * dk:hh * dk + half] = (q0 * cos - q1 * sin).astype(BF16)
        q_s[:, hh * dk + half:(hh + 1) * dk] = (q1 * cos + q0 * sin).astype(BF16)
        k0 = k[:, hh * dk:hh * dk + half]
        k1 = k[:, hh * dk + half:(hh + 1) * dk]
        k_s[:, hh * dk:hh * dk + half] = (k0 * cos - k1 * sin) * (dk ** -0.5)
        k_s[:, hh * dk + half:(hh + 1) * dk] = (k1 * cos + k0 * sin) * (dk ** -0.5)
    v_s[...] = _dot(h, win_ref[:, 2 * H * dk:2 * H * dk + H * dv]).astype(BF16)
    gate_s[...] = _dot(h, win_ref[:, 2 * H * dk + H * dv:])

    def chunk(c, carry):
        r0 = pl.multiple_of(c * C, C)
        rows = pl.ds(r0, C)
        for hh in range(H):
            qc = q_s[rows, hh * dk:(hh + 1) * dk]
            kc = k_s[rows, hh * dk:(hh + 1) * dk]
            vc = v_s[rows, hh * dv:(hh + 1) * dv]
            st = state_s[hh]
            scores = _dot_nt(qc, kc.astype(BF16)) * inner_ref[hh]
            inner = _dot(scores.astype(BF16), vc)
            cross = _dot(qc, st.astype(BF16)) * qdec_ref[hh]
            kd = (kc * kdec_ref[hh]).astype(BF16)
            state_s[hh] = st * cdec_ref[hh] + _dot_tn(kd, vc)
            out = inner + cross
            out = out * lax.rsqrt(jnp.mean(out * out, axis=-1, keepdims=True) + RMS_EPS)
            gt = gate_s[rows, hh * dv:(hh + 1) * dv]
            z_s[rows, hh * dv:(hh + 1) * dv] = (gt * jax.nn.sigmoid(gt) * out).astype(BF16)
        return carry

    lax.fori_loop(0, x.shape[0] // C, chunk, 0, unroll=True)
    o_ref[...] = x + _dot(z_s[...], wout_ref[...])


def _ret_tables():
    H, C = RET_HEADS, RET_CHUNK
    log_gamma = jnp.log(1.0 - 2.0 ** (-5.0 - jnp.arange(H, dtype=F32)))
    idx = jnp.arange(C, dtype=F32)
    diff = idx[:, None] - idx[None, :]
    inner = jnp.where(diff[None] >= 0,
                      jnp.exp(jnp.maximum(diff, 0.0)[None] * log_gamma[:, None, None]), 0.0)
    qdec = jnp.exp((idx[None, :] + 1.0) * log_gamma[:, None])[:, :, None]
    kdec = jnp.exp((C - 1.0 - idx[None, :]) * log_gamma[:, None])[:, :, None]
    cdec = jnp.exp(C * log_gamma)[:, None, None]
    return inner, qdec, kdec, cdec


def _rot_tables(s):
    pos = jnp.arange(s, dtype=F32)
    freq = 1.0 / (ROT_BASE ** jnp.linspace(0.0, 1.0, RET_DK // 2, dtype=F32))
    ang = pos[:, None] * freq[None, :]
    return jnp.cos(ang), jnp.sin(ang)


def _pair_split_columns(w, heads, dim):
    d = w.shape[0]
    w = w.reshape(d, heads, dim // 2, 2)
    return jnp.concatenate([w[..., 0], w[..., 1]], axis=-1).reshape(d, heads * dim)


def _retention(x, g, w_in, w_out):
    b, s, d = x.shape
    H, dk, dv = RET_HEADS, RET_DK, RET_DV
    tb = min(RET_ROWS, s)
    wq = _pair_split_columns(w_in[:, :H * dk], H, dk)
    wk = _pair_split_columns(w_in[:, H * dk:2 * H * dk], H, dk)
    win = jnp.concatenate([wq, wk, w_in[:, 2 * H * dk:]], axis=1).astype(BF16)
    wout = w_out.astype(BF16)
    cos, sin = _rot_tables(s)
    inner, qdec, kdec, cdec = _ret_tables()
    row = pl.BlockSpec((None, tb, d), lambda bi, i: (bi, i, 0))
    rot = pl.BlockSpec((tb, dk // 2), lambda bi, i: (i, 0))
    return pl.pallas_call(
        _ret_kernel,
        grid=(b, s // tb),
        in_specs=[row, _resident((1, d)), _resident(win.shape), _resident(wout.shape), rot, rot,
                  _resident(inner.shape), _resident(qdec.shape), _resident(kdec.shape),
                  _resident(cdec.shape)],
        out_specs=row,
        out_shape=jax.ShapeDtypeStruct((b, s, d), F32),
        scratch_shapes=[
            pltpu.VMEM((tb, H * dk), BF16),
            pltpu.VMEM((tb, H * dk), F32),
            pltpu.VMEM((tb, H * dv), BF16),
            pltpu.VMEM((tb, H * dv), F32),
            pltpu.VMEM((tb, H * dv), BF16),
            pltpu.VMEM((H, dk, dv), F32),
        ],
        compiler_params=_params(2),
        name="retention",
    )(x, g.reshape(1, d), win, wout, cos, sin, inner, qdec, kdec, cdec)


def _dsa_proj_kernel(x_ref, g_ref, win_ref, winw_ref, qn_ref, kvn_ref, kg_ref, kb_ref, wqi_ref,
                     wuq_ref, wuk_ref, qi_ref, wt_ref, ql_ref, kidx_ref, ckv_ref, ckvt_ref):
    RQ, RKV, HI, H, dh, TQ = DSA_Q_RANK, DSA_KV_RANK, IDX_HEADS, DSA_HEADS, DSA_HEAD_DIM, ATT_Q
    n_blk = x_ref.shape[0] // TQ
    h = _rms(x_ref[...], g_ref[...]).astype(BF16)
    proj = _dot(h, win_ref[...])
    c_q = _rms(proj[:, :RQ], qn_ref[...])
    c_kv = _rms(proj[:, RQ:RQ + RKV], kvn_ref[...])
    kx = proj[:, RQ + RKV:RQ + RKV + LANES]
    lane = lax.broadcasted_iota(jnp.int32, kx.shape, 1)
    mu = jnp.sum(kx, axis=-1, keepdims=True) * (1.0 / IDX_DIM)
    cen = jnp.where(lane < IDX_DIM, kx - mu, 0.0)
    var = jnp.sum(cen * cen, axis=-1, keepdims=True) * (1.0 / IDX_DIM)
    kidx_ref[...] = (cen * lax.rsqrt(var + RMS_EPS) * kg_ref[...] + kb_ref[...]).astype(BF16)
    ckv_ref[...] = c_kv.astype(BF16)
    ckvt_ref[0:RKV, :] = c_kv.T.astype(BF16)
    ones_row = lax.broadcasted_iota(jnp.int32, (BF16_ROWS, x_ref.shape[0]), 0) == 0
    ckvt_ref[RKV:, :] = jnp.where(ones_row, 1.0, 0.0).astype(BF16)
    w_t = _dot_nt(winw_ref[...], h) * (HI ** -0.5)
    cq = c_q.astype(BF16)
    qi_t = (_dot_nt(wqi_ref[...], cq) * (IDX_DIM ** -0.5)).astype(BF16)
    q = _dot(cq, wuq_ref[...]).astype(BF16)
    ql_t = [_dot_nt(wuk_ref[hh], q[:, hh * dh:(hh + 1) * dh]).astype(BF16) for hh in range(H)]
    for u in range(n_blk):
        cols = slice(u * TQ, (u + 1) * TQ)
        wt_ref[u] = w_t[0:HI, cols]
        for hh in range(HI):
            qi_ref[u, :, hh * TQ:(hh + 1) * TQ] = qi_t[hh * LANES:(hh + 1) * LANES, cols]
        for hh in range(H):
            ql_ref[u, :, hh * TQ:(hh + 1) * TQ] = ql_t[hh][:, cols]


def _dsa_proj(x2, g, w_in, q_norm, kv_norm, w_uq, w_qidx, kidx_g, kidx_b, w_uk):
    n, d = x2.shape
    RQ, RKV, HI, DI, H, TQ = DSA_Q_RANK, DSA_KV_RANK, IDX_HEADS, IDX_DIM, DSA_HEADS, ATT_Q
    pad_k = jnp.zeros((d, LANES - DI), F32)
    win = jnp.concatenate([w_in[:, :RQ + RKV + DI], pad_k], axis=1).astype(BF16)
    winw = jnp.pad(w_in[:, RQ + RKV + DI:].T, ((0, BF16_ROWS - HI), (0, 0))).astype(BF16)
    kg = jnp.pad(kidx_g, (0, LANES - DI)).reshape(1, LANES)
    kb = jnp.pad(kidx_b, (0, LANES - DI)).reshape(1, LANES)
    wqi = jnp.pad(w_qidx.reshape(RQ, HI, DI), ((0, 0), (0, 0), (0, LANES - DI)))
    wqi = wqi.reshape(RQ, HI * LANES).T.astype(BF16)
    wuq = w_uq.astype(BF16)
    wuk = w_uk.transpose(1, 0, 2).astype(BF16)
    tm = min(PROJ_ROWS, n)
    nb = tm // TQ
    row = lambda w: pl.BlockSpec((tm, w), lambda i: (i, 0))
    blk = lambda r, c: pl.BlockSpec((nb, r, c), lambda i: (i, 0, 0))
    return pl.pallas_call(
        _dsa_proj_kernel,
        grid=(n // tm,),
        in_specs=[row(d), _resident((1, d)), _resident(win.shape), _resident(winw.shape),
                  _resident((1, RQ)), _resident((1, RKV)), _resident((1, LANES)),
                  _resident((1, LANES)), _resident(wqi.shape), _resident(wuq.shape),
                  _resident(wuk.shape)],
        out_specs=[blk(LANES, HI * TQ), blk(HI, TQ), blk(RKV, H * TQ), row(LANES), row(RKV),
                   pl.BlockSpec((KV_ROWS, tm), lambda i: (0, i))],
        out_shape=[
            jax.ShapeDtypeStruct((n // TQ, LANES, HI * TQ), BF16),
            jax.ShapeDtypeStruct((n // TQ, HI, TQ), F32),
            jax.ShapeDtypeStruct((n // TQ, RKV, H * TQ), BF16),
            jax.ShapeDtypeStruct((n, LANES), BF16),
            jax.ShapeDtypeStruct((n, RKV), BF16),
            jax.ShapeDtypeStruct((KV_ROWS, n), BF16),
        ],
        compiler_params=_params(1),
        name="dsa_proj",
    )(x2, g.reshape(1, d), win, winw, q_norm.reshape(1, RQ), kv_norm.reshape(1, RKV), kg, kb, wqi,
      wuq, wuk)


def _fold(t, op):
    groups = t.shape[0] // SUBLANES
    group = lambda r: t[r * SUBLANES:(r + 1) * SUBLANES, :]
    acc = [group(w) for w in range(REDUCE_WAYS)]
    for r in range(REDUCE_WAYS, groups, REDUCE_WAYS):
        acc = [op(acc[w], group(r + w)) for w in range(REDUCE_WAYS)]
    while len(acc) > 1:
        acc = [op(acc[2 * w], acc[2 * w + 1]) for w in range(len(acc) // 2)]
    return acc[0]


def _fori_by_two(n, body, init):
    carry = lax.fori_loop(0, n // 2, lambda q, c: body(2 * q + 1, body(2 * q, c)), init)
    return lax.fori_loop(2 * (n // 2), n, body, carry)


def _dsa_attn_kernel(x_ref, qi_ref, wt_ref, ql_ref, kidx_ref, ckv_ref, ckvt_ref, wuvt_ref,
                     woutt_ref, cmax_ref, o_ref, sc_s, lg_a, lg_b, p_a, p_b, bias_s, acc_s, m_s, *, top_k):
    TQ, TK, TH = ATT_Q, ATT_K, ATT_K // 2
    HI, H, RKV = IDX_HEADS, DSA_HEADS, DSA_KV_RANK
    i = pl.program_id(1)
    n_tiles = (i * TQ + TQ + TK - 1) // TK
    qpos = i * TQ + lax.broadcasted_iota(jnp.int32, (1, TQ), 1)
    key0 = lax.broadcasted_iota(jnp.int32, (TK, 1), 0)
    keyh = lax.broadcasted_iota(jnp.int32, (TH, 1), 0)
    neg_inf = jnp.float32(-jnp.inf)
    kf = jnp.float32(top_k)

    def tile_rows(j):
        return pl.ds(pl.multiple_of(j * TK, TK), TK)

    w_t = wt_ref[...]
    last_half = 2 * n_tiles - 1

    def half_rows(jh):
        return pl.ds(pl.multiple_of(jh * TH, TH), TH)

    def score_dots(jh, buf):
        buf[...] = _dot(kidx_ref[half_rows(jh), :], qi_ref[...])

    def score_reduce(jh, buf, carry):
        rmax, rmin, c_ge0, c_gt0 = carry
        s = jnp.maximum(buf[:, 0:TQ], 0.0) * w_t[0:1, :]
        for hh in range(1, HI):
            s = s + jnp.maximum(buf[:, hh * TQ:(hh + 1) * TQ], 0.0) * w_t[hh:hh + 1, :]
        causal = keyh + jh * TH <= qpos
        s = jnp.where(causal, s, neg_inf)
        sc_s[half_rows(jh), :] = s
        rmax = jnp.maximum(rmax, _fold(s, jnp.maximum))
        rmin = jnp.minimum(rmin, _fold(jnp.where(causal, s, jnp.inf), jnp.minimum))
        c_ge0 = c_ge0 + _fold(jnp.where(s >= 0.0, 1.0, 0.0), jnp.add)
        c_gt0 = c_gt0 + _fold(jnp.where(s > 0.0, 1.0, 0.0), jnp.add)
        return rmax, rmin, c_ge0, c_gt0

    def score_pair(j, carry):
        score_dots(2 * j + 1, lg_b)
        carry = score_reduce(2 * j, lg_a, carry)
        score_dots(jnp.minimum(2 * j + 2, last_half), lg_a)
        return score_reduce(2 * j + 1, lg_b, carry)

    part = lambda v: jnp.full((SUBLANES, TQ), v, F32)
    score_dots(0, lg_a)
    rmax, rmin, c_ge0, c_gt0 = _fori_by_two(
        n_tiles, score_pair, (part(-jnp.inf), part(jnp.inf), part(0.0), part(0.0)))
    rmax = jnp.max(rmax, axis=0, keepdims=True)
    rmin = jnp.min(rmin, axis=0, keepdims=True)
    c_ge0 = jnp.sum(c_ge0, axis=0, keepdims=True)
    c_gt0 = jnp.sum(c_gt0, axis=0, keepdims=True)

    def count(pred):
        def body(j, acc):
            return acc + _fold(jnp.where(pred(sc_s[tile_rows(j), :], j), 1.0, 0.0), jnp.add)
        return jnp.sum(lax.fori_loop(0, n_tiles, body, part(0.0)), axis=0, keepdims=True)

    n_valid = (qpos + 1).astype(F32)
    all_sel = n_valid <= kf
    zero_tie = (c_gt0 < kf) & (c_ge0 >= kf)
    positive = c_gt0 >= kf
    lo = jnp.where(positive | zero_tie, 0.0, rmin)
    hi = jnp.where(positive, 2.0 * rmax, 0.0)
    clo = jnp.where(positive | zero_tie, c_ge0, n_valid)
    chi = jnp.where(positive, 0.0, c_ge0)
    lo = jnp.where(all_sel, jnp.float32(jnp.finfo(jnp.float32).min), lo)
    clo = jnp.where(all_sel, n_valid, clo)
    done = all_sel | zero_tie | (clo == kf)
    one = jnp.ones((1, TQ), F32)
    log_target = math.log(top_k + 0.5)

    def pending(done):
        return jnp.sum(jnp.where(done, 0, 1))

    def search_cond(c):
        return (c[1] > 0) & (c[0] < SEARCH_CAP)

    def search_step(state):
        lo, hi, clo, chi, wl, wh, side, done_f = state
        done = done_f > 0.0
        mid = 0.5 * lo + 0.5 * hi
        fa = (jnp.log(clo) - log_target) * wl
        fb = (log_target - jnp.log(jnp.maximum(chi, 0.5))) * wh
        cand = lo + (hi - lo) * (fa / (fa + fb))
        cand = jnp.where((cand > lo) & (cand < hi), cand, mid)
        collapsed = (cand <= lo) | (cand >= hi)
        cm = count(lambda t, j: t >= cand)
        move = jnp.logical_not(done | collapsed)
        up = move & (cm >= kf)
        down = move & (cm < kf)
        wh = jnp.where(up, jnp.where(side > 0.0, 0.5 * wh, 1.0), jnp.where(down, 1.0, wh))
        wl = jnp.where(down, jnp.where(side < 0.0, 0.5 * wl, 1.0), jnp.where(up, 1.0, wl))
        side = jnp.where(up, 1.0, jnp.where(down, -1.0, side))
        lo = jnp.where(up, cand, lo)
        clo = jnp.where(up, cm, clo)
        hi = jnp.where(down, cand, hi)
        chi = jnp.where(down, cm, chi)
        done = done | collapsed | (clo == kf)
        return lo, hi, clo, chi, wl, wh, side, jnp.where(done, 1.0, 0.0)

    def search_body(c):
        state = search_step(c[2:])
        return (c[0] + 1, pending(state[-1] > 0.0)) + state

    state = (lo, hi, clo, chi, one, one, 0.0 * one, jnp.where(done, 1.0, 0.0))
    state = lax.fori_loop(0, SEARCH_UNCHECKED, lambda _, s: search_step(s), state)

    lo, hi, clo, chi, wl, wh, side, done_f = state
    inf = jnp.float32(jnp.inf)

    def insert(ranks, v):
        out = []
        for r in ranks:
            out.append(jnp.minimum(r, v))
            v = jnp.maximum(r, v)
        return out

    def smallest_body(j, ranks):
        t = sc_s[tile_rows(j), :]
        ranks = list(ranks)
        for g in range(TK // SUBLANES):
            v = t[g * SUBLANES:(g + 1) * SUBLANES, :]
            w = g % 2
            ranks[w] = tuple(insert(ranks[w], jnp.where(v >= lo, v, inf)))
        return tuple(ranks)

    empty = tuple(jnp.full((SUBLANES, TQ), inf, F32) for _ in range(FINISH_RANKS))
    ranks = lax.fori_loop(0, n_tiles, smallest_body, (empty, empty))
    final = [jnp.full((1, TQ), inf, F32) for _ in range(FINISH_RANKS)]
    for chain in ranks:
        for r in chain:
            for sub in range(SUBLANES):
                final = insert(final, r[sub:sub + 1, :])
    extra = clo - kf
    kth = final[0]
    below = -inf
    for e in range(1, FINISH_RANKS):
        kth = jnp.where(extra >= e, final[e], kth)
        below = jnp.where(extra == e, final[e - 1], below)
    near = (done_f <= 0.0) & (extra < FINISH_RANKS)
    lo = jnp.where(near, kth, lo)
    clo = jnp.where(near, jnp.where(below == kth, kf + 1.0, kf), clo)
    done_f = jnp.where(near, 1.0, done_f)
    state = (lo, hi, clo, chi, wl, wh, side, done_f)

    res = lax.while_loop(search_cond, search_body,
                         (jnp.int32(0), pending(state[-1] > 0.0)) + state)
    tau, clo = res[2], res[4]

    excess = clo > kf

    @pl.when(jnp.sum(jnp.where(excess, 1, 0)) > 0)
    def _():
        need = kf - count(lambda t, j: t > tau)
        n_steps = max(1, math.ceil(math.log2(sc_s.shape[0]))) + 1

        def step(_, jb):
            j_lo, j_hi = jb
            j_mid = (j_lo + j_hi) >> 1
            c = count(lambda t, j: (t == tau) & (key0 + j * TK <= j_mid))
            ok = c >= need
            return jnp.where(ok, j_lo, j_mid), jnp.where(ok, j_mid, j_hi)

        _, j_cut = lax.fori_loop(0, n_steps, step, (jnp.full((1, TQ), -1, jnp.int32), qpos))

        def fix(j, carry):
            t = sc_s[tile_rows(j), :]
            drop = excess & (t == tau) & (key0 + j * TK > j_cut)
            sc_s[tile_rows(j), :] = jnp.where(drop, neg_inf, t)
            return carry

        lax.fori_loop(0, n_tiles, fix, 0)

    c_exp = (DSA_HEAD_DIM ** -0.5) * math.log2(math.e)
    def attn_logits(jh, buf):
        buf[...] = _dot(ckv_ref[half_rows(jh), :], ql_ref[...])

    ql_f = ql_ref[...].astype(F32)
    bound = jnp.sqrt(jnp.sum(ql_f * ql_f, axis=0, keepdims=True)) * cmax_ref[...]

    def fast_values(jh, p_buf):
        cols_j = pl.ds(pl.multiple_of(jh * TH, TH), TH)
        acc_s[...] += _dot(ckvt_ref[:, cols_j], p_buf[...])

    def fast_half(jh, buf, p_buf, jh_next, buf_next, jh_prev, p_prev):
        keep = jnp.where(sc_s[half_rows(jh), :] >= tau, 1.0, 0.0).astype(BF16)
        keys_next = ckv_ref[half_rows(jh_next), :]
        vals_prev = ckvt_ref[:, pl.ds(pl.multiple_of(jh_prev * TH, TH), TH)]
        per_group = H // COLUMN_GROUPS
        for grp in range(COLUMN_GROUPS):
            gcols = slice(grp * per_group * TQ, (grp + 1) * per_group * TQ)
            buf_next[:, gcols] = _dot(keys_next, ql_ref[:, gcols])
            for hh in range(grp * per_group, (grp + 1) * per_group):
                cols = slice(hh * TQ, (hh + 1) * TQ)
                p = jnp.exp2((buf[:, cols] - bound[:, cols]) * c_exp).astype(BF16)
                p_buf[:, cols] = p * keep
            acc_s[:, gcols] += _dot(vals_prev, p_prev[:, gcols])

    def fast_pair(j, carry):
        fast_half(2 * j, lg_a, p_a, 2 * j + 1, lg_b, jnp.maximum(2 * j - 1, 0), p_b)
        fast_half(2 * j + 1, lg_b, p_b, jnp.minimum(2 * j + 2, last_half), lg_a, 2 * j, p_a)
        return carry

    acc_s[...] = jnp.zeros_like(acc_s)
    p_b[...] = jnp.zeros_like(p_b)
    attn_logits(0, lg_a)
    _fori_by_two(n_tiles, fast_pair, 0)
    fast_values(last_half, p_b)
    denom_ok = acc_s[RKV:RKV + 1, :] >= DENOM_FLOOR
    underflowed = jnp.sum(jnp.where(denom_ok, 0, 1)) > 0

    def attn_softmax(jh, buf, p_buf):
        bias_s[...] = jnp.where(sc_s[half_rows(jh), :] >= tau, 0.0, MASKED)
        alphas = []
        for hh in range(H):
            cols = slice(hh * TQ, (hh + 1) * TQ)
            m_old = m_s[hh:hh + 1, :]
            mx = _fold(buf[:, cols] + bias_s[...], jnp.maximum)
            m_new = jnp.maximum(m_old, jnp.max(mx, axis=0, keepdims=True))
            m_s[hh:hh + 1, :] = m_new
            alphas.append(jnp.exp2((m_old - m_new) * c_exp))
            p_buf[:, cols] = jnp.exp2((buf[:, cols] + bias_s[...] - m_new) * c_exp).astype(BF16)
        return jnp.concatenate(alphas, axis=1)

    def attn_values(jh, p_buf, alpha):
        cols_j = pl.ds(pl.multiple_of(jh * TH, TH), TH)
        acc_s[...] = acc_s[...] * alpha + _dot(ckvt_ref[:, cols_j], p_buf[...])

    def attn_pair(j, alpha_b):
        attn_logits(2 * j + 1, lg_b)
        alpha_a = attn_softmax(2 * j, lg_a, p_a)
        attn_values(jnp.maximum(2 * j - 1, 0), p_b, alpha_b)
        attn_logits(jnp.minimum(2 * j + 2, last_half), lg_a)
        alpha_b = attn_softmax(2 * j + 1, lg_b, p_b)
        attn_values(2 * j, p_a, alpha_a)
        return alpha_b

    @pl.when(underflowed)
    def _():
        m_s[...] = jnp.full_like(m_s, MASKED)
        acc_s[...] = jnp.zeros_like(acc_s)
        p_b[...] = jnp.zeros_like(p_b)
        attn_logits(0, lg_a)
        alpha_b = lax.fori_loop(0, n_tiles, attn_pair, jnp.ones((1, H * TQ), F32))
        attn_values(last_half, p_b, alpha_b)

    o_lat = (acc_s[0:RKV, :] / acc_s[RKV:RKV + 1, :]).astype(BF16)
    o_t = jnp.concatenate(
        [_dot(wuvt_ref[hh], o_lat[:, hh * TQ:(hh + 1) * TQ]) for hh in range(H)], axis=0)
    out_t = _dot(woutt_ref[...], o_t.astype(BF16))
    o_ref[...] = x_ref[...] + out_t.T


def _dsa_attn(x, qi, wt, ql, kidx, ckv, ckvt, kv_norm, w_uv, w_out):
    b, s, d = x.shape
    H, RKV, HI, TQ = DSA_HEADS, DSA_KV_RANK, IDX_HEADS, ATT_Q
    nq = s // TQ
    top_k = min(TOPK_MAX, s // 4)
    wuvt = w_uv.transpose(1, 2, 0).astype(BF16)
    woutt = w_out.T.astype(BF16)
    cmax = (math.sqrt(RKV) * jnp.max(jnp.abs(kv_norm))).reshape(1, 1).astype(F32)
    qrow = pl.BlockSpec((None, TQ, d), lambda bi, i: (bi, i, 0))
    qblk = lambda r, c: pl.BlockSpec((None, r, c), lambda bi, i: (bi * nq + i, 0, 0))
    seq = lambda width: pl.BlockSpec((s, width), lambda bi, i: (bi, 0))
    return pl.pallas_call(
        functools.partial(_dsa_attn_kernel, top_k=top_k),
        grid=(b, nq),
        in_specs=[qrow, qblk(LANES, HI * TQ), qblk(HI, TQ), qblk(RKV, H * TQ), seq(LANES), seq(RKV),
                  pl.BlockSpec((KV_ROWS, s), lambda bi, i: (0, bi)),
                  _resident(wuvt.shape), _resident(woutt.shape), _resident((1, 1))],
        out_specs=qrow,
        out_shape=jax.ShapeDtypeStruct((b, s, d), F32),
        scratch_shapes=[
            pltpu.VMEM((s, TQ), F32),
            pltpu.VMEM((ATT_K // 2, H * TQ), F32),
            pltpu.VMEM((ATT_K // 2, H * TQ), F32),
            pltpu.VMEM((ATT_K // 2, H * TQ), BF16),
            pltpu.VMEM((ATT_K // 2, H * TQ), BF16),
            pltpu.VMEM((ATT_K // 2, TQ), F32),
            pltpu.VMEM((KV_ROWS, H * TQ), F32),
            pltpu.VMEM((H, TQ), F32),
        ],
        compiler_params=_params(2),
        name="dsa_attn",
    )(x, qi, wt, ql, kidx, ckv, ckvt, wuvt, woutt, cmax)


def _dsa(x, g, w_in, q_norm, kv_norm, w_uq, w_qidx, kidx_g, kidx_b, w_uk, w_uv, w_out):
    b, s, d = x.shape
    qi, wt, ql, kidx, ckv, ckvt = _dsa_proj(x.reshape(b * s, d), g, w_in, q_norm, kv_norm, w_uq,
                                            w_qidx, kidx_g, kidx_b, w_uk)
    return _dsa_attn(x, qi, wt, ql, kidx, ckv, ckvt, kv_norm, w_uv, w_out)


def kernel(x, norm_ffn1, w_ffn1_in, w_ffn1_out, norm_mix, norm_ffn2, w_ffn2_in, w_ffn2_out, ret_w_in,
           ret_w_out, dsa_w_in, dsa_q_norm, dsa_kv_norm, dsa_w_uq, dsa_w_qidx, dsa_kidx_g, dsa_kidx_b,
           dsa_w_uk, dsa_w_uv, dsa_w_out, final_norm):
    b, s, d = x.shape
    depth = norm_ffn1.shape[0]
    for layer in range(depth):
        x = _ffn(x.reshape(b * s, d), norm_ffn1[layer], w_ffn1_in[layer], w_ffn1_out[layer])
        x = x.reshape(b, s, d)
        j = layer // 2
        if layer % 2 == 0:
            x = _retention(x, norm_mix[layer], ret_w_in[j], ret_w_out[j])
        else:
            x = _dsa(x, norm_mix[layer], dsa_w_in[j], dsa_q_norm[j], dsa_kv_norm[j], dsa_w_uq[j],
                     dsa_w_qidx[j], dsa_kidx_g[j], dsa_kidx_b[j], dsa_w_uk[j], dsa_w_uv[j],
                     dsa_w_out[j])
        last = layer == depth - 1
        x = _ffn(x.reshape(b * s, d), norm_ffn2[layer], w_ffn2_in[layer], w_ffn2_out[layer],
                 final_g=final_norm if last else None)
        x = x.reshape(b, s, d)
    return x
```

```python
import functools
import math

import jax
import jax.numpy as jnp
import numpy as np
from jax import lax
from jax.experimental import pallas as pl
from jax.experimental.pallas import tpu as pltpu

F32 = jnp.float32
BF16 = jnp.bfloat16

D_MODEL = 1024
DEPTH = 4
D_FF = 2816
RMS_EPS = 1e-6

RET_HEADS = 4
RET_DK = D_MODEL // RET_HEADS
RET_DV = 2 * RET_DK
RET_CHUNK = 128
ROT_BASE = 10000.0

DSA_HEADS = 8
DSA_HEAD_DIM = D_MODEL // DSA_HEADS
DSA_Q_RANK = 256
DSA_KV_RANK = 256
IDX_HEADS = 8
IDX_DIM = 64
TOPK_MAX = 256

LANES = 128
SUBLANES = 8
BF16_ROWS = 16
VMEM_LIMIT_BYTES = 56 * 1024 * 1024

FFN_ROWS = 512
FFN_CHUNK = 256
RET_ROWS = 512
PROJ_ROWS = 512
ATT_Q = LANES
ATT_K = 512
KV_ROWS = DSA_KV_RANK + BF16_ROWS
MASKED = -1e30
SEARCH_CAP = 400
SEARCH_UNCHECKED = 9
FINISH_RANKS = 4
DENOM_FLOOR = 2.0 ** -80
REDUCE_WAYS = 4
LOOP_GROUP = 4
COLUMN_GROUPS = 4


def _resident(shape):
    nd = len(shape)
    return pl.BlockSpec(shape, lambda *_: (0,) * nd, pipeline_mode=pl.Buffered(1))


def _params(n_grid, flags=None):
    return pltpu.CompilerParams(
        dimension_semantics=("arbitrary",) * n_grid, vmem_limit_bytes=VMEM_LIMIT_BYTES, flags=flags)


def _rms(x, g):
    y = x * lax.rsqrt(jnp.mean(x * x, axis=-1, keepdims=True) + RMS_EPS)
    return y * g


def _dot(a, b):
    return jnp.dot(a, b, preferred_element_type=F32)


def _dot_nt(a, b):
    return lax.dot_general(a, b, (((1,), (1,)), ((), ())), preferred_element_type=F32)


def _dot_tn(a, b):
    return lax.dot_general(a, b, (((0,), (0,)), ((), ())), preferred_element_type=F32)


def _ffn_kernel(x_ref, g_ref, win_ref, wo_ref, *rest, final):
    if final:
        fg_ref, o_ref, acc_ref = rest
    else:
        o_ref, acc_ref = rest
    x = x_ref[...]
    h = _rms(x, g_ref[...]).astype(BF16)
    acc_ref[...] = jnp.zeros_like(acc_ref)
    for c in range(D_FF // FFN_CHUNK):
        cols = slice(c * FFN_CHUNK, (c + 1) * FFN_CHUNK)
        gate = _dot(h, win_ref[:, cols])
        up = _dot(h, win_ref[:, D_FF + c * FFN_CHUNK:D_FF + (c + 1) * FFN_CHUNK])
        a = (gate * jax.nn.sigmoid(gate) * up).astype(BF16)
        acc_ref[...] += _dot(a, wo_ref[cols, :])
    y = x + 0.5 * acc_ref[...]
    if final:
        y = _rms(y, fg_ref[...])
    o_ref[...] = y


def _ffn(x2, g, w_in, w_out, final_g=None):
    n, d = x2.shape
    win = w_in.astype(BF16)
    wo = w_out.astype(BF16)
    row = pl.BlockSpec((FFN_ROWS, d), lambda i: (i, 0))
    ins = [x2, g.reshape(1, d), win, wo]
    specs = [row, _resident((1, d)), _resident(win.shape), _resident(wo.shape)]
    if final_g is not None:
        ins.append(final_g.reshape(1, d))
        specs.append(_resident((1, d)))
    return pl.pallas_call(
        functools.partial(_ffn_kernel, final=final_g is not None),
        grid=(n // FFN_ROWS,),
        in_specs=specs,
        out_specs=row,
        out_shape=jax.ShapeDtypeStruct((n, d), F32),
        scratch_shapes=[pltpu.VMEM((FFN_ROWS, d), F32)],
        compiler_params=_params(1),
        name="ffn_final" if final_g is not None else "ffn",
    )(*ins)


def _ret_kernel(x_ref, g_ref, win_ref, wout_ref, cos_ref, sin_ref, inner_ref, qdec_ref, kdec_ref,
                cdec_ref, o_ref, q_s, k_s, v_s, gate_s, z_s, state_s):
    H, dk, dv, C = RET_HEADS, RET_DK, RET_DV, RET_CHUNK
    half = dk // 2

    @pl.when(pl.program_id(1) == 0)
    def _():
        state_s[...] = jnp.zeros_like(state_s)

    x = x_ref[...]
    h = _rms(x, g_ref[...]).astype(BF16)
    cos = cos_ref[...]
    sin = sin_ref[...]
    q = _dot(h, win_ref[:, 0:H * dk])
    k = _dot(h, win_ref[:, H * dk:2 * H * dk])
    for hh in range(H):
        q0 = q[:, hh * dk:hh * dk + half]
        q1 = q[:, hh * dk + half:(hh + 1) * dk]
        q_s[:, hh * dk:hh * dk + half] = (q0 * cos - q1 * sin).astype(BF16)
        q_s[:, hh * dk + half:(hh + 1) * dk] = (q1 * cos + q0 * sin).astype(BF16)
        k0 = k[:, hh * dk:hh * dk + half]
        k1 = k[:, hh * dk + half:(hh + 1) * dk]
        k_s[:, hh * dk:hh * dk + half] = (k0 * cos - k1 * sin) * (dk ** -0.5)
        k_s[:, hh * dk + half:(hh + 1) * dk] = (k1 * cos + k0 * sin) * (dk ** -0.5)
    v_s[...] = _dot(h, win_ref[:, 2 * H * dk:2 * H * dk + H * dv]).astype(BF16)
    gate_s[...] = _dot(h, win_ref[:, 2 * H * dk + H * dv:])

    def chunk(c, carry):
        r0 = pl.multiple_of(c * C, C)
        rows = pl.ds(r0, C)
        for hh in range(H):
            qc = q_s[rows, hh * dk:(hh + 1) * dk]
            kc = k_s[rows, hh * dk:(hh + 1) * dk]
            vc = v_s[rows, hh * dv:(hh + 1) * dv]
            st = state_s[hh]
            scores = _dot_nt(qc, kc.astype(BF16)) * inner_ref[hh]
            inner = _dot(scores.astype(BF16), vc)
            cross = _dot(qc, st.astype(BF16)) * qdec_ref[hh]
            kd = (kc * kdec_ref[hh]).astype(BF16)
            state_s[hh] = st * cdec_ref[hh] + _dot_tn(kd, vc)
            out = inner + cross
            out = out * lax.rsqrt(jnp.mean(out * out, axis=-1, keepdims=True) + RMS_EPS)
            gt = gate_s[rows, hh * dv:(hh + 1) * dv]
            z_s[rows, hh * dv:(hh + 1) * dv] = (gt * jax.nn.sigmoid(gt) * out).astype(BF16)
        return carry

    lax.fori_loop(0, x.shape[0] // C, chunk, 0, unroll=True)
    o_ref[...] = x + _dot(z_s[...], wout_ref[...])


def _ret_tables():
    H, C = RET_HEADS, RET_CHUNK
    log_gamma = jnp.log(1.0 - 2.0 ** (-5.0 - jnp.arange(H, dtype=F32)))
    idx = jnp.arange(C, dtype=F32)
    diff = idx[:, None] - idx[None, :]
    inner = jnp.where(diff[None] >= 0,
                      jnp.exp(jnp.maximum(diff, 0.0)[None] * log_gamma[:, None, None]), 0.0)
    qdec = jnp.exp((idx[None, :] + 1.0) * log_gamma[:, None])[:, :, None]
    kdec = jnp.exp((C - 1.0 - idx[None, :]) * log_gamma[:, None])[:, :, None]
    cdec = jnp.exp(C * log_gamma)[:, None, None]
    return inner, qdec, kdec, cdec


def _rot_tables(s):
    pos = jnp.arange(s, dtype=F32)
    freq = 1.0 / (ROT_BASE ** jnp.linspace(0.0, 1.0, RET_DK // 2, dtype=F32))
    ang = pos[:, None] * freq[None, :]
    return jnp.cos(ang), jnp.sin(ang)


def _pair_split_columns(w, heads, dim):
    d = w.shape[0]
    w = w.reshape(d, heads, dim // 2, 2)
    return jnp.concatenate([w[..., 0], w[..., 1]], axis=-1).reshape(d, heads * dim)


def _retention(x, g, w_in, w_out):
    b, s, d = x.shape
    H, dk, dv = RET_HEADS, RET_DK, RET_DV
    tb = min(RET_ROWS, s)
    wq = _pair_split_columns(w_in[:, :H * dk], H, dk)
    wk = _pair_split_columns(w_in[:, H * dk:2 * H * dk], H, dk)
    win = jnp.concatenate([wq, wk, w_in[:, 2 * H * dk:]], axis=1).astype(BF16)
    wout = w_out.astype(BF16)
    cos, sin = _rot_tables(s)
    inner, qdec, kdec, cdec = _ret_tables()
    row = pl.BlockSpec((None, tb, d), lambda bi, i: (bi, i, 0))
    rot = pl.BlockSpec((tb, dk // 2), lambda bi, i: (i, 0))
    return pl.pallas_call(
        _ret_kernel,
        grid=(b, s // tb),
        in_specs=[row, _resident((1, d)), _resident(win.shape), _resident(wout.shape), rot, rot,
                  _resident(inner.shape), _resident(qdec.shape), _resident(kdec.shape),
                  _resident(cdec.shape)],
        out_specs=row,
        out_shape=jax.ShapeDtypeStruct((b, s, d), F32),
        scratch_shapes=[
            pltpu.VMEM((tb, H * dk), BF16),
            pltpu.VMEM((tb, H * dk), F32),
            pltpu.VMEM((tb, H * dv), BF16),
            pltpu.VMEM((tb, H * dv), F32),
            pltpu.VMEM((tb, H * dv), BF16),
            pltpu.VMEM((H, dk, dv), F32),
        ],
        compiler_params=_params(2),
        name="retention",
    )(x, g.reshape(1, d), win, wout, cos, sin, inner, qdec, kdec, cdec)


def _dsa_proj_kernel(x_ref, g_ref, win_ref, winw_ref, qn_ref, kvn_ref, kg_ref, kb_ref, wqi_ref,
                     wuq_ref, wuk_ref, qi_ref, wt_ref, ql_ref, kidx_ref, ckv_ref, ckvt_ref):
    RQ, RKV, HI, H, dh, TQ = DSA_Q_RANK, DSA_KV_RANK, IDX_HEADS, DSA_HEADS, DSA_HEAD_DIM, ATT_Q
    n_blk = x_ref.shape[0] // TQ
    h = _rms(x_ref[...], g_ref[...]).astype(BF16)
    proj = _dot(h, win_ref[...])
    c_q = _rms(proj[:, :RQ], qn_ref[...])
    c_kv = _rms(proj[:, RQ:RQ + RKV], kvn_ref[...])
    kx = proj[:, RQ + RKV:RQ + RKV + LANES]
    lane = lax.broadcasted_iota(jnp.int32, kx.shape, 1)
    mu = jnp.sum(kx, axis=-1, keepdims=True) * (1.0 / IDX_DIM)
    cen = jnp.where(lane < IDX_DIM, kx - mu, 0.0)
    var = jnp.sum(cen * cen, axis=-1, keepdims=True) * (1.0 / IDX_DIM)
    kidx_ref[...] = (cen * lax.rsqrt(var + RMS_EPS) * kg_ref[...] + kb_ref[...]).astype(BF16)
    ckv_ref[...] = c_kv.astype(BF16)
    ckvt_ref[0:RKV, :] = c_kv.T.astype(BF16)
    ones_row = lax.broadcasted_iota(jnp.int32, (BF16_ROWS, x_ref.shape[0]), 0) == 0
    ckvt_ref[RKV:, :] = jnp.where(ones_row, 1.0, 0.0).astype(BF16)
    w_t = _dot_nt(winw_ref[...], h) * (HI ** -0.5)
    cq = c_q.astype(BF16)
    qi_t = (_dot_nt(wqi_ref[...], cq) * (IDX_DIM ** -0.5)).astype(BF16)
    q = _dot(cq, wuq_ref[...]).astype(BF16)
    ql_t = [_dot_nt(wuk_ref[hh], q[:, hh * dh:(hh + 1) * dh]).astype(BF16) for hh in range(H)]
    for u in range(n_blk):
        cols = slice(u * TQ, (u + 1) * TQ)
        wt_ref[u] = w_t[0:HI, cols]
        for hh in range(HI):
            qi_ref[u, :, hh * TQ:(hh + 1) * TQ] = qi_t[hh * LANES:(hh + 1) * LANES, cols]
        for hh in range(H):
            ql_ref[u, :, hh * TQ:(hh + 1) * TQ] = ql_t[hh][:, cols]


def _dsa_proj(x2, g, w_in, q_norm, kv_norm, w_uq, w_qidx, kidx_g, kidx_b, w_uk):
    n, d = x2.shape
    RQ, RKV, HI, DI, H, TQ = DSA_Q_RANK, DSA_KV_RANK, IDX_HEADS, IDX_DIM, DSA_HEADS, ATT_Q
    pad_k = jnp.zeros((d, LANES - DI), F32)
    win = jnp.concatenate([w_in[:, :RQ + RKV + DI], pad_k], axis=1).astype(BF16)
    winw = jnp.pad(w_in[:, RQ + RKV + DI:].T, ((0, BF16_ROWS - HI), (0, 0))).astype(BF16)
    kg = jnp.pad(kidx_g, (0, LANES - DI)).reshape(1, LANES)
    kb = jnp.pad(kidx_b, (0, LANES - DI)).reshape(1, LANES)
    wqi = jnp.pad(w_qidx.reshape(RQ, HI, DI), ((0, 0), (0, 0), (0, LANES - DI)))
    wqi = wqi.reshape(RQ, HI * LANES).T.astype(BF16)
    wuq = w_uq.astype(BF16)
    wuk = w_uk.transpose(1, 0, 2).astype(BF16)
    tm = min(PROJ_ROWS, n)
    nb = tm // TQ
    row = lambda w: pl.BlockSpec((tm, w), lambda i: (i, 0))
    blk = lambda r, c: pl.BlockSpec((nb, r, c), lambda i: (i, 0, 0))
    return pl.pallas_call(
        _dsa_proj_kernel,
        grid=(n // tm,),
        in_specs=[row(d), _resident((1, d)), _resident(win.shape), _resident(winw.shape),
                  _resident((1, RQ)), _resident((1, RKV)), _resident((1, LANES)),
                  _resident((1, LANES)), _resident(wqi.shape), _resident(wuq.shape),
                  _resident(wuk.shape)],
        out_specs=[blk(LANES, HI * TQ), blk(HI, TQ), blk(RKV, H * TQ), row(LANES), row(RKV),
                   pl.BlockSpec((KV_ROWS, tm), lambda i: (0, i))],
        out_shape=[
            jax.ShapeDtypeStruct((n // TQ, LANES, HI * TQ), BF16),
            jax.ShapeDtypeStruct((n // TQ, HI, TQ), F32),
            jax.ShapeDtypeStruct((n // TQ, RKV, H * TQ), BF16),
            jax.ShapeDtypeStruct((n, LANES), BF16),
            jax.ShapeDtypeStruct((n, RKV), BF16),
            jax.ShapeDtypeStruct((KV_ROWS, n), BF16),
        ],
        compiler_params=_params(1),
        name="dsa_proj",
    )(x2, g.reshape(1, d), win, winw, q_norm.reshape(1, RQ), kv_norm.reshape(1, RKV), kg, kb, wqi,
      wuq, wuk)


def _fold(t, op):
    groups = t.shape[0] // SUBLANES
    group = lambda r: t[r * SUBLANES:(r + 1) * SUBLANES, :]
    acc = [group(w) for w in range(REDUCE_WAYS)]
    for r in range(REDUCE_WAYS, groups, REDUCE_WAYS):
        acc = [op(acc[w], group(r + w)) for w in range(REDUCE_WAYS)]
    while len(acc) > 1:
        acc = [op(acc[2 * w], acc[2 * w + 1]) for w in range(len(acc) // 2)]
    return acc[0]


def _fori_grouped(n, body, init, group=LOOP_GROUP):
    def grouped(q, carry):
        for u in range(group):
            carry = body(group * q + u, carry)
        return carry
    carry = lax.fori_loop(0, n // group, grouped, init)
    return lax.fori_loop(group * (n // group), n, body, carry)


def _dsa_attn_kernel(x_ref, qi_ref, wt_ref, ql_ref, kidx_ref, ckv_ref, ckvt_ref, wuvt_ref,
                     woutt_ref, cmax_ref, o_ref, sc_s, lg_a, lg_b, p_a, p_b, bias_s, acc_s, m_s, *, top_k):
    TQ, TK, TH = ATT_Q, ATT_K, ATT_K // 2
    HI, H, RKV = IDX_HEADS, DSA_HEADS, DSA_KV_RANK
    i = pl.program_id(1)
    n_tiles = (i * TQ + TQ + TK - 1) // TK
    qpos = i * TQ + lax.broadcasted_iota(jnp.int32, (1, TQ), 1)
    key0 = lax.broadcasted_iota(jnp.int32, (TK, 1), 0)
    keyh = lax.broadcasted_iota(jnp.int32, (TH, 1), 0)
    neg_inf = jnp.float32(-jnp.inf)
    kf = jnp.float32(top_k)

    def tile_rows(j):
        return pl.ds(pl.multiple_of(j * TK, TK), TK)

    w_t = wt_ref[...]
    last_half = 2 * n_tiles - 1

    def half_rows(jh):
        return pl.ds(pl.multiple_of(jh * TH, TH), TH)

    def score_dots(jh, buf):
        buf[...] = _dot(kidx_ref[half_rows(jh), :], qi_ref[...])

    def score_reduce(jh, buf, carry):
        rmax, rmin, c_ge0, c_gt0 = carry
        s = jnp.maximum(buf[:, 0:TQ], 0.0) * w_t[0:1, :]
        for hh in range(1, HI):
            s = s + jnp.maximum(buf[:, hh * TQ:(hh + 1) * TQ], 0.0) * w_t[hh:hh + 1, :]
        causal = keyh + jh * TH <= qpos
        s = jnp.where(causal, s, neg_inf)
        sc_s[half_rows(jh), :] = s
        rmax = jnp.maximum(rmax, _fold(s, jnp.maximum))
        rmin = jnp.minimum(rmin, _fold(jnp.where(causal, s, jnp.inf), jnp.minimum))
        c_ge0 = c_ge0 + _fold(jnp.where(s >= 0.0, 1.0, 0.0), jnp.add)
        c_gt0 = c_gt0 + _fold(jnp.where(s > 0.0, 1.0, 0.0), jnp.add)
        return rmax, rmin, c_ge0, c_gt0

    def score_pair(j, carry):
        score_dots(2 * j + 1, lg_b)
        carry = score_reduce(2 * j, lg_a, carry)
        score_dots(jnp.minimum(2 * j + 2, last_half), lg_a)
        return score_reduce(2 * j + 1, lg_b, carry)

    part = lambda v: jnp.full((SUBLANES, TQ), v, F32)
    score_dots(0, lg_a)
    rmax, rmin, c_ge0, c_gt0 = _fori_grouped(
        n_tiles, score_pair, (part(-jnp.inf), part(jnp.inf), part(0.0), part(0.0)))
    rmax = jnp.max(rmax, axis=0, keepdims=True)
    rmin = jnp.min(rmin, axis=0, keepdims=True)
    c_ge0 = jnp.sum(c_ge0, axis=0, keepdims=True)
    c_gt0 = jnp.sum(c_gt0, axis=0, keepdims=True)

    def count(pred):
        def body(j, acc):
            return acc + _fold(jnp.where(pred(sc_s[tile_rows(j), :], j), 1.0, 0.0), jnp.add)
        return jnp.sum(_fori_grouped(n_tiles, body, part(0.0)), axis=0, keepdims=True)

    n_valid = (qpos + 1).astype(F32)
    all_sel = n_valid <= kf
    zero_tie = (c_gt0 < kf) & (c_ge0 >= kf)
    positive = c_gt0 >= kf
    lo = jnp.where(positive | zero_tie, 0.0, rmin)
    hi = jnp.where(positive, 2.0 * rmax, 0.0)
    clo = jnp.where(positive | zero_tie, c_ge0, n_valid)
    chi = jnp.where(positive, 0.0, c_ge0)
    lo = jnp.where(all_sel, jnp.float32(jnp.finfo(jnp.float32).min), lo)
    clo = jnp.where(all_sel, n_valid, clo)
    done = all_sel | zero_tie | (clo == kf)
    one = jnp.ones((1, TQ), F32)
    log_target = math.log(top_k + 0.5)

    def pending(done):
        return jnp.sum(jnp.where(done, 0, 1))

    def search_cond(c):
        return (c[1] > 0) & (c[0] < SEARCH_CAP)

    def search_step(state):
        lo, hi, clo, chi, wl, wh, side, done_f = state
        done = done_f > 0.0
        mid = 0.5 * lo + 0.5 * hi
        fa = (jnp.log(clo) - log_target) * wl
        fb = (log_target - jnp.log(jnp.maximum(chi, 0.5))) * wh
        cand = lo + (hi - lo) * (fa / (fa + fb))
        cand = jnp.where((cand > lo) & (cand < hi), cand, mid)
        collapsed = (cand <= lo) | (cand >= hi)
        cm = count(lambda t, j: t >= cand)
        move = jnp.logical_not(done | collapsed)
        up = move & (cm >= kf)
        down = move & (cm < kf)
        wh = jnp.where(up, jnp.where(side > 0.0, 0.5 * wh, 1.0), jnp.where(down, 1.0, wh))
        wl = jnp.where(down, jnp.where(side < 0.0, 0.5 * wl, 1.0), jnp.where(up, 1.0, wl))
        side = jnp.where(up, 1.0, jnp.where(down, -1.0, side))
        lo = jnp.where(up, cand, lo)
        clo = jnp.where(up, cm, clo)
        hi = jnp.where(down, cand, hi)
        chi = jnp.where(down, cm, chi)
        done = done | collapsed | (clo == kf)
        return lo, hi, clo, chi, wl, wh, side, jnp.where(done, 1.0, 0.0)

    def search_body(c):
        state = search_step(c[2:])
        return (c[0] + 1, pending(state[-1] > 0.0)) + state

    state = (lo, hi, clo, chi, one, one, 0.0 * one, jnp.where(done, 1.0, 0.0))
    state = lax.fori_loop(0, SEARCH_UNCHECKED, lambda _, s: search_step(s), state)

    lo, hi, clo, chi, wl, wh, side, done_f = state
    inf = jnp.float32(jnp.inf)

    def insert(ranks, v):
        out = []
        for r in ranks:
            out.append(jnp.minimum(r, v))
            v = jnp.maximum(r, v)
        return out

    def smallest_body(j, ranks):
        t = sc_s[tile_rows(j), :]
        ranks = list(ranks)
        for g in range(TK // SUBLANES):
            v = t[g * SUBLANES:(g + 1) * SUBLANES, :]
            w = g % 2
            ranks[w] = tuple(insert(ranks[w], jnp.where(v >= lo, v, inf)))
        return tuple(ranks)

    empty = tuple(jnp.full((SUBLANES, TQ), inf, F32) for _ in range(FINISH_RANKS))
    ranks = lax.fori_loop(0, n_tiles, smallest_body, (empty, empty))
    final = [jnp.full((1, TQ), inf, F32) for _ in range(FINISH_RANKS)]
    for chain in ranks:
        for r in chain:
            for sub in range(SUBLANES):
                final = insert(final, r[sub:sub + 1, :])
    extra = clo - kf
    kth = final[0]
    below = -inf
    for e in range(1, FINISH_RANKS):
        kth = jnp.where(extra >= e, final[e], kth)
        below = jnp.where(extra == e, final[e - 1], below)
    near = (done_f <= 0.0) & (extra < FINISH_RANKS)
    lo = jnp.where(near, kth, lo)
    clo = jnp.where(near, jnp.where(below == kth, kf + 1.0, kf), clo)
    done_f = jnp.where(near, 1.0, done_f)
    state = (lo, hi, clo, chi, wl, wh, side, done_f)

    res = lax.while_loop(search_cond, search_body,
                         (jnp.int32(0), pending(state[-1] > 0.0)) + state)
    tau, clo = res[2], res[4]

    excess = clo > kf

    @pl.when(jnp.sum(jnp.where(excess, 1, 0)) > 0)
    def _():
        need = kf - count(lambda t, j: t > tau)
        n_steps = max(1, math.ceil(math.log2(sc_s.shape[0]))) + 1

        def step(_, jb):
            j_lo, j_hi = jb
            j_mid = (j_lo + j_hi) >> 1
            c = count(lambda t, j: (t == tau) & (key0 + j * TK <= j_mid))
            ok = c >= need
            return jnp.where(ok, j_lo, j_mid), jnp.where(ok, j_mid, j_hi)

        _, j_cut = lax.fori_loop(0, n_steps, step, (jnp.full((1, TQ), -1, jnp.int32), qpos))

        def fix(j, carry):
            t = sc_s[tile_rows(j), :]
            drop = excess & (t == tau) & (key0 + j * TK > j_cut)
            sc_s[tile_rows(j), :] = jnp.where(drop, neg_inf, t)
            return carry

        lax.fori_loop(0, n_tiles, fix, 0)

    c_exp = (DSA_HEAD_DIM ** -0.5) * math.log2(math.e)
    def attn_logits(jh, buf):
        buf[...] = _dot(ckv_ref[half_rows(jh), :], ql_ref[...])

    ql_f = ql_ref[...].astype(F32)
    bound = jnp.sqrt(jnp.sum(ql_f * ql_f, axis=0, keepdims=True)) * cmax_ref[...]

    def fast_values(jh, p_buf):
        cols_j = pl.ds(pl.multiple_of(jh * TH, TH), TH)
        acc_s[...] += _dot(ckvt_ref[:, cols_j], p_buf[...])

    def fast_half(jh, buf, p_buf, jh_next, buf_next, jh_prev, p_prev):
        keep = jnp.where(sc_s[half_rows(jh), :] >= tau, 1.0, 0.0).astype(BF16)
        keys_next = ckv_ref[half_rows(jh_next), :]
        vals_prev = ckvt_ref[:, pl.ds(pl.multiple_of(jh_prev * TH, TH), TH)]
        per_group = H // COLUMN_GROUPS
        for grp in range(COLUMN_GROUPS):
            gcols = slice(grp * per_group * TQ, (grp + 1) * per_group * TQ)
            buf_next[:, gcols] = _dot(keys_next, ql_ref[:, gcols])
            for hh in range(grp * per_group, (grp + 1) * per_group):
                cols = slice(hh * TQ, (hh + 1) * TQ)
                p = jnp.exp2((buf[:, cols] - bound[:, cols]) * c_exp).astype(BF16)
                p_buf[:, cols] = p * keep
            acc_s[:, gcols] += _dot(vals_prev, p_prev[:, gcols])

    def fast_pair(j, carry):
        fast_half(2 * j, lg_a, p_a, 2 * j + 1, lg_b, jnp.maximum(2 * j - 1, 0), p_b)
        fast_half(2 * j + 1, lg_b, p_b, jnp.minimum(2 * j + 2, last_half), lg_a, 2 * j, p_a)
        return carry

    acc_s[...] = jnp.zeros_like(acc_s)
    p_b[...] = jnp.zeros_like(p_b)
    attn_logits(0, lg_a)
    _fori_grouped(n_tiles, fast_pair, 0)
    fast_values(last_half, p_b)
    denom_ok = acc_s[RKV:RKV + 1, :] >= DENOM_FLOOR
    underflowed = jnp.sum(jnp.where(denom_ok, 0, 1)) > 0

    def attn_softmax(jh, buf, p_buf):
        bias_s[...] = jnp.where(sc_s[half_rows(jh), :] >= tau, 0.0, MASKED)
        alphas = []
        for hh in range(H):
            cols = slice(hh * TQ, (hh + 1) * TQ)
            m_old = m_s[hh:hh + 1, :]
            mx = _fold(buf[:, cols] + bias_s[...], jnp.maximum)
            m_new = jnp.maximum(m_old, jnp.max(mx, axis=0, keepdims=True))
            m_s[hh:hh + 1, :] = m_new
            alphas.append(jnp.exp2((m_old - m_new) * c_exp))
            p_buf[:, cols] = jnp.exp2((buf[:, cols] + bias_s[...] - m_new) * c_exp).astype(BF16)
        return jnp.concatenate(alphas, axis=1)

    def attn_values(jh, p_buf, alpha):
        cols_j = pl.ds(pl.multiple_of(jh * TH, TH), TH)
        acc_s[...] = acc_s[...] * alpha + _dot(ckvt_ref[:, cols_j], p_buf[...])

    def attn_pair(j, alpha_b):
        attn_logits(2 * j + 1, lg_b)
        alpha_a = attn_softmax(2 * j, lg_a, p_a)
        attn_values(jnp.maximum(2 * j - 1, 0), p_b, alpha_b)
        attn_logits(jnp.minimum(2 * j + 2, last_half), lg_a)
        alpha_b = attn_softmax(2 * j + 1, lg_b, p_b)
        attn_values(2 * j, p_a, alpha_a)
        return alpha_b

    @pl.when(underflowed)
    def _():
        m_s[...] = jnp.full_like(m_s, MASKED)
        acc_s[...] = jnp.zeros_like(acc_s)
        p_b[...] = jnp.zeros_like(p_b)
        attn_logits(0, lg_a)
        alpha_b = lax.fori_loop(0, n_tiles, attn_pair, jnp.ones((1, H * TQ), F32))
        attn_values(last_half, p_b, alpha_b)

    o_lat = (acc_s[0:RKV, :] / acc_s[RKV:RKV + 1, :]).astype(BF16)
    o_t = jnp.concatenate(
        [_dot(wuvt_ref[hh], o_lat[:, hh * TQ:(hh + 1) * TQ]) for hh in range(H)], axis=0)
    out_t = _dot(woutt_ref[...], o_t.astype(BF16))
    o_ref[...] = x_ref[...] + out_t.T


def _dsa_attn(x, qi, wt, ql, kidx, ckv, ckvt, kv_norm, w_uv, w_out):
    b, s, d = x.shape
    H, RKV, HI, TQ = DSA_HEADS, DSA_KV_RANK, IDX_HEADS, ATT_Q
    nq = s // TQ
    top_k = min(TOPK_MAX, s // 4)
    wuvt = w_uv.transpose(1, 2, 0).astype(BF16)
    woutt = w_out.T.astype(BF16)
    cmax = (math.sqrt(RKV) * jnp.max(jnp.abs(kv_norm))).reshape(1, 1).astype(F32)
    qrow = pl.BlockSpec((None, TQ, d), lambda bi, i: (bi, i, 0))
    qblk = lambda r, c: pl.BlockSpec((None, r, c), lambda bi, i: (bi * nq + i, 0, 0))
    seq = lambda width: pl.BlockSpec((s, width), lambda bi, i: (bi, 0))
    return pl.pallas_call(
        functools.partial(_dsa_attn_kernel, top_k=top_k),
        grid=(b, nq),
        in_specs=[qrow, qblk(LANES, HI * TQ), qblk(HI, TQ), qblk(RKV, H * TQ), seq(LANES), seq(RKV),
                  pl.BlockSpec((KV_ROWS, s), lambda bi, i: (0, bi)),
                  _resident(wuvt.shape), _resident(woutt.shape), _resident((1, 1))],
        out_specs=qrow,
        out_shape=jax.ShapeDtypeStruct((b, s, d), F32),
        scratch_shapes=[
            pltpu.VMEM((s, TQ), F32),
            pltpu.VMEM((ATT_K // 2, H * TQ), F32),
            pltpu.VMEM((ATT_K // 2, H * TQ), F32),
            pltpu.VMEM((ATT_K // 2, H * TQ), BF16),
            pltpu.VMEM((ATT_K // 2, H * TQ), BF16),
            pltpu.VMEM((ATT_K // 2, TQ), F32),
            pltpu.VMEM((KV_ROWS, H * TQ), F32),
            pltpu.VMEM((H, TQ), F32),
        ],
        compiler_params=_params(2),
        name="dsa_attn",
    )(x, qi, wt, ql, kidx, ckv, ckvt, wuvt, woutt, cmax)


def _dsa(x, g, w_in, q_norm, kv_norm, w_uq, w_qidx, kidx_g, kidx_b, w_uk, w_uv, w_out):
    b, s, d = x.shape
    qi, wt, ql, kidx, ckv, ckvt = _dsa_proj(x.reshape(b * s, d), g, w_in, q_norm, kv_norm, w_uq,
                                            w_qidx, kidx_g, kidx_b, w_uk)
    return _dsa_attn(x, qi, wt, ql, kidx, ckv, ckvt, kv_norm, w_uv, w_out)


def kernel(x, norm_ffn1, w_ffn1_in, w_ffn1_out, norm_mix, norm_ffn2, w_ffn2_in, w_ffn2_out, ret_w_in,
           ret_w_out, dsa_w_in, dsa_q_norm, dsa_kv_norm, dsa_w_uq, dsa_w_qidx, dsa_kidx_g, dsa_kidx_b,
           dsa_w_uk, dsa_w_uv, dsa_w_out, final_norm):
    b, s, d = x.shape
    depth = norm_ffn1.shape[0]
    for layer in range(depth):
        x = _ffn(x.reshape(b * s, d), norm_ffn1[layer], w_ffn1_in[layer], w_ffn1_out[layer])
        x = x.reshape(b, s, d)
        j = layer // 2
        if layer % 2 == 0:
            x = _retention(x, norm_mix[layer], ret_w_in[j], ret_w_out[j])
        else:
            x = _dsa(x, norm_mix[layer], dsa_w_in[j], dsa_q_norm[j], dsa_kv_norm[j], dsa_w_uq[j],
                     dsa_w_qidx[j], dsa_kidx_g[j], dsa_kidx_b[j], dsa_w_uk[j], dsa_w_uv[j],
                     dsa_w_out[j])
        last = layer == depth - 1
        x = _ffn(x.reshape(b * s, d), norm_ffn2[layer], w_ffn2_in[layer], w_ffn2_out[layer],
                 final_g=final_norm if last else None)
        x = x.reshape(b, s, d)
    return x
```

```python
import functools
import math

import jax
import jax.numpy as jnp
import numpy as np
from jax import lax
from jax.experimental import pallas as pl
from jax.experimental.pallas import tpu as pltpu

F32 = jnp.float32
BF16 = jnp.bfloat16

D_MODEL = 1024
DEPTH = 4
D_FF = 2816
RMS_EPS = 1e-6

RET_HEADS = 4
RET_DK = D_MODEL // RET_HEADS
RET_DV = 2 * RET_DK
RET_CHUNK = 128
ROT_BASE = 10000.0

DSA_HEADS = 8
DSA_HEAD_DIM = D_MODEL // DSA_HEADS
DSA_Q_RANK = 256
DSA_KV_RANK = 256
IDX_HEADS = 8
IDX_DIM = 64
TOPK_MAX = 256

LANES = 128
SUBLANES = 8
BF16_ROWS = 16
VMEM_LIMIT_BYTES = 56 * 1024 * 1024

FFN_ROWS = 512
FFN_CHUNK = 256
RET_ROWS = 512
PROJ_ROWS = 512
ATT_Q = 2 * LANES
ATT_K = 512
KV_ROWS = DSA_KV_RANK + BF16_ROWS
MASKED = -1e30
SEARCH_CAP = 400
SEARCH_UNCHECKED = 9
FINISH_RANKS = 4
DENOM_FLOOR = 2.0 ** -80
REDUCE_WAYS = 4
LOOP_GROUP = 4
COLUMN_GROUPS = 4


def _resident(shape):
    nd = len(shape)
    return pl.BlockSpec(shape, lambda *_: (0,) * nd, pipeline_mode=pl.Buffered(1))


def _params(n_grid, flags=None):
    return pltpu.CompilerParams(
        dimension_semantics=("arbitrary",) * n_grid, vmem_limit_bytes=VMEM_LIMIT_BYTES, flags=flags)


def _rms(x, g):
    y = x * lax.rsqrt(jnp.mean(x * x, axis=-1, keepdims=True) + RMS_EPS)
    return y * g


def _dot(a, b):
    return jnp.dot(a, b, preferred_element_type=F32)


def _dot_nt(a, b):
    return lax.dot_general(a, b, (((1,), (1,)), ((), ())), preferred_element_type=F32)


def _dot_tn(a, b):
    return lax.dot_general(a, b, (((0,), (0,)), ((), ())), preferred_element_type=F32)


def _ffn_kernel(x_ref, g_ref, win_ref, wo_ref, *rest, final):
    if final:
        fg_ref, o_ref, acc_ref = rest
    else:
        o_ref, acc_ref = rest
    x = x_ref[...]
    h = _rms(x, g_ref[...]).astype(BF16)
    acc_ref[...] = jnp.zeros_like(acc_ref)
    for c in range(D_FF // FFN_CHUNK):
        cols = slice(c * FFN_CHUNK, (c + 1) * FFN_CHUNK)
        gate = _dot(h, win_ref[:, cols])
        up = _dot(h, win_ref[:, D_FF + c * FFN_CHUNK:D_FF + (c + 1) * FFN_CHUNK])
        a = (gate * jax.nn.sigmoid(gate) * up).astype(BF16)
        acc_ref[...] += _dot(a, wo_ref[cols, :])
    y = x + 0.5 * acc_ref[...]
    if final:
        y = _rms(y, fg_ref[...])
    o_ref[...] = y


def _ffn(x2, g, w_in, w_out, final_g=None):
    n, d = x2.shape
    win = w_in.astype(BF16)
    wo = w_out.astype(BF16)
    row = pl.BlockSpec((FFN_ROWS, d), lambda i: (i, 0))
    ins = [x2, g.reshape(1, d), win, wo]
    specs = [row, _resident((1, d)), _resident(win.shape), _resident(wo.shape)]
    if final_g is not None:
        ins.append(final_g.reshape(1, d))
        specs.append(_resident((1, d)))
    return pl.pallas_call(
        functools.partial(_ffn_kernel, final=final_g is not None),
        grid=(n // FFN_ROWS,),
        in_specs=specs,
        out_specs=row,
        out_shape=jax.ShapeDtypeStruct((n, d), F32),
        scratch_shapes=[pltpu.VMEM((FFN_ROWS, d), F32)],
        compiler_params=_params(1),
        name="ffn_final" if final_g is not None else "ffn",
    )(*ins)


def _ret_kernel(x_ref, g_ref, win_ref, wout_ref, cos_ref, sin_ref, inner_ref, qdec_ref, kdec_ref,
                cdec_ref, o_ref, q_s, k_s, v_s, gate_s, z_s, state_s):
    H, dk, dv, C = RET_HEADS, RET_DK, RET_DV, RET_CHUNK
    half = dk // 2

    @pl.when(pl.program_id(1) == 0)
    def _():
        state_s[...] = jnp.zeros_like(state_s)

    x = x_ref[...]
    h = _rms(x, g_ref[...]).astype(BF16)
    cos = cos_ref[...]
    sin = sin_ref[...]
    q = _dot(h, win_ref[:, 0:H * dk])
    k = _dot(h, win_ref[:, H * dk:2 * H * dk])
    for hh in range(H):
        q0 = q[:, hh * dk:hh * dk + half]
        q1 = q[:, hh * dk + half:(hh + 1) * dk]
        q_s[:, hh * dk:hh * dk + half] = (q0 * cos - q1 * sin).astype(BF16)
        q_s[:, hh * dk + half:(hh + 1) * dk] = (q1 * cos + q0 * sin).astype(BF16)
        k0 = k[:, hh * dk:hh * dk + half]
        k1 = k[:, hh * dk + half:(hh + 1) * dk]
        k_s[:, hh * dk:hh * dk + half] = (k0 * cos - k1 * sin) * (dk ** -0.5)
        k_s[:, hh * dk + half:(hh + 1) * dk] = (k1 * cos + k0 * sin) * (dk ** -0.5)
    v_s[...] = _dot(h, win_ref[:, 2 * H * dk:2 * H * dk + H * dv]).astype(BF16)
    gate_s[...] = _dot(h, win_ref[:, 2 * H * dk + H * dv:])

    def chunk(c, carry):
        r0 = pl.multiple_of(c * C, C)
        rows = pl.ds(r0, C)
        for hh in range(H):
            qc = q_s[rows, hh * dk:(hh + 1) * dk]
            kc = k_s[rows, hh * dk:(hh + 1) * dk]
            vc = v_s[rows, hh * dv:(hh + 1) * dv]
            st = state_s[hh]
            scores = _dot_nt(qc, kc.astype(BF16)) * inner_ref[hh]
            inner = _dot(scores.astype(BF16), vc)
            cross = _dot(qc, st.astype(BF16)) * qdec_ref[hh]
            kd = (kc * kdec_ref[hh]).astype(BF16)
            state_s[hh] = st * cdec_ref[hh] + _dot_tn(kd, vc)
            out = inner + cross
            out = out * lax.rsqrt(jnp.mean(out * out, axis=-1, keepdims=True) + RMS_EPS)
            gt = gate_s[rows, hh * dv:(hh + 1) * dv]
            z_s[rows, hh * dv:(hh + 1) * dv] = (gt * jax.nn.sigmoid(gt) * out).astype(BF16)
        return carry

    lax.fori_loop(0, x.shape[0] // C, chunk, 0, unroll=True)
    o_ref[...] = x + _dot(z_s[...], wout_ref[...])


def _ret_tables():
    H, C = RET_HEADS, RET_CHUNK
    log_gamma = jnp.log(1.0 - 2.0 ** (-5.0 - jnp.arange(H, dtype=F32)))
    idx = jnp.arange(C, dtype=F32)
    diff = idx[:, None] - idx[None, :]
    inner = jnp.where(diff[None] >= 0,
                      jnp.exp(jnp.maximum(diff, 0.0)[None] * log_gamma[:, None, None]), 0.0)
    qdec = jnp.exp((idx[None, :] + 1.0) * log_gamma[:, None])[:, :, None]
    kdec = jnp.exp((C - 1.0 - idx[None, :]) * log_gamma[:, None])[:, :, None]
    cdec = jnp.exp(C * log_gamma)[:, None, None]
    return inner, qdec, kdec, cdec


def _rot_tables(s):
    pos = jnp.arange(s, dtype=F32)
    freq = 1.0 / (ROT_BASE ** jnp.linspace(0.0, 1.0, RET_DK // 2, dtype=F32))
    ang = pos[:, None] * freq[None, :]
    return jnp.cos(ang), jnp.sin(ang)


def _pair_split_columns(w, heads, dim):
    d = w.shape[0]
    w = w.reshape(d, heads, dim // 2, 2)
    return jnp.concatenate([w[..., 0], w[..., 1]], axis=-1).reshape(d, heads * dim)


def _retention(x, g, w_in, w_out):
    b, s, d = x.shape
    H, dk, dv = RET_HEADS, RET_DK, RET_DV
    tb = min(RET_ROWS, s)
    wq = _pair_split_columns(w_in[:, :H * dk], H, dk)
    wk = _pair_split_columns(w_in[:, H * dk:2 * H * dk], H, dk)
    win = jnp.concatenate([wq, wk, w_in[:, 2 * H * dk:]], axis=1).astype(BF16)
    wout = w_out.astype(BF16)
    cos, sin = _rot_tables(s)
    inner, qdec, kdec, cdec = _ret_tables()
    row = pl.BlockSpec((None, tb, d), lambda bi, i: (bi, i, 0))
    rot = pl.BlockSpec((tb, dk // 2), lambda bi, i: (i, 0))
    return pl.pallas_call(
        _ret_kernel,
        grid=(b, s // tb),
        in_specs=[row, _resident((1, d)), _resident(win.shape), _resident(wout.shape), rot, rot,
                  _resident(inner.shape), _resident(qdec.shape), _resident(kdec.shape),
                  _resident(cdec.shape)],
        out_specs=row,
        out_shape=jax.ShapeDtypeStruct((b, s, d), F32),
        scratch_shapes=[
            pltpu.VMEM((tb, H * dk), BF16),
            pltpu.VMEM((tb, H * dk), F32),
            pltpu.VMEM((tb, H * dv), BF16),
            pltpu.VMEM((tb, H * dv), F32),
            pltpu.VMEM((tb, H * dv), BF16),
            pltpu.VMEM((H, dk, dv), F32),
        ],
        compiler_params=_params(2),
        name="retention",
    )(x, g.reshape(1, d), win, wout, cos, sin, inner, qdec, kdec, cdec)


def _dsa_proj_kernel(x_ref, g_ref, win_ref, winw_ref, qn_ref, kvn_ref, kg_ref, kb_ref, wqi_ref,
                     wuq_ref, wuk_ref, qi_ref, wt_ref, ql_ref, kidx_ref, ckv_ref, ckvt_ref):
    RQ, RKV, HI, H, dh, TQ = DSA_Q_RANK, DSA_KV_RANK, IDX_HEADS, DSA_HEADS, DSA_HEAD_DIM, ATT_Q
    n_blk = x_ref.shape[0] // TQ
    h = _rms(x_ref[...], g_ref[...]).astype(BF16)
    proj = _dot(h, win_ref[...])
    c_q = _rms(proj[:, :RQ], qn_ref[...])
    c_kv = _rms(proj[:, RQ:RQ + RKV], kvn_ref[...])
    kx = proj[:, RQ + RKV:RQ + RKV + LANES]
    lane = lax.broadcasted_iota(jnp.int32, kx.shape, 1)
    mu = jnp.sum(kx, axis=-1, keepdims=True) * (1.0 / IDX_DIM)
    cen = jnp.where(lane < IDX_DIM, kx - mu, 0.0)
    var = jnp.sum(cen * cen, axis=-1, keepdims=True) * (1.0 / IDX_DIM)
    kidx_ref[...] = (cen * lax.rsqrt(var + RMS_EPS) * kg_ref[...] + kb_ref[...]).astype(BF16)
    ckv_ref[...] = c_kv.astype(BF16)
    ckvt_ref[0:RKV, :] = c_kv.T.astype(BF16)
    ones_row = lax.broadcasted_iota(jnp.int32, (BF16_ROWS, x_ref.shape[0]), 0) == 0
    ckvt_ref[RKV:, :] = jnp.where(ones_row, 1.0, 0.0).astype(BF16)
    w_t = _dot_nt(winw_ref[...], h) * (HI ** -0.5)
    cq = c_q.astype(BF16)
    qi_t = (_dot_nt(wqi_ref[...], cq) * (IDX_DIM ** -0.5)).astype(BF16)
    q = _dot(cq, wuq_ref[...]).astype(BF16)
    ql_t = [_dot_nt(wuk_ref[hh], q[:, hh * dh:(hh + 1) * dh]).astype(BF16) for hh in range(H)]
    for u in range(n_blk):
        cols = slice(u * TQ, (u + 1) * TQ)
        wt_ref[u] = w_t[0:HI, cols]
        for hh in range(HI):
            qi_ref[u, :, hh * TQ:(hh + 1) * TQ] = qi_t[hh * LANES:(hh + 1) * LANES, cols]
        for hh in range(H):
            ql_ref[u, :, hh * TQ:(hh + 1) * TQ] = ql_t[hh][:, cols]


def _dsa_proj(x2, g, w_in, q_norm, kv_norm, w_uq, w_qidx, kidx_g, kidx_b, w_uk):
    n, d = x2.shape
    RQ, RKV, HI, DI, H, TQ = DSA_Q_RANK, DSA_KV_RANK, IDX_HEADS, IDX_DIM, DSA_HEADS, ATT_Q
    pad_k = jnp.zeros((d, LANES - DI), F32)
    win = jnp.concatenate([w_in[:, :RQ + RKV + DI], pad_k], axis=1).astype(BF16)
    winw = jnp.pad(w_in[:, RQ + RKV + DI:].T, ((0, BF16_ROWS - HI), (0, 0))).astype(BF16)
    kg = jnp.pad(kidx_g, (0, LANES - DI)).reshape(1, LANES)
    kb = jnp.pad(kidx_b, (0, LANES - DI)).reshape(1, LANES)
    wqi = jnp.pad(w_qidx.reshape(RQ, HI, DI), ((0, 0), (0, 0), (0, LANES - DI)))
    wqi = wqi.reshape(RQ, HI * LANES).T.astype(BF16)
    wuq = w_uq.astype(BF16)
    wuk = w_uk.transpose(1, 0, 2).astype(BF16)
    tm = min(PROJ_ROWS, n)
    nb = tm // TQ
    row = lambda w: pl.BlockSpec((tm, w), lambda i: (i, 0))
    blk = lambda r, c: pl.BlockSpec((nb, r, c), lambda i: (i, 0, 0))
    return pl.pallas_call(
        _dsa_proj_kernel,
        grid=(n // tm,),
        in_specs=[row(d), _resident((1, d)), _resident(win.shape), _resident(winw.shape),
                  _resident((1, RQ)), _resident((1, RKV)), _resident((1, LANES)),
                  _resident((1, LANES)), _resident(wqi.shape), _resident(wuq.shape),
                  _resident(wuk.shape)],
        out_specs=[blk(LANES, HI * TQ), blk(HI, TQ), blk(RKV, H * TQ), row(LANES), row(RKV),
                   pl.BlockSpec((KV_ROWS, tm), lambda i: (0, i))],
        out_shape=[
            jax.ShapeDtypeStruct((n // TQ, LANES, HI * TQ), BF16),
            jax.ShapeDtypeStruct((n // TQ, HI, TQ), F32),
            jax.ShapeDtypeStruct((n // TQ, RKV, H * TQ), BF16),
            jax.ShapeDtypeStruct((n, LANES), BF16),
            jax.ShapeDtypeStruct((n, RKV), BF16),
            jax.ShapeDtypeStruct((KV_ROWS, n), BF16),
        ],
        compiler_params=_params(1),
        name="dsa_proj",
    )(x2, g.reshape(1, d), win, winw, q_norm.reshape(1, RQ), kv_norm.reshape(1, RKV), kg, kb, wqi,
      wuq, wuk)


def _fold(t, op):
    groups = t.shape[0] // SUBLANES
    group = lambda r: t[r * SUBLANES:(r + 1) * SUBLANES, :]
    acc = [group(w) for w in range(REDUCE_WAYS)]
    for r in range(REDUCE_WAYS, groups, REDUCE_WAYS):
        acc = [op(acc[w], group(r + w)) for w in range(REDUCE_WAYS)]
    while len(acc) > 1:
        acc = [op(acc[2 * w], acc[2 * w + 1]) for w in range(len(acc) // 2)]
    return acc[0]


def _fori_grouped(n, body, init, group=LOOP_GROUP):
    def grouped(q, carry):
        for u in range(group):
            carry = body(group * q + u, carry)
        return carry
    carry = lax.fori_loop(0, n // group, grouped, init)
    return lax.fori_loop(group * (n // group), n, body, carry)


def _dsa_attn_kernel(x_ref, qi_ref, wt_ref, ql_ref, kidx_ref, ckv_ref, ckvt_ref, wuvt_ref,
                     woutt_ref, cmax_ref, o_ref, sc_s, lg_a, lg_b, p_a, p_b, bias_s, acc_s, m_s, *, top_k):
    TQ, TK, TH = ATT_Q, ATT_K, ATT_K // 2
    HI, H, RKV = IDX_HEADS, DSA_HEADS, DSA_KV_RANK
    i = pl.program_id(1)
    n_tiles = (i * TQ + TQ + TK - 1) // TK
    qpos = i * TQ + lax.broadcasted_iota(jnp.int32, (1, TQ), 1)
    key0 = lax.broadcasted_iota(jnp.int32, (TK, 1), 0)
    keyh = lax.broadcasted_iota(jnp.int32, (TH, 1), 0)
    neg_inf = jnp.float32(-jnp.inf)
    kf = jnp.float32(top_k)

    def tile_rows(j):
        return pl.ds(pl.multiple_of(j * TK, TK), TK)

    w_t = wt_ref[...]
    last_half = 2 * n_tiles - 1

    def half_rows(jh):
        return pl.ds(pl.multiple_of(jh * TH, TH), TH)

    def score_dots(jh, buf):
        buf[...] = _dot(kidx_ref[half_rows(jh), :], qi_ref[...])

    def score_reduce(jh, buf, carry):
        rmax, rmin, c_ge0, c_gt0 = carry
        s = jnp.maximum(buf[:, 0:TQ], 0.0) * w_t[0:1, :]
        for hh in range(1, HI):
            s = s + jnp.maximum(buf[:, hh * TQ:(hh + 1) * TQ], 0.0) * w_t[hh:hh + 1, :]
        causal = keyh + jh * TH <= qpos
        s = jnp.where(causal, s, neg_inf)
        sc_s[half_rows(jh), :] = s
        rmax = jnp.maximum(rmax, _fold(s, jnp.maximum))
        rmin = jnp.minimum(rmin, _fold(jnp.where(causal, s, jnp.inf), jnp.minimum))
        c_ge0 = c_ge0 + _fold(jnp.where(s >= 0.0, 1.0, 0.0), jnp.add)
        c_gt0 = c_gt0 + _fold(jnp.where(s > 0.0, 1.0, 0.0), jnp.add)
        return rmax, rmin, c_ge0, c_gt0

    def score_pair(j, carry):
        score_dots(2 * j + 1, lg_b)
        carry = score_reduce(2 * j, lg_a, carry)
        score_dots(jnp.minimum(2 * j + 2, last_half), lg_a)
        return score_reduce(2 * j + 1, lg_b, carry)

    part = lambda v: jnp.full((SUBLANES, TQ), v, F32)
    score_dots(0, lg_a)
    rmax, rmin, c_ge0, c_gt0 = _fori_grouped(
        n_tiles, score_pair, (part(-jnp.inf), part(jnp.inf), part(0.0), part(0.0)))
    rmax = jnp.max(rmax, axis=0, keepdims=True)
    rmin = jnp.min(rmin, axis=0, keepdims=True)
    c_ge0 = jnp.sum(c_ge0, axis=0, keepdims=True)
    c_gt0 = jnp.sum(c_gt0, axis=0, keepdims=True)

    def count(pred):
        def body(j, acc):
            return acc + _fold(jnp.where(pred(sc_s[tile_rows(j), :], j), 1.0, 0.0), jnp.add)
        return jnp.sum(_fori_grouped(n_tiles, body, part(0.0)), axis=0, keepdims=True)

    n_valid = (qpos + 1).astype(F32)
    all_sel = n_valid <= kf
    zero_tie = (c_gt0 < kf) & (c_ge0 >= kf)
    positive = c_gt0 >= kf
    lo = jnp.where(positive | zero_tie, 0.0, rmin)
    hi = jnp.where(positive, 2.0 * rmax, 0.0)
    clo = jnp.where(positive | zero_tie, c_ge0, n_valid)
    chi = jnp.where(positive, 0.0, c_ge0)
    lo = jnp.where(all_sel, jnp.float32(jnp.finfo(jnp.float32).min), lo)
    clo = jnp.where(all_sel, n_valid, clo)
    done = all_sel | zero_tie | (clo == kf)
    one = jnp.ones((1, TQ), F32)
    log_target = math.log(top_k + 0.5)

    def pending(done):
        return jnp.sum(jnp.where(done, 0, 1))

    def search_cond(c):
        return (c[1] > 0) & (c[0] < SEARCH_CAP)

    def search_step(state):
        lo, hi, clo, chi, wl, wh, side, done_f = state
        done = done_f > 0.0
        mid = 0.5 * lo + 0.5 * hi
        fa = (jnp.log(clo) - log_target) * wl
        fb = (log_target - jnp.log(jnp.maximum(chi, 0.5))) * wh
        cand = lo + (hi - lo) * (fa / (fa + fb))
        cand = jnp.where((cand > lo) & (cand < hi), cand, mid)
        collapsed = (cand <= lo) | (cand >= hi)
        cm = count(lambda t, j: t >= cand)
        move = jnp.logical_not(done | collapsed)
        up = move & (cm >= kf)
        down = move & (cm < kf)
        wh = jnp.where(up, jnp.where(side > 0.0, 0.5 * wh, 1.0), jnp.where(down, 1.0, wh))
        wl = jnp.where(down, jnp.where(side < 0.0, 0.5 * wl, 1.0), jnp.where(up, 1.0, wl))
        side = jnp.where(up, 1.0, jnp.where(down, -1.0, side))
        lo = jnp.where(up, cand, lo)
        clo = jnp.where(up, cm, clo)
        hi = jnp.where(down, cand, hi)
        chi = jnp.where(down, cm, chi)
        done = done | collapsed | (clo == kf)
        return lo, hi, clo, chi, wl, wh, side, jnp.where(done, 1.0, 0.0)

    def search_body(c):
        state = search_step(c[2:])
        return (c[0] + 1, pending(state[-1] > 0.0)) + state

    state = (lo, hi, clo, chi, one, one, 0.0 * one, jnp.where(done, 1.0, 0.0))
    state = lax.fori_loop(0, SEARCH_UNCHECKED, lambda _, s: search_step(s), state)

    lo, hi, clo, chi, wl, wh, side, done_f = state
    inf = jnp.float32(jnp.inf)

    def insert(ranks, v):
        out = []
        for r in ranks:
            out.append(jnp.minimum(r, v))
            v = jnp.maximum(r, v)
        return out

    def smallest_body(j, ranks):
        t = sc_s[tile_rows(j), :]
        ranks = list(ranks)
        for g in range(TK // SUBLANES):
            v = t[g * SUBLANES:(g + 1) * SUBLANES, :]
            w = g % 2
            ranks[w] = tuple(insert(ranks[w], jnp.where(v >= lo, v, inf)))
        return tuple(ranks)

    empty = tuple(jnp.full((SUBLANES, TQ), inf, F32) for _ in range(FINISH_RANKS))
    ranks = lax.fori_loop(0, n_tiles, smallest_body, (empty, empty))
    final = [jnp.full((1, TQ), inf, F32) for _ in range(FINISH_RANKS)]
    for chain in ranks:
        for r in chain:
            for sub in range(SUBLANES):
                final = insert(final, r[sub:sub + 1, :])
    extra = clo - kf
    kth = final[0]
    below = -inf
    for e in range(1, FINISH_RANKS):
        kth = jnp.where(extra >= e, final[e], kth)
        below = jnp.where(extra == e, final[e - 1], below)
    near = (done_f <= 0.0) & (extra < FINISH_RANKS)
    lo = jnp.where(near, kth, lo)
    clo = jnp.where(near, jnp.where(below == kth, kf + 1.0, kf), clo)
    done_f = jnp.where(near, 1.0, done_f)
    state = (lo, hi, clo, chi, wl, wh, side, done_f)

    res = lax.while_loop(search_cond, search_body,
                         (jnp.int32(0), pending(state[-1] > 0.0)) + state)
    tau, clo = res[2], res[4]

    excess = clo > kf

    @pl.when(jnp.sum(jnp.where(excess, 1, 0)) > 0)
    def _():
        need = kf - count(lambda t, j: t > tau)
        n_steps = max(1, math.ceil(math.log2(sc_s.shape[0]))) + 1

        def step(_, jb):
            j_lo, j_hi = jb
            j_mid = (j_lo + j_hi) >> 1
            c = count(lambda t, j: (t == tau) & (key0 + j * TK <= j_mid))
            ok = c >= need
            return jnp.where(ok, j_lo, j_mid), jnp.where(ok, j_mid, j_hi)

        _, j_cut = lax.fori_loop(0, n_steps, step, (jnp.full((1, TQ), -1, jnp.int32), qpos))

        def fix(j, carry):
            t = sc_s[tile_rows(j), :]
            drop = excess & (t == tau) & (key0 + j * TK > j_cut)
            sc_s[tile_rows(j), :] = jnp.where(drop, neg_inf, t)
            return carry

        lax.fori_loop(0, n_tiles, fix, 0)

    c_exp = (DSA_HEAD_DIM ** -0.5) * math.log2(math.e)
    def attn_logits(jh, buf):
        buf[...] = _dot(ckv_ref[half_rows(jh), :], ql_ref[...])

    ql_f = ql_ref[...].astype(F32)
    bound = jnp.sqrt(jnp.sum(ql_f * ql_f, axis=0, keepdims=True)) * cmax_ref[...]

    def fast_values(jh, p_buf):
        cols_j = pl.ds(pl.multiple_of(jh * TH, TH), TH)
        acc_s[...] += _dot(ckvt_ref[:, cols_j], p_buf[...])

    def fast_half(jh, buf, p_buf, jh_next, buf_next, jh_prev, p_prev):
        keep = jnp.where(sc_s[half_rows(jh), :] >= tau, 1.0, 0.0).astype(BF16)
        keys_next = ckv_ref[half_rows(jh_next), :]
        vals_prev = ckvt_ref[:, pl.ds(pl.multiple_of(jh_prev * TH, TH), TH)]
        per_group = H // COLUMN_GROUPS
        for grp in range(COLUMN_GROUPS):
            gcols = slice(grp * per_group * TQ, (grp + 1) * per_group * TQ)
            buf_next[:, gcols] = _dot(keys_next, ql_ref[:, gcols])
            for hh in range(grp * per_group, (grp + 1) * per_group):
                cols = slice(hh * TQ, (hh + 1) * TQ)
                p = jnp.exp2((buf[:, cols] - bound[:, cols]) * c_exp).astype(BF16)
                p_buf[:, cols] = p * keep
            acc_s[:, gcols] += _dot(vals_prev, p_prev[:, gcols])

    def fast_pair(j, carry):
        fast_half(2 * j, lg_a, p_a, 2 * j + 1, lg_b, jnp.maximum(2 * j - 1, 0), p_b)
        fast_half(2 * j + 1, lg_b, p_b, jnp.minimum(2 * j + 2, last_half), lg_a, 2 * j, p_a)
        return carry

    acc_s[...] = jnp.zeros_like(acc_s)
    p_b[...] = jnp.zeros_like(p_b)
    attn_logits(0, lg_a)
    _fori_grouped(n_tiles, fast_pair, 0)
    fast_values(last_half, p_b)
    denom_ok = acc_s[RKV:RKV + 1, :] >= DENOM_FLOOR
    underflowed = jnp.sum(jnp.where(denom_ok, 0, 1)) > 0

    def attn_softmax(jh, buf, p_buf):
        bias_s[...] = jnp.where(sc_s[half_rows(jh), :] >= tau, 0.0, MASKED)
        alphas = []
        for hh in range(H):
            cols = slice(hh * TQ, (hh + 1) * TQ)
            m_old = m_s[hh:hh + 1, :]
            mx = _fold(buf[:, cols] + bias_s[...], jnp.maximum)
            m_new = jnp.maximum(m_old, jnp.max(mx, axis=0, keepdims=True))
            m_s[hh:hh + 1, :] = m_new
            alphas.append(jnp.exp2((m_old - m_new) * c_exp))
            p_buf[:, cols] = jnp.exp2((buf[:, cols] + bias_s[...] - m_new) * c_exp).astype(BF16)
        return jnp.concatenate(alphas, axis=1)

    def attn_values(jh, p_buf, alpha):
        cols_j = pl.ds(pl.multiple_of(jh * TH, TH), TH)
        acc_s[...] = acc_s[...] * alpha + _dot(ckvt_ref[:, cols_j], p_buf[...])

    def attn_pair(j, alpha_b):
        attn_logits(2 * j + 1, lg_b)
        alpha_a = attn_softmax(2 * j, lg_a, p_a)
        attn_values(jnp.maximum(2 * j - 1, 0), p_b, alpha_b)
        attn_logits(jnp.minimum(2 * j + 2, last_half), lg_a)
        alpha_b = attn_softmax(2 * j + 1, lg_b, p_b)
        attn_values(2 * j, p_a, alpha_a)
        return alpha_b

    @pl.when(underflowed)
    def _():
        m_s[...] = jnp.full_like(m_s, MASKED)
        acc_s[...] = jnp.zeros_like(acc_s)
        p_b[...] = jnp.zeros_like(p_b)
        attn_logits(0, lg_a)
        alpha_b = lax.fori_loop(0, n_tiles, attn_pair, jnp.ones((1, H * TQ), F32))
        attn_values(last_half, p_b, alpha_b)

    o_lat = (acc_s[0:RKV, :] / acc_s[RKV:RKV + 1, :]).astype(BF16)
    o_t = jnp.concatenate(
        [_dot(wuvt_ref[hh], o_lat[:, hh * TQ:(hh + 1) * TQ]) for hh in range(H)], axis=0)
    out_t = _dot(woutt_ref[...], o_t.astype(BF16))
    o_ref[...] = x_ref[...] + out_t.T


def _dsa_attn(x, qi, wt, ql, kidx, ckv, ckvt, kv_norm, w_uv, w_out):
    b, s, d = x.shape
    H, RKV, HI, TQ = DSA_HEADS, DSA_KV_RANK, IDX_HEADS, ATT_Q
    nq = s // TQ
    top_k = min(TOPK_MAX, s // 4)
    wuvt = w_uv.transpose(1, 2, 0).astype(BF16)
    woutt = w_out.T.astype(BF16)
    cmax = (math.sqrt(RKV) * jnp.max(jnp.abs(kv_norm))).reshape(1, 1).astype(F32)
    qrow = pl.BlockSpec((None, TQ, d), lambda bi, i: (bi, i, 0))
    qblk = lambda r, c: pl.BlockSpec((None, r, c), lambda bi, i: (bi * nq + i, 0, 0))
    seq = lambda width: pl.BlockSpec((s, width), lambda bi, i: (bi, 0))
    return pl.pallas_call(
        functools.partial(_dsa_attn_kernel, top_k=top_k),
        grid=(b, nq),
        in_specs=[qrow, qblk(LANES, HI * TQ), qblk(HI, TQ), qblk(RKV, H * TQ), seq(LANES), seq(RKV),
                  pl.BlockSpec((KV_ROWS, s), lambda bi, i: (0, bi)),
                  _resident(wuvt.shape), _resident(woutt.shape), _resident((1, 1))],
        out_specs=qrow,
        out_shape=jax.ShapeDtypeStruct((b, s, d), F32),
        scratch_shapes=[
            pltpu.VMEM((s, TQ), F32),
            pltpu.VMEM((ATT_K // 2, H * TQ), F32),
            pltpu.VMEM((ATT_K // 2, H * TQ), F32),
            pltpu.VMEM((ATT_K // 2, H * TQ), BF16),
            pltpu.VMEM((ATT_K // 2, H * TQ), BF16),
            pltpu.VMEM((ATT_K // 2, TQ), F32),
            pltpu.VMEM((KV_ROWS, H * TQ), F32),
            pltpu.VMEM((H, TQ), F32),
        ],
        compiler_params=_params(2),
        name="dsa_attn",
    )(x, qi, wt, ql, kidx, ckv, ckvt, wuvt, woutt, cmax)


def _dsa(x, g, w_in, q_norm, kv_norm, w_uq, w_qidx, kidx_g, kidx_b, w_uk, w_uv, w_out):
    b, s, d = x.shape
    qi, wt, ql, kidx, ckv, ckvt = _dsa_proj(x.reshape(b * s, d), g, w_in, q_norm, kv_norm, w_uq,
                                            w_qidx, kidx_g, kidx_b, w_uk)
    return _dsa_attn(x, qi, wt, ql, kidx, ckv, ckvt, kv_norm, w_uv, w_out)


def kernel(x, norm_ffn1, w_ffn1_in, w_ffn1_out, norm_mix, norm_ffn2, w_ffn2_in, w_ffn2_out, ret_w_in,
           ret_w_out, dsa_w_in, dsa_q_norm, dsa_kv_norm, dsa_w_uq, dsa_w_qidx, dsa_kidx_g, dsa_kidx_b,
           dsa_w_uk, dsa_w_uv, dsa_w_out, final_norm):
    b, s, d = x.shape
    depth = norm_ffn1.shape[0]
    for layer in range(depth):
        x = _ffn(x.reshape(b * s, d), norm_ffn1[layer], w_ffn1_in[layer], w_ffn1_out[layer])
        x = x.reshape(b, s, d)
        j = layer // 2
        if layer % 2 == 0:
            x = _retention(x, norm_mix[layer], ret_w_in[j], ret_w_out[j])
        else:
            x = _dsa(x, norm_mix[layer], dsa_w_in[j], dsa_q_norm[j], dsa_kv_norm[j], dsa_w_uq[j],
                     dsa_w_qidx[j], dsa_kidx_g[j], dsa_kidx_b[j], dsa_w_uk[j], dsa_w_uv[j],
                     dsa_w_out[j])
        last = layer == depth - 1
        x = _ffn(x.reshape(b * s, d), norm_ffn2[layer], w_ffn2_in[layer], w_ffn2_out[layer],
                 final_g=final_norm if last else None)
        x = x.reshape(b, s, d)
    return x
```

```python
import functools
import math

import jax
import jax.numpy as jnp
import numpy as np
from jax import lax
from jax.experimental import pallas as pl
from jax.experimental.pallas import tpu as pltpu

F32 = jnp.float32
BF16 = jnp.bfloat16

D_MODEL = 1024
DEPTH = 4
D_FF = 2816
RMS_EPS = 1e-6

RET_HEADS = 4
RET_DK = D_MODEL // RET_HEADS
RET_DV = 2 * RET_DK
RET_CHUNK = 128
ROT_BASE = 10000.0

DSA_HEADS = 8
DSA_HEAD_DIM = D_MODEL // DSA_HEADS
DSA_Q_RANK = 256
DSA_KV_RANK = 256
IDX_HEADS = 8
IDX_DIM = 64
TOPK_MAX = 256

LANES = 128
SUBLANES = 8
BF16_ROWS = 16
VMEM_LIMIT_BYTES = 56 * 1024 * 1024

FFN_ROWS = 512
FFN_CHUNK = 256
RET_ROWS = 512
PROJ_ROWS = 512
ATT_Q = LANES
ATT_K = 512
KV_ROWS = DSA_KV_RANK + BF16_ROWS
MASKED = -1e30
SEARCH_CAP = 400
SEARCH_UNCHECKED = 9
FINISH_RANKS = 4
DENOM_FLOOR = 2.0 ** -80
REDUCE_WAYS = 4
LOOP_GROUP = 4
COLUMN_GROUPS = 4


def _resident(shape):
    nd = len(shape)
    return pl.BlockSpec(shape, lambda *_: (0,) * nd, pipeline_mode=pl.Buffered(1))


def _params(n_grid, flags=None):
    return pltpu.CompilerParams(
        dimension_semantics=("arbitrary",) * n_grid, vmem_limit_bytes=VMEM_LIMIT_BYTES, flags=flags)


def _rms(x, g):
    y = x * lax.rsqrt(jnp.mean(x * x, axis=-1, keepdims=True) + RMS_EPS)
    return y * g


def _dot(a, b):
    return jnp.dot(a, b, preferred_element_type=F32)


def _dot_nt(a, b):
    return lax.dot_general(a, b, (((1,), (1,)), ((), ())), preferred_element_type=F32)


def _dot_tn(a, b):
    return lax.dot_general(a, b, (((0,), (0,)), ((), ())), preferred_element_type=F32)


def _ffn_kernel(x_ref, g_ref, win_ref, wo_ref, *rest, final):
    if final:
        fg_ref, o_ref, acc_ref = rest
    else:
        o_ref, acc_ref = rest
    x = x_ref[...]
    h = _rms(x, g_ref[...]).astype(BF16)
    acc_ref[...] = jnp.zeros_like(acc_ref)
    for c in range(D_FF // FFN_CHUNK):
        cols = slice(c * FFN_CHUNK, (c + 1) * FFN_CHUNK)
        gate = _dot(h, win_ref[:, cols])
        up = _dot(h, win_ref[:, D_FF + c * FFN_CHUNK:D_FF + (c + 1) * FFN_CHUNK])
        a = (gate * jax.nn.sigmoid(gate) * up).astype(BF16)
        acc_ref[...] += _dot(a, wo_ref[cols, :])
    y = x + 0.5 * acc_ref[...]
    if final:
        y = _rms(y, fg_ref[...])
    o_ref[...] = y


def _ffn(x2, g, w_in, w_out, final_g=None):
    n, d = x2.shape
    win = w_in.astype(BF16)
    wo = w_out.astype(BF16)
    row = pl.BlockSpec((FFN_ROWS, d), lambda i: (i, 0))
    ins = [x2, g.reshape(1, d), win, wo]
    specs = [row, _resident((1, d)), _resident(win.shape), _resident(wo.shape)]
    if final_g is not None:
        ins.append(final_g.reshape(1, d))
        specs.append(_resident((1, d)))
    return pl.pallas_call(
        functools.partial(_ffn_kernel, final=final_g is not None),
        grid=(n // FFN_ROWS,),
        in_specs=specs,
        out_specs=row,
        out_shape=jax.ShapeDtypeStruct((n, d), F32),
        scratch_shapes=[pltpu.VMEM((FFN_ROWS, d), F32)],
        compiler_params=_params(1),
        name="ffn_final" if final_g is not None else "ffn",
    )(*ins)


def _ret_kernel(x_ref, g_ref, win_ref, wout_ref, cos_ref, sin_ref, inner_ref, qdec_ref, kdec_ref,
                cdec_ref, o_ref, q_s, k_s, v_s, gate_s, z_s, state_s):
    H, dk, dv, C = RET_HEADS, RET_DK, RET_DV, RET_CHUNK
    half = dk // 2

    @pl.when(pl.program_id(1) == 0)
    def _():
        state_s[...] = jnp.zeros_like(state_s)

    x = x_ref[...]
    h = _rms(x, g_ref[...]).astype(BF16)
    cos = cos_ref[...]
    sin = sin_ref[...]
    q = _dot(h, win_ref[:, 0:H * dk])
    k = _dot(h, win_ref[:, H * dk:2 * H * dk])
    for hh in range(H):
        q0 = q[:, hh * dk:hh * dk + half]
        q1 = q[:, hh * dk + half:(hh + 1) * dk]
        q_s[:, hh * dk:hh * dk + half] = (q0 * cos - q1 * sin).astype(BF16)
        q_s[:, hh * dk + half:(hh + 1) * dk] = (q1 * cos + q0 * sin).astype(BF16)
        k0 = k[:, hh * dk:hh * dk + half]
        k1 = k[:, hh * dk + half:(hh + 1) * dk]
        k_s[:, hh * dk:hh * dk + half] = (k0 * cos - k1 * sin) * (dk ** -0.5)
        k_s[:, hh * dk + half:(hh + 1) * dk] = (k1 * cos + k0 * sin) * (dk ** -0.5)
    v_s[...] = _dot(h, win_ref[:, 2 * H * dk:2 * H * dk + H * dv]).astype(BF16)
    gate_s[...] = _dot(h, win_ref[:, 2 * H * dk + H * dv:])

    def chunk(c, carry):
        r0 = pl.multiple_of(c * C, C)
        rows = pl.ds(r0, C)
        for hh in range(H):
            qc = q_s[rows, hh * dk:(hh + 1) * dk]
            kc = k_s[rows, hh * dk:(hh + 1) * dk]
            vc = v_s[rows, hh * dv:(hh + 1) * dv]
            st = state_s[hh]
            scores = _dot_nt(qc, kc.astype(BF16)) * inner_ref[hh]
            inner = _dot(scores.astype(BF16), vc)
            cross = _dot(qc, st.astype(BF16)) * qdec_ref[hh]
            kd = (kc * kdec_ref[hh]).astype(BF16)
            state_s[hh] = st * cdec_ref[hh] + _dot_tn(kd, vc)
            out = inner + cross
            out = out * lax.rsqrt(jnp.mean(out * out, axis=-1, keepdims=True) + RMS_EPS)
            gt = gate_s[rows, hh * dv:(hh + 1) * dv]
            z_s[rows, hh * dv:(hh + 1) * dv] = (gt * jax.nn.sigmoid(gt) * out).astype(BF16)
        return carry

    lax.fori_loop(0, x.shape[0] // C, chunk, 0, unroll=True)
    o_ref[...] = x + _dot(z_s[...], wout_ref[...])


def _ret_tables():
    H, C = RET_HEADS, RET_CHUNK
    log_gamma = jnp.log(1.0 - 2.0 ** (-5.0 - jnp.arange(H, dtype=F32)))
    idx = jnp.arange(C, dtype=F32)
    diff = idx[:, None] - idx[None, :]
    inner = jnp.where(diff[None] >= 0,
                      jnp.exp(jnp.maximum(diff, 0.0)[None] * log_gamma[:, None, None]), 0.0)
    qdec = jnp.exp((idx[None, :] + 1.0) * log_gamma[:, None])[:, :, None]
    kdec = jnp.exp((C - 1.0 - idx[None, :]) * log_gamma[:, None])[:, :, None]
    cdec = jnp.exp(C * log_gamma)[:, None, None]
    return inner, qdec, kdec, cdec


def _rot_tables(s):
    pos = jnp.arange(s, dtype=F32)
    freq = 1.0 / (ROT_BASE ** jnp.linspace(0.0, 1.0, RET_DK // 2, dtype=F32))
    ang = pos[:, None] * freq[None, :]
    return jnp.cos(ang), jnp.sin(ang)


def _pair_split_columns(w, heads, dim):
    d = w.shape[0]
    w = w.reshape(d, heads, dim // 2, 2)
    return jnp.concatenate([w[..., 0], w[..., 1]], axis=-1).reshape(d, heads * dim)


def _retention(x, g, w_in, w_out):
    b, s, d = x.shape
    H, dk, dv = RET_HEADS, RET_DK, RET_DV
    tb = min(RET_ROWS, s)
    wq = _pair_split_columns(w_in[:, :H * dk], H, dk)
    wk = _pair_split_columns(w_in[:, H * dk:2 * H * dk], H, dk)
    win = jnp.concatenate([wq, wk, w_in[:, 2 * H * dk:]], axis=1).astype(BF16)
    wout = w_out.astype(BF16)
    cos, sin = _rot_tables(s)
    inner, qdec, kdec, cdec = _ret_tables()
    row = pl.BlockSpec((None, tb, d), lambda bi, i: (bi, i, 0))
    rot = pl.BlockSpec((tb, dk // 2), lambda bi, i: (i, 0))
    return pl.pallas_call(
        _ret_kernel,
        grid=(b, s // tb),
        in_specs=[row, _resident((1, d)), _resident(win.shape), _resident(wout.shape), rot, rot,
                  _resident(inner.shape), _resident(qdec.shape), _resident(kdec.shape),
                  _resident(cdec.shape)],
        out_specs=row,
        out_shape=jax.ShapeDtypeStruct((b, s, d), F32),
        scratch_shapes=[
            pltpu.VMEM((tb, H * dk), BF16),
            pltpu.VMEM((tb, H * dk), F32),
            pltpu.VMEM((tb, H * dv), BF16),
            pltpu.VMEM((tb, H * dv), F32),
            pltpu.VMEM((tb, H * dv), BF16),
            pltpu.VMEM((H, dk, dv), F32),
        ],
        compiler_params=_params(2),
        name="retention",
    )(x, g.reshape(1, d), win, wout, cos, sin, inner, qdec, kdec, cdec)


def _dsa_proj_kernel(x_ref, g_ref, win_ref, winw_ref, qn_ref, kvn_ref, kg_ref, kb_ref, wqi_ref,
                     wuq_ref, wuk_ref, qi_ref, wt_ref, ql_ref, kidx_ref, ckv_ref, ckvt_ref):
    RQ, RKV, HI, H, dh, TQ = DSA_Q_RANK, DSA_KV_RANK, IDX_HEADS, DSA_HEADS, DSA_HEAD_DIM, ATT_Q
    n_blk = x_ref.shape[0] // TQ
    h = _rms(x_ref[...], g_ref[...]).astype(BF16)
    proj = _dot(h, win_ref[...])
    c_q = _rms(proj[:, :RQ], qn_ref[...])
    c_kv = _rms(proj[:, RQ:RQ + RKV], kvn_ref[...])
    kx = proj[:, RQ + RKV:RQ + RKV + LANES]
    lane = lax.broadcasted_iota(jnp.int32, kx.shape, 1)
    mu = jnp.sum(kx, axis=-1, keepdims=True) * (1.0 / IDX_DIM)
    cen = jnp.where(lane < IDX_DIM, kx - mu, 0.0)
    var = jnp.sum(cen * cen, axis=-1, keepdims=True) * (1.0 / IDX_DIM)
    kidx_ref[...] = (cen * lax.rsqrt(var + RMS_EPS) * kg_ref[...] + kb_ref[...]).astype(BF16)
    ckv_ref[...] = c_kv.astype(BF16)
    ckvt_ref[0:RKV, :] = c_kv.T.astype(BF16)
    ones_row = lax.broadcasted_iota(jnp.int32, (BF16_ROWS, x_ref.shape[0]), 0) == 0
    ckvt_ref[RKV:, :] = jnp.where(ones_row, 1.0, 0.0).astype(BF16)
    w_t = _dot_nt(winw_ref[...], h) * (HI ** -0.5)
    cq = c_q.astype(BF16)
    qi_t = (_dot_nt(wqi_ref[...], cq) * (IDX_DIM ** -0.5)).astype(BF16)
    q = _dot(cq, wuq_ref[...]).astype(BF16)
    ql_t = [_dot_nt(wuk_ref[hh], q[:, hh * dh:(hh + 1) * dh]).astype(BF16) for hh in range(H)]
    for u in range(n_blk):
        cols = slice(u * TQ, (u + 1) * TQ)
        wt_ref[u] = w_t[0:HI, cols]
        for hh in range(HI):
            qi_ref[u, :, hh * TQ:(hh + 1) * TQ] = qi_t[hh * LANES:(hh + 1) * LANES, cols]
        for hh in range(H):
            ql_ref[u, :, hh * TQ:(hh + 1) * TQ] = ql_t[hh][:, cols]


def _dsa_proj(x2, g, w_in, q_norm, kv_norm, w_uq, w_qidx, kidx_g, kidx_b, w_uk):
    n, d = x2.shape
    RQ, RKV, HI, DI, H, TQ = DSA_Q_RANK, DSA_KV_RANK, IDX_HEADS, IDX_DIM, DSA_HEADS, ATT_Q
    pad_k = jnp.zeros((d, LANES - DI), F32)
    win = jnp.concatenate([w_in[:, :RQ + RKV + DI], pad_k], axis=1).astype(BF16)
    winw = jnp.pad(w_in[:, RQ + RKV + DI:].T, ((0, BF16_ROWS - HI), (0, 0))).astype(BF16)
    kg = jnp.pad(kidx_g, (0, LANES - DI)).reshape(1, LANES)
    kb = jnp.pad(kidx_b, (0, LANES - DI)).reshape(1, LANES)
    wqi = jnp.pad(w_qidx.reshape(RQ, HI, DI), ((0, 0), (0, 0), (0, LANES - DI)))
    wqi = wqi.reshape(RQ, HI * LANES).T.astype(BF16)
    wuq = w_uq.astype(BF16)
    wuk = w_uk.transpose(1, 0, 2).astype(BF16)
    tm = min(PROJ_ROWS, n)
    nb = tm // TQ
    row = lambda w: pl.BlockSpec((tm, w), lambda i: (i, 0))
    blk = lambda r, c: pl.BlockSpec((nb, r, c), lambda i: (i, 0, 0))
    return pl.pallas_call(
        _dsa_proj_kernel,
        grid=(n // tm,),
        in_specs=[row(d), _resident((1, d)), _resident(win.shape), _resident(winw.shape),
                  _resident((1, RQ)), _resident((1, RKV)), _resident((1, LANES)),
                  _resident((1, LANES)), _resident(wqi.shape), _resident(wuq.shape),
                  _resident(wuk.shape)],
        out_specs=[blk(LANES, HI * TQ), blk(HI, TQ), blk(RKV, H * TQ), row(LANES), row(RKV),
                   pl.BlockSpec((KV_ROWS, tm), lambda i: (0, i))],
        out_shape=[
            jax.ShapeDtypeStruct((n // TQ, LANES, HI * TQ), BF16),
            jax.ShapeDtypeStruct((n // TQ, HI, TQ), F32),
            jax.ShapeDtypeStruct((n // TQ, RKV, H * TQ), BF16),
            jax.ShapeDtypeStruct((n, LANES), BF16),
            jax.ShapeDtypeStruct((n, RKV), BF16),
            jax.ShapeDtypeStruct((KV_ROWS, n), BF16),
        ],
        compiler_params=_params(1),
        name="dsa_proj",
    )(x2, g.reshape(1, d), win, winw, q_norm.reshape(1, RQ), kv_norm.reshape(1, RKV), kg, kb, wqi,
      wuq, wuk)


def _fold(t, op):
    groups = t.shape[0] // SUBLANES
    group = lambda r: t[r * SUBLANES:(r + 1) * SUBLANES, :]
    acc = [group(w) for w in range(REDUCE_WAYS)]
    for r in range(REDUCE_WAYS, groups, REDUCE_WAYS):
        acc = [op(acc[w], group(r + w)) for w in range(REDUCE_WAYS)]
    while len(acc) > 1:
        acc = [op(acc[2 * w], acc[2 * w + 1]) for w in range(len(acc) // 2)]
    return acc[0]


def _fori_grouped(n, body, init, group=LOOP_GROUP):
    def grouped(q, carry):
        for u in range(group):
            carry = body(group * q + u, carry)
        return carry
    carry = lax.fori_loop(0, n // group, grouped, init)
    return lax.fori_loop(group * (n // group), n, body, carry)


def _dsa_attn_kernel(x_ref, qi_ref, wt_ref, ql_ref, kidx_ref, ckv_ref, ckvt_ref, wuvt_ref,
                     wout_ref, cmax_ref, o_ref, sc_s, lg_a, lg_b, p_a, p_b, bias_s, acc_s, m_s, *, top_k):
    TQ, TK, TH = ATT_Q, ATT_K, ATT_K // 2
    HI, H, RKV = IDX_HEADS, DSA_HEADS, DSA_KV_RANK
    i = pl.program_id(1)
    n_tiles = (i * TQ + TQ + TK - 1) // TK
    qpos = i * TQ + lax.broadcasted_iota(jnp.int32, (1, TQ), 1)
    key0 = lax.broadcasted_iota(jnp.int32, (TK, 1), 0)
    keyh = lax.broadcasted_iota(jnp.int32, (TH, 1), 0)
    neg_inf = jnp.float32(-jnp.inf)
    kf = jnp.float32(top_k)

    def tile_rows(j):
        return pl.ds(pl.multiple_of(j * TK, TK), TK)

    w_t = wt_ref[...]
    last_half = 2 * n_tiles - 1

    def half_rows(jh):
        return pl.ds(pl.multiple_of(jh * TH, TH), TH)

    def score_dots(jh, buf):
        buf[...] = _dot(kidx_ref[half_rows(jh), :], qi_ref[...])

    def score_reduce(jh, buf, carry):
        rmax, rmin, c_ge0, c_gt0 = carry
        s = jnp.maximum(buf[:, 0:TQ], 0.0) * w_t[0:1, :]
        for hh in range(1, HI):
            s = s + jnp.maximum(buf[:, hh * TQ:(hh + 1) * TQ], 0.0) * w_t[hh:hh + 1, :]
        causal = keyh + jh * TH <= qpos
        s = jnp.where(causal, s, neg_inf)
        sc_s[half_rows(jh), :] = s
        rmax = jnp.maximum(rmax, _fold(s, jnp.maximum))
        rmin = jnp.minimum(rmin, _fold(jnp.where(causal, s, jnp.inf), jnp.minimum))
        c_ge0 = c_ge0 + _fold(jnp.where(s >= 0.0, 1.0, 0.0), jnp.add)
        c_gt0 = c_gt0 + _fold(jnp.where(s > 0.0, 1.0, 0.0), jnp.add)
        return rmax, rmin, c_ge0, c_gt0

    def score_pair(j, carry):
        score_dots(2 * j + 1, lg_b)
        carry = score_reduce(2 * j, lg_a, carry)
        score_dots(jnp.minimum(2 * j + 2, last_half), lg_a)
        return score_reduce(2 * j + 1, lg_b, carry)

    part = lambda v: jnp.full((SUBLANES, TQ), v, F32)
    score_dots(0, lg_a)
    rmax, rmin, c_ge0, c_gt0 = _fori_grouped(
        n_tiles, score_pair, (part(-jnp.inf), part(jnp.inf), part(0.0), part(0.0)))
    rmax = jnp.max(rmax, axis=0, keepdims=True)
    rmin = jnp.min(rmin, axis=0, keepdims=True)
    c_ge0 = jnp.sum(c_ge0, axis=0, keepdims=True)
    c_gt0 = jnp.sum(c_gt0, axis=0, keepdims=True)

    def count(pred):
        def body(j, acc):
            return acc + _fold(jnp.where(pred(sc_s[tile_rows(j), :], j), 1.0, 0.0), jnp.add)
        return jnp.sum(_fori_grouped(n_tiles, body, part(0.0)), axis=0, keepdims=True)

    n_valid = (qpos + 1).astype(F32)
    all_sel = n_valid <= kf
    zero_tie = (c_gt0 < kf) & (c_ge0 >= kf)
    positive = c_gt0 >= kf
    lo = jnp.where(positive | zero_tie, 0.0, rmin)
    hi = jnp.where(positive, 2.0 * rmax, 0.0)
    clo = jnp.where(positive | zero_tie, c_ge0, n_valid)
    chi = jnp.where(positive, 0.0, c_ge0)
    lo = jnp.where(all_sel, jnp.float32(jnp.finfo(jnp.float32).min), lo)
    clo = jnp.where(all_sel, n_valid, clo)
    done = all_sel | zero_tie | (clo == kf)
    one = jnp.ones((1, TQ), F32)
    log_target = math.log(top_k + 0.5)

    def pending(done):
        return jnp.sum(jnp.where(done, 0, 1))

    def search_cond(c):
        return (c[1] > 0) & (c[0] < SEARCH_CAP)

    def search_step(state):
        lo, hi, clo, chi, wl, wh, side, done_f = state
        done = done_f > 0.0
        mid = 0.5 * lo + 0.5 * hi
        fa = (jnp.log(clo) - log_target) * wl
        fb = (log_target - jnp.log(jnp.maximum(chi, 0.5))) * wh
        cand = lo + (hi - lo) * (fa / (fa + fb))
        cand = jnp.where((cand > lo) & (cand < hi), cand, mid)
        collapsed = (cand <= lo) | (cand >= hi)
        cm = count(lambda t, j: t >= cand)
        move = jnp.logical_not(done | collapsed)
        up = move & (cm >= kf)
        down = move & (cm < kf)
        wh = jnp.where(up, jnp.where(side > 0.0, 0.5 * wh, 1.0), jnp.where(down, 1.0, wh))
        wl = jnp.where(down, jnp.where(side < 0.0, 0.5 * wl, 1.0), jnp.where(up, 1.0, wl))
        side = jnp.where(up, 1.0, jnp.where(down, -1.0, side))
        lo = jnp.where(up, cand, lo)
        clo = jnp.where(up, cm, clo)
        hi = jnp.where(down, cand, hi)
        chi = jnp.where(down, cm, chi)
        done = done | collapsed | (clo == kf)
        return lo, hi, clo, chi, wl, wh, side, jnp.where(done, 1.0, 0.0)

    def search_body(c):
        state = search_step(c[2:])
        return (c[0] + 1, pending(state[-1] > 0.0)) + state

    state = (lo, hi, clo, chi, one, one, 0.0 * one, jnp.where(done, 1.0, 0.0))
    state = lax.fori_loop(0, SEARCH_UNCHECKED, lambda _, s: search_step(s), state)

    lo, hi, clo, chi, wl, wh, side, done_f = state
    inf = jnp.float32(jnp.inf)

    def insert(ranks, v):
        out = []
        for r in ranks:
            out.append(jnp.minimum(r, v))
            v = jnp.maximum(r, v)
        return out

    def smallest_body(j, ranks):
        t = sc_s[tile_rows(j), :]
        ranks = list(ranks)
        for g in range(TK // SUBLANES):
            v = t[g * SUBLANES:(g + 1) * SUBLANES, :]
            w = g % 2
            ranks[w] = tuple(insert(ranks[w], jnp.where(v >= lo, v, inf)))
        return tuple(ranks)

    empty = tuple(jnp.full((SUBLANES, TQ), inf, F32) for _ in range(FINISH_RANKS))
    ranks = lax.fori_loop(0, n_tiles, smallest_body, (empty, empty))
    final = [jnp.full((1, TQ), inf, F32) for _ in range(FINISH_RANKS)]
    for chain in ranks:
        for r in chain:
            for sub in range(SUBLANES):
                final = insert(final, r[sub:sub + 1, :])
    extra = clo - kf
    kth = final[0]
    below = -inf
    for e in range(1, FINISH_RANKS):
        kth = jnp.where(extra >= e, final[e], kth)
        below = jnp.where(extra == e, final[e - 1], below)
    near = (done_f <= 0.0) & (extra < FINISH_RANKS)
    lo = jnp.where(near, kth, lo)
    clo = jnp.where(near, jnp.where(below == kth, kf + 1.0, kf), clo)
    done_f = jnp.where(near, 1.0, done_f)
    state = (lo, hi, clo, chi, wl, wh, side, done_f)

    res = lax.while_loop(search_cond, search_body,
                         (jnp.int32(0), pending(state[-1] > 0.0)) + state)
    tau, clo = res[2], res[4]

    excess = clo > kf

    @pl.when(jnp.sum(jnp.where(excess, 1, 0)) > 0)
    def _():
        need = kf - count(lambda t, j: t > tau)
        n_steps = max(1, math.ceil(math.log2(sc_s.shape[0]))) + 1

        def step(_, jb):
            j_lo, j_hi = jb
            j_mid = (j_lo + j_hi) >> 1
            c = count(lambda t, j: (t == tau) & (key0 + j * TK <= j_mid))
            ok = c >= need
            return jnp.where(ok, j_lo, j_mid), jnp.where(ok, j_mid, j_hi)

        _, j_cut = lax.fori_loop(0, n_steps, step, (jnp.full((1, TQ), -1, jnp.int32), qpos))

        def fix(j, carry):
            t = sc_s[tile_rows(j), :]
            drop = excess & (t == tau) & (key0 + j * TK > j_cut)
            sc_s[tile_rows(j), :] = jnp.where(drop, neg_inf, t)
            return carry

        lax.fori_loop(0, n_tiles, fix, 0)

    c_exp = (DSA_HEAD_DIM ** -0.5) * math.log2(math.e)
    def attn_logits(jh, buf):
        buf[...] = _dot(ckv_ref[half_rows(jh), :], ql_ref[...])

    ql_f = ql_ref[...].astype(F32)
    bound = jnp.sqrt(jnp.sum(ql_f * ql_f, axis=0, keepdims=True)) * cmax_ref[...]

    def fast_values(jh, p_buf):
        cols_j = pl.ds(pl.multiple_of(jh * TH, TH), TH)
        acc_s[...] += _dot(ckvt_ref[:, cols_j], p_buf[...])

    def fast_half(jh, buf, p_buf, jh_next, buf_next, jh_prev, p_prev):
        keep = jnp.where(sc_s[half_rows(jh), :] >= tau, 1.0, 0.0).astype(BF16)
        keys_next = ckv_ref[half_rows(jh_next), :]
        vals_prev = ckvt_ref[:, pl.ds(pl.multiple_of(jh_prev * TH, TH), TH)]
        per_group = H // COLUMN_GROUPS
        for grp in range(COLUMN_GROUPS):
            gcols = slice(grp * per_group * TQ, (grp + 1) * per_group * TQ)
            buf_next[:, gcols] = _dot(keys_next, ql_ref[:, gcols])
            for hh in range(grp * per_group, (grp + 1) * per_group):
                cols = slice(hh * TQ, (hh + 1) * TQ)
                p = jnp.exp2((buf[:, cols] - bound[:, cols]) * c_exp).astype(BF16)
                p_buf[:, cols] = p * keep
            acc_s[:, gcols] += _dot(vals_prev, p_prev[:, gcols])

    def fast_pair(j, carry):
        fast_half(2 * j, lg_a, p_a, 2 * j + 1, lg_b, jnp.maximum(2 * j - 1, 0), p_b)
        fast_half(2 * j + 1, lg_b, p_b, jnp.minimum(2 * j + 2, last_half), lg_a, 2 * j, p_a)
        return carry

    acc_s[...] = jnp.zeros_like(acc_s)
    p_b[...] = jnp.zeros_like(p_b)
    attn_logits(0, lg_a)
    _fori_grouped(n_tiles, fast_pair, 0)
    fast_values(last_half, p_b)
    denom_ok = acc_s[RKV:RKV + 1, :] >= DENOM_FLOOR
    underflowed = jnp.sum(jnp.where(denom_ok, 0, 1)) > 0

    def attn_softmax(jh, buf, p_buf):
        bias_s[...] = jnp.where(sc_s[half_rows(jh), :] >= tau, 0.0, MASKED)
        alphas = []
        for hh in range(H):
            cols = slice(hh * TQ, (hh + 1) * TQ)
            m_old = m_s[hh:hh + 1, :]
            mx = _fold(buf[:, cols] + bias_s[...], jnp.maximum)
            m_new = jnp.maximum(m_old, jnp.max(mx, axis=0, keepdims=True))
            m_s[hh:hh + 1, :] = m_new
            alphas.append(jnp.exp2((m_old - m_new) * c_exp))
            p_buf[:, cols] = jnp.exp2((buf[:, cols] + bias_s[...] - m_new) * c_exp).astype(BF16)
        return jnp.concatenate(alphas, axis=1)

    def attn_values(jh, p_buf, alpha):
        cols_j = pl.ds(pl.multiple_of(jh * TH, TH), TH)
        acc_s[...] = acc_s[...] * alpha + _dot(ckvt_ref[:, cols_j], p_buf[...])

    def attn_pair(j, alpha_b):
        attn_logits(2 * j + 1, lg_b)
        alpha_a = attn_softmax(2 * j, lg_a, p_a)
        attn_values(jnp.maximum(2 * j - 1, 0), p_b, alpha_b)
        attn_logits(jnp.minimum(2 * j + 2, last_half), lg_a)
        alpha_b = attn_softmax(2 * j + 1, lg_b, p_b)
        attn_values(2 * j, p_a, alpha_a)
        return alpha_b

    @pl.when(underflowed)
    def _():
        m_s[...] = jnp.full_like(m_s, MASKED)
        acc_s[...] = jnp.zeros_like(acc_s)
        p_b[...] = jnp.zeros_like(p_b)
        attn_logits(0, lg_a)
        alpha_b = lax.fori_loop(0, n_tiles, attn_pair, jnp.ones((1, H * TQ), F32))
        attn_values(last_half, p_b, alpha_b)

    o_lat = (acc_s[0:RKV, :] / acc_s[RKV:RKV + 1, :]).astype(BF16)
    o_t = jnp.concatenate(
        [_dot(wuvt_ref[hh], o_lat[:, hh * TQ:(hh + 1) * TQ]) for hh in range(H)], axis=0)
    o_ref[...] = x_ref[...] + _dot(o_t.T.astype(BF16), wout_ref[...])


def _dsa_attn(x, qi, wt, ql, kidx, ckv, ckvt, kv_norm, w_uv, w_out):
    b, s, d = x.shape
    H, RKV, HI, TQ = DSA_HEADS, DSA_KV_RANK, IDX_HEADS, ATT_Q
    nq = s // TQ
    top_k = min(TOPK_MAX, s // 4)
    wuvt = w_uv.transpose(1, 2, 0).astype(BF16)
    wout = w_out.astype(BF16)
    cmax = (math.sqrt(RKV) * jnp.max(jnp.abs(kv_norm))).reshape(1, 1).astype(F32)
    qrow = pl.BlockSpec((None, TQ, d), lambda bi, i: (bi, i, 0))
    qblk = lambda r, c: pl.BlockSpec((None, r, c), lambda bi, i: (bi * nq + i, 0, 0))
    seq = lambda width: pl.BlockSpec((s, width), lambda bi, i: (bi, 0))
    return pl.pallas_call(
        functools.partial(_dsa_attn_kernel, top_k=top_k),
        grid=(b, nq),
        in_specs=[qrow, qblk(LANES, HI * TQ), qblk(HI, TQ), qblk(RKV, H * TQ), seq(LANES), seq(RKV),
                  pl.BlockSpec((KV_ROWS, s), lambda bi, i: (0, bi)),
                  _resident(wuvt.shape), _resident(wout.shape), _resident((1, 1))],
        out_specs=qrow,
        out_shape=jax.ShapeDtypeStruct((b, s, d), F32),
        scratch_shapes=[
            pltpu.VMEM((s, TQ), F32),
            pltpu.VMEM((ATT_K // 2, H * TQ), F32),
            pltpu.VMEM((ATT_K // 2, H * TQ), F32),
            pltpu.VMEM((ATT_K // 2, H * TQ), BF16),
            pltpu.VMEM((ATT_K // 2, H * TQ), BF16),
            pltpu.VMEM((ATT_K // 2, TQ), F32),
            pltpu.VMEM((KV_ROWS, H * TQ), F32),
            pltpu.VMEM((H, TQ), F32),
        ],
        compiler_params=_params(2),
        name="dsa_attn",
    )(x, qi, wt, ql, kidx, ckv, ckvt, wuvt, wout, cmax)


def _dsa(x, g, w_in, q_norm, kv_norm, w_uq, w_qidx, kidx_g, kidx_b, w_uk, w_uv, w_out):
    b, s, d = x.shape
    qi, wt, ql, kidx, ckv, ckvt = _dsa_proj(x.reshape(b * s, d), g, w_in, q_norm, kv_norm, w_uq,
                                            w_qidx, kidx_g, kidx_b, w_uk)
    return _dsa_attn(x, qi, wt, ql, kidx, ckv, ckvt, kv_norm, w_uv, w_out)


def kernel(x, norm_ffn1, w_ffn1_in, w_ffn1_out, norm_mix, norm_ffn2, w_ffn2_in, w_ffn2_out, ret_w_in,
           ret_w_out, dsa_w_in, dsa_q_norm, dsa_kv_norm, dsa_w_uq, dsa_w_qidx, dsa_kidx_g, dsa_kidx_b,
           dsa_w_uk, dsa_w_uv, dsa_w_out, final_norm):
    b, s, d = x.shape
    depth = norm_ffn1.shape[0]
    for layer in range(depth):
        x = _ffn(x.reshape(b * s, d), norm_ffn1[layer], w_ffn1_in[layer], w_ffn1_out[layer])
        x = x.reshape(b, s, d)
        j = layer // 2
        if layer % 2 == 0:
            x = _retention(x, norm_mix[layer], ret_w_in[j], ret_w_out[j])
        else:
            x = _dsa(x, norm_mix[layer], dsa_w_in[j], dsa_q_norm[j], dsa_kv_norm[j], dsa_w_uq[j],
                     dsa_w_qidx[j], dsa_kidx_g[j], dsa_kidx_b[j], dsa_w_uk[j], dsa_w_uv[j],
                     dsa_w_out[j])
        last = layer == depth - 1
        x = _ffn(x.reshape(b * s, d), norm_ffn2[layer], w_ffn2_in[layer], w_ffn2_out[layer],
                 final_g=final_norm if last else None)
        x = x.reshape(b, s, d)
    return x
```

```python
import functools
import math

import jax
import jax.numpy as jnp
import numpy as np
from jax import lax
from jax.experimental import pallas as pl
from jax.experimental.pallas import tpu as pltpu

F32 = jnp.float32
BF16 = jnp.bfloat16

D_MODEL = 1024
DEPTH = 4
D_FF = 2816
RMS_EPS = 1e-6

RET_HEADS = 4
RET_DK = D_MODEL // RET_HEADS
RET_DV = 2 * RET_DK
RET_CHUNK = 128
ROT_BASE = 10000.0

DSA_HEADS = 8
DSA_HEAD_DIM = D_MODEL // DSA_HEADS
DSA_Q_RANK = 256
DSA_KV_RANK = 256
IDX_HEADS = 8
IDX_DIM = 64
TOPK_MAX = 256

LANES = 128
SUBLANES = 8
BF16_ROWS = 16
VMEM_LIMIT_BYTES = 56 * 1024 * 1024

FFN_ROWS = 512
FFN_CHUNK = 256
RET_ROWS = 512
PROJ_ROWS = 512
ATT_Q = LANES
ATT_K = 512
KV_ROWS = DSA_KV_RANK + BF16_ROWS
MASKED = -1e30
SEARCH_CAP = 400
SEARCH_UNCHECKED = 9
FINISH_RANKS = 4
DENOM_FLOOR = 2.0 ** -80
REDUCE_WAYS = 4
LOOP_GROUP = 4
COLUMN_GROUPS = 4


def _resident(shape):
    nd = len(shape)
    return pl.BlockSpec(shape, lambda *_: (0,) * nd, pipeline_mode=pl.Buffered(1))


def _params(n_grid, flags=None):
    return pltpu.CompilerParams(
        dimension_semantics=("arbitrary",) * n_grid, vmem_limit_bytes=VMEM_LIMIT_BYTES, flags=flags)


def _rms(x, g):
    y = x * lax.rsqrt(jnp.mean(x * x, axis=-1, keepdims=True) + RMS_EPS)
    return y * g


def _dot(a, b):
    return jnp.dot(a, b, preferred_element_type=F32)


def _dot_nt(a, b):
    return lax.dot_general(a, b, (((1,), (1,)), ((), ())), preferred_element_type=F32)


def _dot_tn(a, b):
    return lax.dot_general(a, b, (((0,), (0,)), ((), ())), preferred_element_type=F32)


def _ffn_kernel(x_ref, g_ref, win_ref, wo_ref, *rest, final):
    if final:
        fg_ref, o_ref, acc_ref = rest
    else:
        o_ref, acc_ref = rest
    x = x_ref[...]
    h = _rms(x, g_ref[...]).astype(BF16)
    acc_ref[...] = jnp.zeros_like(acc_ref)
    for c in range(D_FF // FFN_CHUNK):
        cols = slice(c * FFN_CHUNK, (c + 1) * FFN_CHUNK)
        gate = _dot(h, win_ref[:, cols])
        up = _dot(h, win_ref[:, D_FF + c * FFN_CHUNK:D_FF + (c + 1) * FFN_CHUNK])
        a = (gate * jax.nn.sigmoid(gate) * up).astype(BF16)
        acc_ref[...] += _dot(a, wo_ref[cols, :])
    y = x + 0.5 * acc_ref[...]
    if final:
        y = _rms(y, fg_ref[...])
    o_ref[...] = y


def _ffn(x2, g, w_in, w_out, final_g=None):
    n, d = x2.shape
    win = w_in.astype(BF16)
    wo = w_out.astype(BF16)
    row = pl.BlockSpec((FFN_ROWS, d), lambda i: (i, 0))
    ins = [x2, g.reshape(1, d), win, wo]
    specs = [row, _resident((1, d)), _resident(win.shape), _resident(wo.shape)]
    if final_g is not None:
        ins.append(final_g.reshape(1, d))
        specs.append(_resident((1, d)))
    return pl.pallas_call(
        functools.partial(_ffn_kernel, final=final_g is not None),
        grid=(n // FFN_ROWS,),
        in_specs=specs,
        out_specs=row,
        out_shape=jax.ShapeDtypeStruct((n, d), F32),
        scratch_shapes=[pltpu.VMEM((FFN_ROWS, d), F32)],
        compiler_params=_params(1),
        name="ffn_final" if final_g is not None else "ffn",
    )(*ins)


def _ret_kernel(x_ref, g_ref, win_ref, wout_ref, cos_ref, sin_ref, inner_ref, qdec_ref, kdec_ref,
                cdec_ref, o_ref, q_s, k_s, v_s, gate_s, z_s, state_s):
    H, dk, dv, C = RET_HEADS, RET_DK, RET_DV, RET_CHUNK
    half = dk // 2

    @pl.when(pl.program_id(1) == 0)
    def _():
        state_s[...] = jnp.zeros_like(state_s)

    x = x_ref[...]
    h = _rms(x, g_ref[...]).astype(BF16)
    cos = cos_ref[...]
    sin = sin_ref[...]
    q = _dot(h, win_ref[:, 0:H * dk])
    k = _dot(h, win_ref[:, H * dk:2 * H * dk])
    for hh in range(H):
        q0 = q[:, hh * dk:hh * dk + half]
        q1 = q[:, hh * dk + half:(hh + 1) * dk]
        q_s[:, hh * dk:hh * dk + half] = (q0 * cos - q1 * sin).astype(BF16)
        q_s[:, hh * dk + half:(hh + 1) * dk] = (q1 * cos + q0 * sin).astype(BF16)
        k0 = k[:, hh * dk:hh * dk + half]
        k1 = k[:, hh * dk + half:(hh + 1) * dk]
        k_s[:, hh * dk:hh * dk + half] = (k0 * cos - k1 * sin) * (dk ** -0.5)
        k_s[:, hh * dk + half:(hh + 1) * dk] = (k1 * cos + k0 * sin) * (dk ** -0.5)
    v_s[...] = _dot(h, win_ref[:, 2 * H * dk:2 * H * dk + H * dv]).astype(BF16)
    gate_s[...] = _dot(h, win_ref[:, 2 * H * dk + H * dv:])

    def chunk(c, carry):
        r0 = pl.multiple_of(c * C, C)
        rows = pl.ds(r0, C)
        for hh in range(H):
            qc = q_s[rows, hh * dk:(hh + 1) * dk]
            kc = k_s[rows, hh * dk:(hh + 1) * dk]
            vc = v_s[rows, hh * dv:(hh + 1) * dv]
            st = state_s[hh]
            scores = _dot_nt(qc, kc.astype(BF16)) * inner_ref[hh]
            inner = _dot(scores.astype(BF16), vc)
            cross = _dot(qc, st.astype(BF16)) * qdec_ref[hh]
            kd = (kc * kdec_ref[hh]).astype(BF16)
            state_s[hh] = st * cdec_ref[hh] + _dot_tn(kd, vc)
            out = inner + cross
            out = out * lax.rsqrt(jnp.mean(out * out, axis=-1, keepdims=True) + RMS_EPS)
            gt = gate_s[rows, hh * dv:(hh + 1) * dv]
            z_s[rows, hh * dv:(hh + 1) * dv] = (gt * jax.nn.sigmoid(gt) * out).astype(BF16)
        return carry

    lax.fori_loop(0, x.shape[0] // C, chunk, 0, unroll=True)
    o_ref[...] = x + _dot(z_s[...], wout_ref[...])


def _ret_tables():
    H, C = RET_HEADS, RET_CHUNK
    log_gamma = jnp.log(1.0 - 2.0 ** (-5.0 - jnp.arange(H, dtype=F32)))
    idx = jnp.arange(C, dtype=F32)
    diff = idx[:, None] - idx[None, :]
    inner = jnp.where(diff[None] >= 0,
                      jnp.exp(jnp.maximum(diff, 0.0)[None] * log_gamma[:, None, None]), 0.0)
    qdec = jnp.exp((idx[None, :] + 1.0) * log_gamma[:, None])[:, :, None]
    kdec = jnp.exp((C - 1.0 - idx[None, :]) * log_gamma[:, None])[:, :, None]
    cdec = jnp.exp(C * log_gamma)[:, None, None]
    return inner, qdec, kdec, cdec


def _rot_tables(s):
    pos = jnp.arange(s, dtype=F32)
    freq = 1.0 / (ROT_BASE ** jnp.linspace(0.0, 1.0, RET_DK // 2, dtype=F32))
    ang = pos[:, None] * freq[None, :]
    return jnp.cos(ang), jnp.sin(ang)


def _pair_split_columns(w, heads, dim):
    d = w.shape[0]
    w = w.reshape(d, heads, dim // 2, 2)
    return jnp.concatenate([w[..., 0], w[..., 1]], axis=-1).reshape(d, heads * dim)


def _retention(x, g, w_in, w_out):
    b, s, d = x.shape
    H, dk, dv = RET_HEADS, RET_DK, RET_DV
    tb = min(RET_ROWS, s)
    wq = _pair_split_columns(w_in[:, :H * dk], H, dk)
    wk = _pair_split_columns(w_in[:, H * dk:2 * H * dk], H, dk)
    win = jnp.concatenate([wq, wk, w_in[:, 2 * H * dk:]], axis=1).astype(BF16)
    wout = w_out.astype(BF16)
    cos, sin = _rot_tables(s)
    inner, qdec, kdec, cdec = _ret_tables()
    row = pl.BlockSpec((None, tb, d), lambda bi, i: (bi, i, 0))
    rot = pl.BlockSpec((tb, dk // 2), lambda bi, i: (i, 0))
    return pl.pallas_call(
        _ret_kernel,
        grid=(b, s // tb),
        in_specs=[row, _resident((1, d)), _resident(win.shape), _resident(wout.shape), rot, rot,
                  _resident(inner.shape), _resident(qdec.shape), _resident(kdec.shape),
                  _resident(cdec.shape)],
        out_specs=row,
        out_shape=jax.ShapeDtypeStruct((b, s, d), F32),
        scratch_shapes=[
            pltpu.VMEM((tb, H * dk), BF16),
            pltpu.VMEM((tb, H * dk), F32),
            pltpu.VMEM((tb, H * dv), BF16),
            pltpu.VMEM((tb, H * dv), F32),
            pltpu.VMEM((tb, H * dv), BF16),
            pltpu.VMEM((H, dk, dv), F32),
        ],
        compiler_params=_params(2),
        name="retention",
    )(x, g.reshape(1, d), win, wout, cos, sin, inner, qdec, kdec, cdec)


def _dsa_proj_kernel(x_ref, g_ref, win_ref, winw_ref, qn_ref, kvn_ref, kg_ref, kb_ref, wqi_ref,
                     wuq_ref, wuk_ref, qi_ref, wt_ref, ql_ref, kidx_ref, ckv_ref, ckvt_ref):
    RQ, RKV, HI, H, dh, TQ = DSA_Q_RANK, DSA_KV_RANK, IDX_HEADS, DSA_HEADS, DSA_HEAD_DIM, ATT_Q
    n_blk = x_ref.shape[0] // TQ
    h = _rms(x_ref[...], g_ref[...]).astype(BF16)
    proj = _dot(h, win_ref[...])
    c_q = _rms(proj[:, :RQ], qn_ref[...])
    c_kv = _rms(proj[:, RQ:RQ + RKV], kvn_ref[...])
    kx = proj[:, RQ + RKV:RQ + RKV + LANES]
    lane = lax.broadcasted_iota(jnp.int32, kx.shape, 1)
    mu = jnp.sum(kx, axis=-1, keepdims=True) * (1.0 / IDX_DIM)
    cen = jnp.where(lane < IDX_DIM, kx - mu, 0.0)
    var = jnp.sum(cen * cen, axis=-1, keepdims=True) * (1.0 / IDX_DIM)
    kidx_ref[...] = (cen * lax.rsqrt(var + RMS_EPS) * kg_ref[...] + kb_ref[...]).astype(BF16)
    ckv_ref[...] = c_kv.astype(BF16)
    ckvt_ref[0:RKV, :] = c_kv.T.astype(BF16)
    ones_row = lax.broadcasted_iota(jnp.int32, (BF16_ROWS, x_ref.shape[0]), 0) == 0
    ckvt_ref[RKV:, :] = jnp.where(ones_row, 1.0, 0.0).astype(BF16)
    w_t = _dot_nt(winw_ref[...], h) * (HI ** -0.5)
    cq = c_q.astype(BF16)
    qi_t = (_dot_nt(wqi_ref[...], cq) * (IDX_DIM ** -0.5)).astype(BF16)
    q = _dot(cq, wuq_ref[...]).astype(BF16)
    ql_t = [_dot_nt(wuk_ref[hh], q[:, hh * dh:(hh + 1) * dh]).astype(BF16) for hh in range(H)]
    for u in range(n_blk):
        cols = slice(u * TQ, (u + 1) * TQ)
        wt_ref[u] = w_t[0:HI, cols]
        for hh in range(HI):
            qi_ref[u, :, hh * TQ:(hh + 1) * TQ] = qi_t[hh * LANES:(hh + 1) * LANES, cols]
        for hh in range(H):
            ql_ref[u, :, hh * TQ:(hh + 1) * TQ] = ql_t[hh][:, cols]


def _dsa_proj(x2, g, w_in, q_norm, kv_norm, w_uq, w_qidx, kidx_g, kidx_b, w_uk):
    n, d = x2.shape
    RQ, RKV, HI, DI, H, TQ = DSA_Q_RANK, DSA_KV_RANK, IDX_HEADS, IDX_DIM, DSA_HEADS, ATT_Q
    pad_k = jnp.zeros((d, LANES - DI), F32)
    win = jnp.concatenate([w_in[:, :RQ + RKV + DI], pad_k], axis=1).astype(BF16)
    winw = jnp.pad(w_in[:, RQ + RKV + DI:].T, ((0, BF16_ROWS - HI), (0, 0))).astype(BF16)
    kg = jnp.pad(kidx_g, (0, LANES - DI)).reshape(1, LANES)
    kb = jnp.pad(kidx_b, (0, LANES - DI)).reshape(1, LANES)
    wqi = jnp.pad(w_qidx.reshape(RQ, HI, DI), ((0, 0), (0, 0), (0, LANES - DI)))
    wqi = wqi.reshape(RQ, HI * LANES).T.astype(BF16)
    wuq = w_uq.astype(BF16)
    wuk = w_uk.transpose(1, 0, 2).astype(BF16)
    tm = min(PROJ_ROWS, n)
    nb = tm // TQ
    row = lambda w: pl.BlockSpec((tm, w), lambda i: (i, 0))
    blk = lambda r, c: pl.BlockSpec((nb, r, c), lambda i: (i, 0, 0))
    return pl.pallas_call(
        _dsa_proj_kernel,
        grid=(n // tm,),
        in_specs=[row(d), _resident((1, d)), _resident(win.shape), _resident(winw.shape),
                  _resident((1, RQ)), _resident((1, RKV)), _resident((1, LANES)),
                  _resident((1, LANES)), _resident(wqi.shape), _resident(wuq.shape),
                  _resident(wuk.shape)],
        out_specs=[blk(LANES, HI * TQ), blk(HI, TQ), blk(RKV, H * TQ), row(LANES), row(RKV),
                   pl.BlockSpec((KV_ROWS, tm), lambda i: (0, i))],
        out_shape=[
            jax.ShapeDtypeStruct((n // TQ, LANES, HI * TQ), BF16),
            jax.ShapeDtypeStruct((n // TQ, HI, TQ), F32),
            jax.ShapeDtypeStruct((n // TQ, RKV, H * TQ), BF16),
            jax.ShapeDtypeStruct((n, LANES), BF16),
            jax.ShapeDtypeStruct((n, RKV), BF16),
            jax.ShapeDtypeStruct((KV_ROWS, n), BF16),
        ],
        compiler_params=_params(1),
        name="dsa_proj",
    )(x2, g.reshape(1, d), win, winw, q_norm.reshape(1, RQ), kv_norm.reshape(1, RKV), kg, kb, wqi,
      wuq, wuk)


def _fold(t, op):
    groups = t.shape[0] // SUBLANES
    group = lambda r: t[r * SUBLANES:(r + 1) * SUBLANES, :]
    acc = [group(w) for w in range(REDUCE_WAYS)]
    for r in range(REDUCE_WAYS, groups, REDUCE_WAYS):
        acc = [op(acc[w], group(r + w)) for w in range(REDUCE_WAYS)]
    while len(acc) > 1:
        acc = [op(acc[2 * w], acc[2 * w + 1]) for w in range(len(acc) // 2)]
    return acc[0]


def _fori_grouped(n, body, init, group=LOOP_GROUP):
    def grouped(q, carry):
        for u in range(group):
            carry = body(group * q + u, carry)
        return carry
    carry = lax.fori_loop(0, n // group, grouped, init)
    return lax.fori_loop(group * (n // group), n, body, carry)


def _dsa_attn_kernel(x_ref, qi_ref, wt_ref, ql_ref, kidx_ref, ckv_ref, ckvt_ref, wuvt_ref,
                     wout_ref, cmax_ref, o_ref, sc_s, lg_a, lg_b, p_a, p_b, bias_s, acc_s, m_s, *, top_k):
    TQ, TK, TH = ATT_Q, ATT_K, ATT_K // 2
    HI, H, RKV = IDX_HEADS, DSA_HEADS, DSA_KV_RANK
    i = pl.program_id(1)
    n_tiles = (i * TQ + TQ + TK - 1) // TK
    qpos = i * TQ + lax.broadcasted_iota(jnp.int32, (1, TQ), 1)
    key0 = lax.broadcasted_iota(jnp.int32, (TK, 1), 0)
    keyh = lax.broadcasted_iota(jnp.int32, (TH, 1), 0)
    neg_inf = jnp.float32(-jnp.inf)
    kf = jnp.float32(top_k)

    def tile_rows(j):
        return pl.ds(pl.multiple_of(j * TK, TK), TK)

    w_t = wt_ref[...]
    last_half = 2 * n_tiles - 1

    def half_rows(jh):
        return pl.ds(pl.multiple_of(jh * TH, TH), TH)

    def score_dots(jh, buf):
        buf[...] = _dot(kidx_ref[half_rows(jh), :], qi_ref[...])

    def score_reduce(jh, buf, carry):
        rmax, rmin, c_ge0, c_gt0 = carry
        s = jnp.maximum(buf[:, 0:TQ], 0.0) * w_t[0:1, :]
        for hh in range(1, HI):
            s = s + jnp.maximum(buf[:, hh * TQ:(hh + 1) * TQ], 0.0) * w_t[hh:hh + 1, :]
        causal = keyh + jh * TH <= qpos
        s = jnp.where(causal, s, neg_inf)
        sc_s[half_rows(jh), :] = s
        rmax = jnp.maximum(rmax, _fold(s, jnp.maximum))
        rmin = jnp.minimum(rmin, _fold(jnp.where(causal, s, jnp.inf), jnp.minimum))
        c_ge0 = c_ge0 + _fold(jnp.where(s >= 0.0, 1.0, 0.0), jnp.add)
        c_gt0 = c_gt0 + _fold(jnp.where(s > 0.0, 1.0, 0.0), jnp.add)
        return rmax, rmin, c_ge0, c_gt0

    def score_pair(j, carry):
        score_dots(2 * j + 1, lg_b)
        carry = score_reduce(2 * j, lg_a, carry)
        score_dots(jnp.minimum(2 * j + 2, last_half), lg_a)
        return score_reduce(2 * j + 1, lg_b, carry)

    part = lambda v: jnp.full((SUBLANES, TQ), v, F32)
    score_dots(0, lg_a)
    rmax, rmin, c_ge0, c_gt0 = _fori_grouped(
        n_tiles, score_pair, (part(-jnp.inf), part(jnp.inf), part(0.0), part(0.0)))
    rmax = jnp.max(rmax, axis=0, keepdims=True)
    rmin = jnp.min(rmin, axis=0, keepdims=True)
    c_ge0 = jnp.sum(c_ge0, axis=0, keepdims=True)
    c_gt0 = jnp.sum(c_gt0, axis=0, keepdims=True)

    def count(pred):
        def body(j, acc):
            return acc + _fold(jnp.where(pred(sc_s[tile_rows(j), :], j), 1.0, 0.0), jnp.add)
        return jnp.sum(_fori_grouped(n_tiles, body, part(0.0)), axis=0, keepdims=True)

    n_valid = (qpos + 1).astype(F32)
    all_sel = n_valid <= kf
    zero_tie = (c_gt0 < kf) & (c_ge0 >= kf)
    positive = c_gt0 >= kf
    lo = jnp.where(positive | zero_tie, 0.0, rmin)
    hi = jnp.where(positive, 2.0 * rmax, 0.0)
    clo = jnp.where(positive | zero_tie, c_ge0, n_valid)
    chi = jnp.where(positive, 0.0, c_ge0)
    lo = jnp.where(all_sel, jnp.float32(jnp.finfo(jnp.float32).min), lo)
    clo = jnp.where(all_sel, n_valid, clo)
    done = all_sel | zero_tie | (clo == kf)
    one = jnp.ones((1, TQ), F32)
    log_target = math.log(top_k + 0.5)

    def pending(done):
        return jnp.sum(jnp.where(done, 0, 1))

    def search_cond(c):
        return (c[1] > 0) & (c[0] < SEARCH_CAP)

    def search_step(state):
        lo, hi, clo, chi, wl, wh, side, done_f = state
        done = done_f > 0.0
        mid = 0.5 * lo + 0.5 * hi
        fa = (jnp.log(clo) - log_target) * wl
        fb = (log_target - jnp.log(jnp.maximum(chi, 0.5))) * wh
        cand = lo + (hi - lo) * (fa / (fa + fb))
        cand = jnp.where((cand > lo) & (cand < hi), cand, mid)
        collapsed = (cand <= lo) | (cand >= hi)
        cm = count(lambda t, j: t >= cand)
        move = jnp.logical_not(done | collapsed)
        up = move & (cm >= kf)
        down = move & (cm < kf)
        wh = jnp.where(up, jnp.where(side > 0.0, 0.5 * wh, 1.0), jnp.where(down, 1.0, wh))
        wl = jnp.where(down, jnp.where(side < 0.0, 0.5 * wl, 1.0), jnp.where(up, 1.0, wl))
        side = jnp.where(up, 1.0, jnp.where(down, -1.0, side))
        lo = jnp.where(up, cand, lo)
        clo = jnp.where(up, cm, clo)
        hi = jnp.where(down, cand, hi)
        chi = jnp.where(down, cm, chi)
        done = done | collapsed | (clo == kf)
        return lo, hi, clo, chi, wl, wh, side, jnp.where(done, 1.0, 0.0)

    def search_body(c):
        state = search_step(c[2:])
        return (c[0] + 1, pending(state[-1] > 0.0)) + state

    state = (lo, hi, clo, chi, one, one, 0.0 * one, jnp.where(done, 1.0, 0.0))
    state = lax.fori_loop(0, SEARCH_UNCHECKED, lambda _, s: search_step(s), state)

    lo, hi, clo, chi, wl, wh, side, done_f = state
    inf = jnp.float32(jnp.inf)

    def insert(ranks, v):
        out = []
        for r in ranks:
            out.append(jnp.minimum(r, v))
            v = jnp.maximum(r, v)
        return out

    def smallest_body(j, ranks):
        t = sc_s[tile_rows(j), :]
        ranks = list(ranks)
        for g in range(TK // SUBLANES):
            v = t[g * SUBLANES:(g + 1) * SUBLANES, :]
            w = g % 2
            ranks[w] = tuple(insert(ranks[w], jnp.where(v >= lo, v, inf)))
        return tuple(ranks)

    empty = tuple(jnp.full((SUBLANES, TQ), inf, F32) for _ in range(FINISH_RANKS))
    ranks = lax.fori_loop(0, n_tiles, smallest_body, (empty, empty))
    final = [jnp.full((1, TQ), inf, F32) for _ in range(FINISH_RANKS)]
    for chain in ranks:
        for r in chain:
            for sub in range(SUBLANES):
                final = insert(final, r[sub:sub + 1, :])
    extra = clo - kf
    kth = final[0]
    below = -inf
    for e in range(1, FINISH_RANKS):
        kth = jnp.where(extra >= e, final[e], kth)
        below = jnp.where(extra == e, final[e - 1], below)
    near = (done_f <= 0.0) & (extra < FINISH_RANKS)
    lo = jnp.where(near, kth, lo)
    clo = jnp.where(near, jnp.where(below == kth, kf + 1.0, kf), clo)
    done_f = jnp.where(near, 1.0, done_f)
    state = (lo, hi, clo, chi, wl, wh, side, done_f)

    res = lax.while_loop(search_cond, search_body,
                         (jnp.int32(0), pending(state[-1] > 0.0)) + state)
    tau, clo = res[2], res[4]

    excess = clo > kf

    def drop_excess_ties():
        need = kf - count(lambda t, j: t > tau)
        n_steps = max(1, math.ceil(math.log2(sc_s.shape[0]))) + 1

        def step(_, jb):
            j_lo, j_hi = jb
            j_mid = (j_lo + j_hi) >> 1
            c = count(lambda t, j: (t == tau) & (key0 + j * TK <= j_mid))
            ok = c >= need
            return jnp.where(ok, j_lo, j_mid), jnp.where(ok, j_mid, j_hi)

        _, j_cut = lax.fori_loop(0, n_steps, step, (jnp.full((1, TQ), -1, jnp.int32), qpos))

        def fix(j, carry):
            t = sc_s[tile_rows(j), :]
            drop = excess & (t == tau) & (key0 + j * TK > j_cut)
            sc_s[tile_rows(j), :] = jnp.where(drop, neg_inf, t)
            return carry

        lax.fori_loop(0, n_tiles, fix, 0)

    c_exp = (DSA_HEAD_DIM ** -0.5) * math.log2(math.e)
    def attn_logits(jh, buf):
        buf[...] = _dot(ckv_ref[half_rows(jh), :], ql_ref[...])

    ql_f = ql_ref[...].astype(F32)
    bound = jnp.sqrt(jnp.sum(ql_f * ql_f, axis=0, keepdims=True)) * cmax_ref[...]

    def fast_values(jh, p_buf):
        cols_j = pl.ds(pl.multiple_of(jh * TH, TH), TH)
        acc_s[...] += _dot(ckvt_ref[:, cols_j], p_buf[...])

    def fast_half(jh, buf, p_buf, jh_next, buf_next, jh_prev, p_prev):
        keep = jnp.where(sc_s[half_rows(jh), :] >= tau, 1.0, 0.0).astype(BF16)
        keys_next = ckv_ref[half_rows(jh_next), :]
        vals_prev = ckvt_ref[:, pl.ds(pl.multiple_of(jh_prev * TH, TH), TH)]
        per_group = H // COLUMN_GROUPS
        for grp in range(COLUMN_GROUPS):
            gcols = slice(grp * per_group * TQ, (grp + 1) * per_group * TQ)
            buf_next[:, gcols] = _dot(keys_next, ql_ref[:, gcols])
            for hh in range(grp * per_group, (grp + 1) * per_group):
                cols = slice(hh * TQ, (hh + 1) * TQ)
                p = jnp.exp2((buf[:, cols] - bound[:, cols]) * c_exp).astype(BF16)
                p_buf[:, cols] = p * keep
            acc_s[:, gcols] += _dot(vals_prev, p_prev[:, gcols])

    def fast_pair(j, carry):
        fast_half(2 * j, lg_a, p_a, 2 * j + 1, lg_b, jnp.maximum(2 * j - 1, 0), p_b)
        fast_half(2 * j + 1, lg_b, p_b, jnp.minimum(2 * j + 2, last_half), lg_a, 2 * j, p_a)
        return carry

    acc_s[...] = jnp.zeros_like(acc_s)
    p_b[...] = jnp.zeros_like(p_b)
    attn_logits(0, lg_a)
    _fori_grouped(n_tiles, fast_pair, 0)
    fast_values(last_half, p_b)

    def write_output():
        o_lat = (acc_s[0:RKV, :] / acc_s[RKV:RKV + 1, :]).astype(BF16)
        o_t = jnp.concatenate(
            [_dot(wuvt_ref[hh], o_lat[:, hh * TQ:(hh + 1) * TQ]) for hh in range(H)], axis=0)
        o_ref[...] = x_ref[...] + _dot(o_t.T.astype(BF16), wout_ref[...])

    write_output()
    denom_bad = jnp.logical_not(acc_s[RKV:RKV + 1, :] >= DENOM_FLOOR)
    n_denom_bad = jnp.sum(jnp.where(denom_bad, 1, 0))
    redo = n_denom_bad + jnp.sum(jnp.where(excess, 1, 0)) > 0

    def attn_softmax(jh, buf, p_buf):
        bias_s[...] = jnp.where(sc_s[half_rows(jh), :] >= tau, 0.0, MASKED)
        alphas = []
        for hh in range(H):
            cols = slice(hh * TQ, (hh + 1) * TQ)
            m_old = m_s[hh:hh + 1, :]
            mx = _fold(buf[:, cols] + bias_s[...], jnp.maximum)
            m_new = jnp.maximum(m_old, jnp.max(mx, axis=0, keepdims=True))
            m_s[hh:hh + 1, :] = m_new
            alphas.append(jnp.exp2((m_old - m_new) * c_exp))
            p_buf[:, cols] = jnp.exp2((buf[:, cols] + bias_s[...] - m_new) * c_exp).astype(BF16)
        return jnp.concatenate(alphas, axis=1)

    def attn_values(jh, p_buf, alpha):
        cols_j = pl.ds(pl.multiple_of(jh * TH, TH), TH)
        acc_s[...] = acc_s[...] * alpha + _dot(ckvt_ref[:, cols_j], p_buf[...])

    def attn_pair(j, alpha_b):
        attn_logits(2 * j + 1, lg_b)
        alpha_a = attn_softmax(2 * j, lg_a, p_a)
        attn_values(jnp.maximum(2 * j - 1, 0), p_b, alpha_b)
        attn_logits(jnp.minimum(2 * j + 2, last_half), lg_a)
        alpha_b = attn_softmax(2 * j + 1, lg_b, p_b)
        attn_values(2 * j, p_a, alpha_a)
        return alpha_b

    @pl.when(redo)
    def _():
        drop_excess_ties()
        m_s[...] = jnp.full_like(m_s, MASKED)
        acc_s[...] = jnp.zeros_like(acc_s)
        p_b[...] = jnp.zeros_like(p_b)
        attn_logits(0, lg_a)
        alpha_b = lax.fori_loop(0, n_tiles, attn_pair, jnp.ones((1, H * TQ), F32))
        attn_values(last_half, p_b, alpha_b)
        write_output()


def _dsa_attn(x, qi, wt, ql, kidx, ckv, ckvt, kv_norm, w_uv, w_out):
    b, s, d = x.shape
    H, RKV, HI, TQ = DSA_HEADS, DSA_KV_RANK, IDX_HEADS, ATT_Q
    nq = s // TQ
    top_k = min(TOPK_MAX, s // 4)
    wuvt = w_uv.transpose(1, 2, 0).astype(BF16)
    wout = w_out.astype(BF16)
    cmax = (math.sqrt(RKV) * jnp.max(jnp.abs(kv_norm))).reshape(1, 1).astype(F32)
    qrow = pl.BlockSpec((None, TQ, d), lambda bi, i: (bi, i, 0))
    qblk = lambda r, c: pl.BlockSpec((None, r, c), lambda bi, i: (bi * nq + i, 0, 0))
    seq = lambda width: pl.BlockSpec((s, width), lambda bi, i: (bi, 0))
    return pl.pallas_call(
        functools.partial(_dsa_attn_kernel, top_k=top_k),
        grid=(b, nq),
        in_specs=[qrow, qblk(LANES, HI * TQ), qblk(HI, TQ), qblk(RKV, H * TQ), seq(LANES), seq(RKV),
                  pl.BlockSpec((KV_ROWS, s), lambda bi, i: (0, bi)),
                  _resident(wuvt.shape), _resident(wout.shape), _resident((1, 1))],
        out_specs=qrow,
        out_shape=jax.ShapeDtypeStruct((b, s, d), F32),
        scratch_shapes=[
            pltpu.VMEM((s, TQ), F32),
            pltpu.VMEM((ATT_K // 2, H * TQ), F32),
            pltpu.VMEM((ATT_K // 2, H * TQ), F32),
            pltpu.VMEM((ATT_K // 2, H * TQ), BF16),
            pltpu.VMEM((ATT_K // 2, H * TQ), BF16),
            pltpu.VMEM((ATT_K // 2, TQ), F32),
            pltpu.VMEM((KV_ROWS, H * TQ), F32),
            pltpu.VMEM((H, TQ), F32),
        ],
        compiler_params=_params(2),
        name="dsa_attn",
    )(x, qi, wt, ql, kidx, ckv, ckvt, wuvt, wout, cmax)


def _dsa(x, g, w_in, q_norm, kv_norm, w_uq, w_qidx, kidx_g, kidx_b, w_uk, w_uv, w_out):
    b, s, d = x.shape
    qi, wt, ql, kidx, ckv, ckvt = _dsa_proj(x.reshape(b * s, d), g, w_in, q_norm, kv_norm, w_uq,
                                            w_qidx, kidx_g, kidx_b, w_uk)
    return _dsa_attn(x, qi, wt, ql, kidx, ckv, ckvt, kv_norm, w_uv, w_out)


def kernel(x, norm_ffn1, w_ffn1_in, w_ffn1_out, norm_mix, norm_ffn2, w_ffn2_in, w_ffn2_out, ret_w_in,
           ret_w_out, dsa_w_in, dsa_q_norm, dsa_kv_norm, dsa_w_uq, dsa_w_qidx, dsa_kidx_g, dsa_kidx_b,
           dsa_w_uk, dsa_w_uv, dsa_w_out, final_norm):
    b, s, d = x.shape
    depth = norm_ffn1.shape[0]
    for layer in range(depth):
        x = _ffn(x.reshape(b * s, d), norm_ffn1[layer], w_ffn1_in[layer], w_ffn1_out[layer])
        x = x.reshape(b, s, d)
        j = layer // 2
        if layer % 2 == 0:
            x = _retention(x, norm_mix[layer], ret_w_in[j], ret_w_out[j])
        else:
            x = _dsa(x, norm_mix[layer], dsa_w_in[j], dsa_q_norm[j], dsa_kv_norm[j], dsa_w_uq[j],
                     dsa_w_qidx[j], dsa_kidx_g[j], dsa_kidx_b[j], dsa_w_uk[j], dsa_w_uv[j],
                     dsa_w_out[j])
        last = layer == depth - 1
        x = _ffn(x.reshape(b * s, d), norm_ffn2[layer], w_ffn2_in[layer], w_ffn2_out[layer],
                 final_g=final_norm if last else None)
        x = x.reshape(b, s, d)
    return x
```

```python
import functools
import math

import jax
import jax.numpy as jnp
import numpy as np
from jax import lax
from jax.experimental import pallas as pl
from jax.experimental.pallas import tpu as pltpu

F32 = jnp.float32
BF16 = jnp.bfloat16

D_MODEL = 1024
DEPTH = 4
D_FF = 2816
RMS_EPS = 1e-6

RET_HEADS = 4
RET_DK = D_MODEL // RET_HEADS
RET_DV = 2 * RET_DK
RET_CHUNK = 128
ROT_BASE = 10000.0

DSA_HEADS = 8
DSA_HEAD_DIM = D_MODEL // DSA_HEADS
DSA_Q_RANK = 256
DSA_KV_RANK = 256
IDX_HEADS = 8
IDX_DIM = 64
TOPK_MAX = 256

LANES = 128
SUBLANES = 8
BF16_ROWS = 16
VMEM_LIMIT_BYTES = 56 * 1024 * 1024

FFN_ROWS = 512
FFN_CHUNK = 256
RET_ROWS = 512
PROJ_ROWS = 512
ATT_Q = LANES
ATT_K = 512
KV_ROWS = DSA_KV_RANK + BF16_ROWS
MASKED = -1e30
SEARCH_CAP = 400
SEARCH_UNCHECKED = 9
FINISH_RANKS = 4
DENOM_FLOOR = 2.0 ** -80
REDUCE_WAYS = 4
LOOP_GROUP = 4
COLUMN_GROUPS = 4


def _resident(shape):
    nd = len(shape)
    return pl.BlockSpec(shape, lambda *_: (0,) * nd, pipeline_mode=pl.Buffered(1))


def _params(n_grid, flags=None):
    return pltpu.CompilerParams(
        dimension_semantics=("arbitrary",) * n_grid, vmem_limit_bytes=VMEM_LIMIT_BYTES, flags=flags)


def _rms(x, g):
    y = x * lax.rsqrt(jnp.mean(x * x, axis=-1, keepdims=True) + RMS_EPS)
    return y * g


def _dot(a, b):
    return jnp.dot(a, b, preferred_element_type=F32)


def _dot_nt(a, b):
    return lax.dot_general(a, b, (((1,), (1,)), ((), ())), preferred_element_type=F32)


def _dot_tn(a, b):
    return lax.dot_general(a, b, (((0,), (0,)), ((), ())), preferred_element_type=F32)


def _ffn_kernel(x_ref, g_ref, win_ref, wo_ref, *rest, final):
    if final:
        fg_ref, o_ref, acc_ref = rest
    else:
        o_ref, acc_ref = rest
    x = x_ref[...]
    h = _rms(x, g_ref[...]).astype(BF16)
    acc_ref[...] = jnp.zeros_like(acc_ref)
    for c in range(D_FF // FFN_CHUNK):
        cols = slice(c * FFN_CHUNK, (c + 1) * FFN_CHUNK)
        gate = _dot(h, win_ref[:, cols])
        up = _dot(h, win_ref[:, D_FF + c * FFN_CHUNK:D_FF + (c + 1) * FFN_CHUNK])
        a = (gate * jax.nn.sigmoid(gate) * up).astype(BF16)
        acc_ref[...] += _dot(a, wo_ref[cols, :])
    y = x + 0.5 * acc_ref[...]
    if final:
        y = _rms(y, fg_ref[...])
    o_ref[...] = y


def _ffn(x2, g, w_in, w_out, final_g=None):
    n, d = x2.shape
    win = w_in.astype(BF16)
    wo = w_out.astype(BF16)
    row = pl.BlockSpec((FFN_ROWS, d), lambda i: (i, 0))
    ins = [x2, g.reshape(1, d), win, wo]
    specs = [row, _resident((1, d)), _resident(win.shape), _resident(wo.shape)]
    if final_g is not None:
        ins.append(final_g.reshape(1, d))
        specs.append(_resident((1, d)))
    return pl.pallas_call(
        functools.partial(_ffn_kernel, final=final_g is not None),
        grid=(n // FFN_ROWS,),
        in_specs=specs,
        out_specs=row,
        out_shape=jax.ShapeDtypeStruct((n, d), F32),
        scratch_shapes=[pltpu.VMEM((FFN_ROWS, d), F32)],
        compiler_params=_params(1),
        name="ffn_final" if final_g is not None else "ffn",
    )(*ins)


def _ret_kernel(x_ref, g_ref, win_ref, wout_ref, cos_ref, sin_ref, inner_ref, qdec_ref, kdec_ref,
                cdec_ref, o_ref, q_s, k_s, v_s, gate_s, z_s, state_s):
    H, dk, dv, C = RET_HEADS, RET_DK, RET_DV, RET_CHUNK
    half = dk // 2

    @pl.when(pl.program_id(1) == 0)
    def _():
        state_s[...] = jnp.zeros_like(state_s)

    x = x_ref[...]
    h = _rms(x, g_ref[...]).astype(BF16)
    cos = cos_ref[...]
    sin = sin_ref[...]
    q = _dot(h, win_ref[:, 0:H * dk])
    k = _dot(h, win_ref[:, H * dk:2 * H * dk])
    for hh in range(H):
        q0 = q[:, hh * dk:hh * dk + half]
        q1 = q[:, hh * dk + half:(hh + 1) * dk]
        q_s[:, hh * dk:hh * dk + half] = (q0 * cos - q1 * sin).astype(BF16)
        q_s[:, hh * dk + half:(hh + 1) * dk] = (q1 * cos + q0 * sin).astype(BF16)
        k0 = k[:, hh * dk:hh * dk + half]
        k1 = k[:, hh * dk + half:(hh + 1) * dk]
        k_s[:, hh * dk:hh * dk + half] = (k0 * cos - k1 * sin) * (dk ** -0.5)
        k_s[:, hh * dk + half:(hh + 1) * dk] = (k1 * cos + k0 * sin) * (dk ** -0.5)
    v_s[...] = _dot(h, win_ref[:, 2 * H * dk:2 * H * dk + H * dv]).astype(BF16)
    gate_s[...] = _dot(h, win_ref[:, 2 * H * dk + H * dv:])

    def chunk(c, carry):
        r0 = pl.multiple_of(c * C, C)
        rows = pl.ds(r0, C)
        for hh in range(H):
            qc = q_s[rows, hh * dk:(hh + 1) * dk]
            kc = k_s[rows, hh * dk:(hh + 1) * dk]
            vc = v_s[rows, hh * dv:(hh + 1) * dv]
            st = state_s[hh]
            scores = _dot_nt(qc, kc.astype(BF16)) * inner_ref[hh]
            inner = _dot(scores.astype(BF16), vc)
            cross = _dot(qc, st.astype(BF16)) * qdec_ref[hh]
            kd = (kc * kdec_ref[hh]).astype(BF16)
            state_s[hh] = st * cdec_ref[hh] + _dot_tn(kd, vc)
            out = inner + cross
            out = out * lax.rsqrt(jnp.mean(out * out, axis=-1, keepdims=True) + RMS_EPS)
            gt = gate_s[rows, hh * dv:(hh + 1) * dv]
            z_s[rows, hh * dv:(hh + 1) * dv] = (gt * jax.nn.sigmoid(gt) * out).astype(BF16)
        return carry

    lax.fori_loop(0, x.shape[0] // C, chunk, 0, unroll=True)
    o_ref[...] = x + _dot(z_s[...], wout_ref[...])


def _ret_tables():
    H, C = RET_HEADS, RET_CHUNK
    log_gamma = jnp.log(1.0 - 2.0 ** (-5.0 - jnp.arange(H, dtype=F32)))
    idx = jnp.arange(C, dtype=F32)
    diff = idx[:, None] - idx[None, :]
    inner = jnp.where(diff[None] >= 0,
                      jnp.exp(jnp.maximum(diff, 0.0)[None] * log_gamma[:, None, None]), 0.0)
    qdec = jnp.exp((idx[None, :] + 1.0) * log_gamma[:, None])[:, :, None]
    kdec = jnp.exp((C - 1.0 - idx[None, :]) * log_gamma[:, None])[:, :, None]
    cdec = jnp.exp(C * log_gamma)[:, None, None]
    return inner, qdec, kdec, cdec


def _rot_tables(s):
    pos = jnp.arange(s, dtype=F32)
    freq = 1.0 / (ROT_BASE ** jnp.linspace(0.0, 1.0, RET_DK // 2, dtype=F32))
    ang = pos[:, None] * freq[None, :]
    return jnp.cos(ang), jnp.sin(ang)


def _pair_split_columns(w, heads, dim):
    d = w.shape[0]
    w = w.reshape(d, heads, dim // 2, 2)
    return jnp.concatenate([w[..., 0], w[..., 1]], axis=-1).reshape(d, heads * dim)


def _retention(x, g, w_in, w_out):
    b, s, d = x.shape
    H, dk, dv = RET_HEADS, RET_DK, RET_DV
    tb = min(RET_ROWS, s)
    wq = _pair_split_columns(w_in[:, :H * dk], H, dk)
    wk = _pair_split_columns(w_in[:, H * dk:2 * H * dk], H, dk)
    win = jnp.concatenate([wq, wk, w_in[:, 2 * H * dk:]], axis=1).astype(BF16)
    wout = w_out.astype(BF16)
    cos, sin = _rot_tables(s)
    inner, qdec, kdec, cdec = _ret_tables()
    row = pl.BlockSpec((None, tb, d), lambda bi, i: (bi, i, 0))
    rot = pl.BlockSpec((tb, dk // 2), lambda bi, i: (i, 0))
    return pl.pallas_call(
        _ret_kernel,
        grid=(b, s // tb),
        in_specs=[row, _resident((1, d)), _resident(win.shape), _resident(wout.shape), rot, rot,
                  _resident(inner.shape), _resident(qdec.shape), _resident(kdec.shape),
                  _resident(cdec.shape)],
        out_specs=row,
        out_shape=jax.ShapeDtypeStruct((b, s, d), F32),
        scratch_shapes=[
            pltpu.VMEM((tb, H * dk), BF16),
            pltpu.VMEM((tb, H * dk), F32),
            pltpu.VMEM((tb, H * dv), BF16),
            pltpu.VMEM((tb, H * dv), F32),
            pltpu.VMEM((tb, H * dv), BF16),
            pltpu.VMEM((H, dk, dv), F32),
        ],
        compiler_params=_params(2),
        name="retention",
    )(x, g.reshape(1, d), win, wout, cos, sin, inner, qdec, kdec, cdec)


def _dsa_proj_kernel(x_ref, g_ref, win_ref, winw_ref, qn_ref, kvn_ref, kg_ref, kb_ref, wqi_ref,
                     wuq_ref, wuk_ref, qi_ref, wt_ref, ql_ref, kidx_ref, ckv_ref, ckvt_ref):
    RQ, RKV, HI, H, dh, TQ = DSA_Q_RANK, DSA_KV_RANK, IDX_HEADS, DSA_HEADS, DSA_HEAD_DIM, ATT_Q
    n_blk = x_ref.shape[0] // TQ
    h = _rms(x_ref[...], g_ref[...]).astype(BF16)
    proj = _dot(h, win_ref[...])
    c_q = _rms(proj[:, :RQ], qn_ref[...])
    c_kv = _rms(proj[:, RQ:RQ + RKV], kvn_ref[...])
    kx = proj[:, RQ + RKV:RQ + RKV + LANES]
    lane = lax.broadcasted_iota(jnp.int32, kx.shape, 1)
    mu = jnp.sum(kx, axis=-1, keepdims=True) * (1.0 / IDX_DIM)
    cen = jnp.where(lane < IDX_DIM, kx - mu, 0.0)
    var = jnp.sum(cen * cen, axis=-1, keepdims=True) * (1.0 / IDX_DIM)
    kidx_ref[...] = (cen * lax.rsqrt(var + RMS_EPS) * kg_ref[...] + kb_ref[...]).astype(BF16)
    ckv_ref[...] = c_kv.astype(BF16)
    ckvt_ref[0:RKV, :] = c_kv.T.astype(BF16)
    ones_row = lax.broadcasted_iota(jnp.int32, (BF16_ROWS, x_ref.shape[0]), 0) == 0
    ckvt_ref[RKV:, :] = jnp.where(ones_row, 1.0, 0.0).astype(BF16)
    w_t = _dot_nt(winw_ref[...], h) * (HI ** -0.5)
    cq = c_q.astype(BF16)
    qi_t = (_dot_nt(wqi_ref[...], cq) * (IDX_DIM ** -0.5)).astype(BF16)
    q = _dot(cq, wuq_ref[...]).astype(BF16)
    ql_t = [_dot_nt(wuk_ref[hh], q[:, hh * dh:(hh + 1) * dh]).astype(BF16) for hh in range(H)]
    for u in range(n_blk):
        cols = slice(u * TQ, (u + 1) * TQ)
        wt_ref[u] = w_t[0:HI, cols]
        for hh in range(HI):
            qi_ref[u, :, hh * TQ:(hh + 1) * TQ] = qi_t[hh * LANES:(hh + 1) * LANES, cols]
        for hh in range(H):
            ql_ref[u, :, hh * TQ:(hh + 1) * TQ] = ql_t[hh][:, cols]


def _dsa_proj(x2, g, w_in, q_norm, kv_norm, w_uq, w_qidx, kidx_g, kidx_b, w_uk):
    n, d = x2.shape
    RQ, RKV, HI, DI, H, TQ = DSA_Q_RANK, DSA_KV_RANK, IDX_HEADS, IDX_DIM, DSA_HEADS, ATT_Q
    pad_k = jnp.zeros((d, LANES - DI), F32)
    win = jnp.concatenate([w_in[:, :RQ + RKV + DI], pad_k], axis=1).astype(BF16)
    winw = jnp.pad(w_in[:, RQ + RKV + DI:].T, ((0, BF16_ROWS - HI), (0, 0))).astype(BF16)
    kg = jnp.pad(kidx_g, (0, LANES - DI)).reshape(1, LANES)
    kb = jnp.pad(kidx_b, (0, LANES - DI)).reshape(1, LANES)
    wqi = jnp.pad(w_qidx.reshape(RQ, HI, DI), ((0, 0), (0, 0), (0, LANES - DI)))
    wqi = wqi.reshape(RQ, HI * LANES).T.astype(BF16)
    wuq = w_uq.astype(BF16)
    wuk = w_uk.transpose(1, 0, 2).astype(BF16)
    tm = min(PROJ_ROWS, n)
    nb = tm // TQ
    row = lambda w: pl.BlockSpec((tm, w), lambda i: (i, 0))
    blk = lambda r, c: pl.BlockSpec((nb, r, c), lambda i: (i, 0, 0))
    return pl.pallas_call(
        _dsa_proj_kernel,
        grid=(n // tm,),
        in_specs=[row(d), _resident((1, d)), _resident(win.shape), _resident(winw.shape),
                  _resident((1, RQ)), _resident((1, RKV)), _resident((1, LANES)),
                  _resident((1, LANES)), _resident(wqi.shape), _resident(wuq.shape),
                  _resident(wuk.shape)],
        out_specs=[blk(LANES, HI * TQ), blk(HI, TQ), blk(RKV, H * TQ), row(LANES), row(RKV),
                   pl.BlockSpec((KV_ROWS, tm), lambda i: (0, i))],
        out_shape=[
            jax.ShapeDtypeStruct((n // TQ, LANES, HI * TQ), BF16),
            jax.ShapeDtypeStruct((n // TQ, HI, TQ), F32),
            jax.ShapeDtypeStruct((n // TQ, RKV, H * TQ), BF16),
            jax.ShapeDtypeStruct((n, LANES), BF16),
            jax.ShapeDtypeStruct((n, RKV), BF16),
            jax.ShapeDtypeStruct((KV_ROWS, n), BF16),
        ],
        compiler_params=_params(1),
        name="dsa_proj",
    )(x2, g.reshape(1, d), win, winw, q_norm.reshape(1, RQ), kv_norm.reshape(1, RKV), kg, kb, wqi,
      wuq, wuk)


def _fold(t, op):
    groups = t.shape[0] // SUBLANES
    group = lambda r: t[r * SUBLANES:(r + 1) * SUBLANES, :]
    acc = [group(w) for w in range(REDUCE_WAYS)]
    for r in range(REDUCE_WAYS, groups, REDUCE_WAYS):
        acc = [op(acc[w], group(r + w)) for w in range(REDUCE_WAYS)]
    while len(acc) > 1:
        acc = [op(acc[2 * w], acc[2 * w + 1]) for w in range(len(acc) // 2)]
    return acc[0]


def _fori_grouped(n, body, init, group=LOOP_GROUP):
    def grouped(q, carry):
        for u in range(group):
            carry = body(group * q + u, carry)
        return carry
    carry = lax.fori_loop(0, n // group, grouped, init)
    return lax.fori_loop(group * (n // group), n, body, carry)


def _dsa_attn_kernel(x_ref, qi_ref, wt_ref, ql_ref, kidx_ref, ckv_ref, ckvt_ref, wuvt_ref,
                     wout_ref, cmax_ref, o_ref, sc_s, lg_a, lg_b, p_a, p_b, bias_s, acc_s, m_s, jcut_s, *, top_k):
    TQ, TK, TH = ATT_Q, ATT_K, ATT_K // 2
    HI, H, RKV = IDX_HEADS, DSA_HEADS, DSA_KV_RANK
    i = pl.program_id(1)
    n_tiles = (i * TQ + TQ + TK - 1) // TK
    qpos = i * TQ + lax.broadcasted_iota(jnp.int32, (1, TQ), 1)
    key0 = lax.broadcasted_iota(jnp.int32, (TK, 1), 0)
    keyh = lax.broadcasted_iota(jnp.int32, (TH, 1), 0)
    neg_inf = jnp.float32(-jnp.inf)
    kf = jnp.float32(top_k)

    def tile_rows(j):
        return pl.ds(pl.multiple_of(j * TK, TK), TK)

    w_t = wt_ref[...]
    last_half = 2 * n_tiles - 1

    def half_rows(jh):
        return pl.ds(pl.multiple_of(jh * TH, TH), TH)

    def score_dots(jh, buf):
        buf[...] = _dot(kidx_ref[half_rows(jh), :], qi_ref[...])

    def score_reduce(jh, buf, carry):
        rmax, rmin, c_ge0, c_gt0 = carry
        s = jnp.maximum(buf[:, 0:TQ], 0.0) * w_t[0:1, :]
        for hh in range(1, HI):
            s = s + jnp.maximum(buf[:, hh * TQ:(hh + 1) * TQ], 0.0) * w_t[hh:hh + 1, :]
        causal = keyh + jh * TH <= qpos
        s = jnp.where(causal, s, neg_inf)
        sc_s[half_rows(jh), :] = s
        rmax = jnp.maximum(rmax, _fold(s, jnp.maximum))
        rmin = jnp.minimum(rmin, _fold(jnp.where(causal, s, jnp.inf), jnp.minimum))
        c_ge0 = c_ge0 + _fold(jnp.where(s >= 0.0, 1.0, 0.0), jnp.add)
        c_gt0 = c_gt0 + _fold(jnp.where(s > 0.0, 1.0, 0.0), jnp.add)
        return rmax, rmin, c_ge0, c_gt0

    def score_pair(j, carry):
        score_dots(2 * j + 1, lg_b)
        carry = score_reduce(2 * j, lg_a, carry)
        score_dots(jnp.minimum(2 * j + 2, last_half), lg_a)
        return score_reduce(2 * j + 1, lg_b, carry)

    part = lambda v: jnp.full((SUBLANES, TQ), v, F32)
    score_dots(0, lg_a)
    rmax, rmin, c_ge0, c_gt0 = _fori_grouped(
        n_tiles, score_pair, (part(-jnp.inf), part(jnp.inf), part(0.0), part(0.0)))
    rmax = jnp.max(rmax, axis=0, keepdims=True)
    rmin = jnp.min(rmin, axis=0, keepdims=True)
    c_ge0 = jnp.sum(c_ge0, axis=0, keepdims=True)
    c_gt0 = jnp.sum(c_gt0, axis=0, keepdims=True)

    def count(pred):
        def body(j, acc):
            return acc + _fold(jnp.where(pred(sc_s[tile_rows(j), :], j), 1.0, 0.0), jnp.add)
        return jnp.sum(_fori_grouped(n_tiles, body, part(0.0)), axis=0, keepdims=True)

    n_valid = (qpos + 1).astype(F32)
    all_sel = n_valid <= kf
    zero_tie = (c_gt0 < kf) & (c_ge0 >= kf)
    positive = c_gt0 >= kf
    lo = jnp.where(positive | zero_tie, 0.0, rmin)
    hi = jnp.where(positive, 2.0 * rmax, 0.0)
    clo = jnp.where(positive | zero_tie, c_ge0, n_valid)
    chi = jnp.where(positive, 0.0, c_ge0)
    lo = jnp.where(all_sel, jnp.float32(jnp.finfo(jnp.float32).min), lo)
    clo = jnp.where(all_sel, n_valid, clo)
    done = all_sel | zero_tie | (clo == kf)
    one = jnp.ones((1, TQ), F32)
    log_target = math.log(top_k + 0.5)

    def pending(done):
        return jnp.sum(jnp.where(done, 0, 1))

    def search_cond(c):
        return (c[1] > 0) & (c[0] < SEARCH_CAP)

    def search_step(state):
        lo, hi, clo, chi, wl, wh, side, done_f = state
        done = done_f > 0.0
        mid = 0.5 * lo + 0.5 * hi
        fa = (jnp.log(clo) - log_target) * wl
        fb = (log_target - jnp.log(jnp.maximum(chi, 0.5))) * wh
        cand = lo + (hi - lo) * (fa / (fa + fb))
        cand = jnp.where((cand > lo) & (cand < hi), cand, mid)
        collapsed = (cand <= lo) | (cand >= hi)
        cm = count(lambda t, j: t >= cand)
        move = jnp.logical_not(done | collapsed)
        up = move & (cm >= kf)
        down = move & (cm < kf)
        wh = jnp.where(up, jnp.where(side > 0.0, 0.5 * wh, 1.0), jnp.where(down, 1.0, wh))
        wl = jnp.where(down, jnp.where(side < 0.0, 0.5 * wl, 1.0), jnp.where(up, 1.0, wl))
        side = jnp.where(up, 1.0, jnp.where(down, -1.0, side))
        lo = jnp.where(up, cand, lo)
        clo = jnp.where(up, cm, clo)
        hi = jnp.where(down, cand, hi)
        chi = jnp.where(down, cm, chi)
        done = done | collapsed | (clo == kf)
        return lo, hi, clo, chi, wl, wh, side, jnp.where(done, 1.0, 0.0)

    def search_body(c):
        state = search_step(c[2:])
        return (c[0] + 1, pending(state[-1] > 0.0)) + state

    state = (lo, hi, clo, chi, one, one, 0.0 * one, jnp.where(done, 1.0, 0.0))
    state = lax.fori_loop(0, SEARCH_UNCHECKED, lambda _, s: search_step(s), state)

    lo, hi, clo, chi, wl, wh, side, done_f = state
    inf = jnp.float32(jnp.inf)

    def insert(ranks, v):
        out = []
        for r in ranks:
            out.append(jnp.minimum(r, v))
            v = jnp.maximum(r, v)
        return out

    def smallest_body(j, ranks):
        t = sc_s[tile_rows(j), :]
        ranks = list(ranks)
        for g in range(TK // SUBLANES):
            v = t[g * SUBLANES:(g + 1) * SUBLANES, :]
            w = g % 2
            ranks[w] = tuple(insert(ranks[w], jnp.where(v >= lo, v, inf)))
        return tuple(ranks)

    empty = tuple(jnp.full((SUBLANES, TQ), inf, F32) for _ in range(FINISH_RANKS))
    ranks = lax.fori_loop(0, n_tiles, smallest_body, (empty, empty))
    final = [jnp.full((1, TQ), inf, F32) for _ in range(FINISH_RANKS)]
    for chain in ranks:
        for r in chain:
            for sub in range(SUBLANES):
                final = insert(final, r[sub:sub + 1, :])
    extra = clo - kf
    kth = final[0]
    below = -inf
    for e in range(1, FINISH_RANKS):
        kth = jnp.where(extra >= e, final[e], kth)
        below = jnp.where(extra == e, final[e - 1], below)
    near = (done_f <= 0.0) & (extra < FINISH_RANKS)
    lo = jnp.where(near, kth, lo)
    clo = jnp.where(near, jnp.where(below == kth, kf + 1.0, kf), clo)
    done_f = jnp.where(near, 1.0, done_f)
    state = (lo, hi, clo, chi, wl, wh, side, done_f)

    res = lax.while_loop(search_cond, search_body,
                         (jnp.int32(0), pending(state[-1] > 0.0)) + state)
    tau, clo = res[2], res[4]

    excess = clo > kf

    @pl.when(jnp.sum(jnp.where(excess, 1, 0)) > 0)
    def _():
        big = jnp.int32(sc_s.shape[0])
        ipart = lambda v: jnp.full((SUBLANES, TQ), v, jnp.int32)

        def tie_stats(j, carry):
            c_gt, c_eq, k_first, k_last = carry
            t = sc_s[tile_rows(j), :]
            tied = t == tau
            key = key0 + j * TK
            c_gt = c_gt + _fold(jnp.where(t > tau, 1.0, 0.0), jnp.add)
            c_eq = c_eq + _fold(jnp.where(tied, 1.0, 0.0), jnp.add)
            k_first = jnp.minimum(k_first, _fold(jnp.where(tied, key, big), jnp.minimum))
            k_last = jnp.maximum(k_last, _fold(jnp.where(tied, key, -1), jnp.maximum))
            return c_gt, c_eq, k_first, k_last

        c_gt, c_eq, k_first, k_last = lax.fori_loop(
            0, n_tiles, tie_stats, (part(0.0), part(0.0), ipart(big), ipart(-1)))
        need = kf - jnp.sum(c_gt, axis=0, keepdims=True)
        spare = jnp.sum(c_eq, axis=0, keepdims=True) - need
        k_first = jnp.min(k_first, axis=0, keepdims=True)
        k_last = jnp.max(k_last, axis=0, keepdims=True)
        keep_first_only = need == 1.0
        drop_last_only = spare == 1.0
        jcut_s[0:1, :] = jnp.where(keep_first_only, k_first, k_last - 1)
        general = excess & jnp.logical_not(keep_first_only | drop_last_only)

        @pl.when(jnp.sum(jnp.where(general, 1, 0)) > 0)
        def _():
            n_steps = max(1, math.ceil(math.log2(sc_s.shape[0]))) + 1

            def step(_, jb):
                j_lo, j_hi = jb
                j_mid = (j_lo + j_hi) >> 1
                c = count(lambda t, j: (t == tau) & (key0 + j * TK <= j_mid))
                ok = c >= need
                return jnp.where(ok, j_lo, j_mid), jnp.where(ok, j_mid, j_hi)

            _, j_bis = lax.fori_loop(0, n_steps, step, (jnp.full((1, TQ), -1, jnp.int32), qpos))
            jcut_s[0:1, :] = jnp.where(general, j_bis, jcut_s[0:1, :])

        j_cut = jcut_s[0:1, :]

        def fix(j, carry):
            t = sc_s[tile_rows(j), :]
            drop = excess & (t == tau) & (key0 + j * TK > j_cut)
            sc_s[tile_rows(j), :] = jnp.where(drop, neg_inf, t)
            return carry

        lax.fori_loop(0, n_tiles, fix, 0)

    c_exp = (DSA_HEAD_DIM ** -0.5) * math.log2(math.e)
    def attn_logits(jh, buf):
        buf[...] = _dot(ckv_ref[half_rows(jh), :], ql_ref[...])

    ql_f = ql_ref[...].astype(F32)
    bound = jnp.sqrt(jnp.sum(ql_f * ql_f, axis=0, keepdims=True)) * cmax_ref[...]

    def fast_values(jh, p_buf):
        cols_j = pl.ds(pl.multiple_of(jh * TH, TH), TH)
        acc_s[...] += _dot(ckvt_ref[:, cols_j], p_buf[...])

    def fast_half(jh, buf, p_buf, jh_next, buf_next, jh_prev, p_prev):
        keep = jnp.where(sc_s[half_rows(jh), :] >= tau, 1.0, 0.0).astype(BF16)
        keys_next = ckv_ref[half_rows(jh_next), :]
        vals_prev = ckvt_ref[:, pl.ds(pl.multiple_of(jh_prev * TH, TH), TH)]
        per_group = H // COLUMN_GROUPS
        for grp in range(COLUMN_GROUPS):
            gcols = slice(grp * per_group * TQ, (grp + 1) * per_group * TQ)
            buf_next[:, gcols] = _dot(keys_next, ql_ref[:, gcols])
            for hh in range(grp * per_group, (grp + 1) * per_group):
                cols = slice(hh * TQ, (hh + 1) * TQ)
                p = jnp.exp2((buf[:, cols] - bound[:, cols]) * c_exp).astype(BF16)
                p_buf[:, cols] = p * keep
            acc_s[:, gcols] += _dot(vals_prev, p_prev[:, gcols])

    def fast_pair(j, carry):
        fast_half(2 * j, lg_a, p_a, 2 * j + 1, lg_b, jnp.maximum(2 * j - 1, 0), p_b)
        fast_half(2 * j + 1, lg_b, p_b, jnp.minimum(2 * j + 2, last_half), lg_a, 2 * j, p_a)
        return carry

    acc_s[...] = jnp.zeros_like(acc_s)
    p_b[...] = jnp.zeros_like(p_b)
    attn_logits(0, lg_a)
    _fori_grouped(n_tiles, fast_pair, 0)
    fast_values(last_half, p_b)

    def write_output():
        o_lat = (acc_s[0:RKV, :] / acc_s[RKV:RKV + 1, :]).astype(BF16)
        o_t = jnp.concatenate(
            [_dot(wuvt_ref[hh], o_lat[:, hh * TQ:(hh + 1) * TQ]) for hh in range(H)], axis=0)
        o_ref[...] = x_ref[...] + _dot(o_t.T.astype(BF16), wout_ref[...])

    write_output()
    denom_bad = jnp.logical_not(acc_s[RKV:RKV + 1, :] >= DENOM_FLOOR)
    redo = jnp.sum(jnp.where(denom_bad, 1, 0)) > 0

    def attn_softmax(jh, buf, p_buf):
        bias_s[...] = jnp.where(sc_s[half_rows(jh), :] >= tau, 0.0, MASKED)
        alphas = []
        for hh in range(H):
            cols = slice(hh * TQ, (hh + 1) * TQ)
            m_old = m_s[hh:hh + 1, :]
            mx = _fold(buf[:, cols] + bias_s[...], jnp.maximum)
            m_new = jnp.maximum(m_old, jnp.max(mx, axis=0, keepdims=True))
            m_s[hh:hh + 1, :] = m_new
            alphas.append(jnp.exp2((m_old - m_new) * c_exp))
            p_buf[:, cols] = jnp.exp2((buf[:, cols] + bias_s[...] - m_new) * c_exp).astype(BF16)
        return jnp.concatenate(alphas, axis=1)

    def attn_values(jh, p_buf, alpha):
        cols_j = pl.ds(pl.multiple_of(jh * TH, TH), TH)
        acc_s[...] = acc_s[...] * alpha + _dot(ckvt_ref[:, cols_j], p_buf[...])

    def attn_pair(j, alpha_b):
        attn_logits(2 * j + 1, lg_b)
        alpha_a = attn_softmax(2 * j, lg_a, p_a)
        attn_values(jnp.maximum(2 * j - 1, 0), p_b, alpha_b)
        attn_logits(jnp.minimum(2 * j + 2, last_half), lg_a)
        alpha_b = attn_softmax(2 * j + 1, lg_b, p_b)
        attn_values(2 * j, p_a, alpha_a)
        return alpha_b

    @pl.when(redo)
    def _():
        m_s[...] = jnp.full_like(m_s, MASKED)
        acc_s[...] = jnp.zeros_like(acc_s)
        p_b[...] = jnp.zeros_like(p_b)
        attn_logits(0, lg_a)
        alpha_b = lax.fori_loop(0, n_tiles, attn_pair, jnp.ones((1, H * TQ), F32))
        attn_values(last_half, p_b, alpha_b)
        write_output()


def _dsa_attn(x, qi, wt, ql, kidx, ckv, ckvt, kv_norm, w_uv, w_out):
    b, s, d = x.shape
    H, RKV, HI, TQ = DSA_HEADS, DSA_KV_RANK, IDX_HEADS, ATT_Q
    nq = s // TQ
    top_k = min(TOPK_MAX, s // 4)
    wuvt = w_uv.transpose(1, 2, 0).astype(BF16)
    wout = w_out.astype(BF16)
    cmax = (math.sqrt(RKV) * jnp.max(jnp.abs(kv_norm))).reshape(1, 1).astype(F32)
    qrow = pl.BlockSpec((None, TQ, d), lambda bi, i: (bi, i, 0))
    qblk = lambda r, c: pl.BlockSpec((None, r, c), lambda bi, i: (bi * nq + i, 0, 0))
    seq = lambda width: pl.BlockSpec((s, width), lambda bi, i: (bi, 0))
    return pl.pallas_call(
        functools.partial(_dsa_attn_kernel, top_k=top_k),
        grid=(b, nq),
        in_specs=[qrow, qblk(LANES, HI * TQ), qblk(HI, TQ), qblk(RKV, H * TQ), seq(LANES), seq(RKV),
                  pl.BlockSpec((KV_ROWS, s), lambda bi, i: (0, bi)),
                  _resident(wuvt.shape), _resident(wout.shape), _resident((1, 1))],
        out_specs=qrow,
        out_shape=jax.ShapeDtypeStruct((b, s, d), F32),
        scratch_shapes=[
            pltpu.VMEM((s, TQ), F32),
            pltpu.VMEM((ATT_K // 2, H * TQ), F32),
            pltpu.VMEM((ATT_K // 2, H * TQ), F32),
            pltpu.VMEM((ATT_K // 2, H * TQ), BF16),
            pltpu.VMEM((ATT_K // 2, H * TQ), BF16),
            pltpu.VMEM((ATT_K // 2, TQ), F32),
            pltpu.VMEM((KV_ROWS, H * TQ), F32),
            pltpu.VMEM((H, TQ), F32),
            pltpu.VMEM((SUBLANES, TQ), jnp.int32),
        ],
        compiler_params=_params(2),
        name="dsa_attn",
    )(x, qi, wt, ql, kidx, ckv, ckvt, wuvt, wout, cmax)


def _dsa(x, g, w_in, q_norm, kv_norm, w_uq, w_qidx, kidx_g, kidx_b, w_uk, w_uv, w_out):
    b, s, d = x.shape
    qi, wt, ql, kidx, ckv, ckvt = _dsa_proj(x.reshape(b * s, d), g, w_in, q_norm, kv_norm, w_uq,
                                            w_qidx, kidx_g, kidx_b, w_uk)
    return _dsa_attn(x, qi, wt, ql, kidx, ckv, ckvt, kv_norm, w_uv, w_out)


def kernel(x, norm_ffn1, w_ffn1_in, w_ffn1_out, norm_mix, norm_ffn2, w_ffn2_in, w_ffn2_out, ret_w_in,
           ret_w_out, dsa_w_in, dsa_q_norm, dsa_kv_norm, dsa_w_uq, dsa_w_qidx, dsa_kidx_g, dsa_kidx_b,
           dsa_w_uk, dsa_w_uv, dsa_w_out, final_norm):
    b, s, d = x.shape
    depth = norm_ffn1.shape[0]
    for layer in range(depth):
        x = _ffn(x.reshape(b * s, d), norm_ffn1[layer], w_ffn1_in[layer], w_ffn1_out[layer])
        x = x.reshape(b, s, d)
        j = layer // 2
        if layer % 2 == 0:
            x = _retention(x, norm_mix[layer], ret_w_in[j], ret_w_out[j])
        else:
            x = _dsa(x, norm_mix[layer], dsa_w_in[j], dsa_q_norm[j], dsa_kv_norm[j], dsa_w_uq[j],
                     dsa_w_qidx[j], dsa_kidx_g[j], dsa_kidx_b[j], dsa_w_uk[j], dsa_w_uv[j],
                     dsa_w_out[j])
        last = layer == depth - 1
        x = _ffn(x.reshape(b * s, d), norm_ffn2[layer], w_ffn2_in[layer], w_ffn2_out[layer],
                 final_g=final_norm if last else None)
        x = x.reshape(b, s, d)
    return x
```

```python
import functools
import math

import jax
import jax.numpy as jnp
import numpy as np
from jax import lax
from jax.experimental import pallas as pl
from jax.experimental.pallas import tpu as pltpu

F32 = jnp.float32
BF16 = jnp.bfloat16

D_MODEL = 1024
DEPTH = 4
D_FF = 2816
RMS_EPS = 1e-6

RET_HEADS = 4
RET_DK = D_MODEL // RET_HEADS
RET_DV = 2 * RET_DK
RET_CHUNK = 128
ROT_BASE = 10000.0

DSA_HEADS = 8
DSA_HEAD_DIM = D_MODEL // DSA_HEADS
DSA_Q_RANK = 256
DSA_KV_RANK = 256
IDX_HEADS = 8
IDX_DIM = 64
TOPK_MAX = 256

LANES = 128
SUBLANES = 8
BF16_ROWS = 16
VMEM_LIMIT_BYTES = 56 * 1024 * 1024

FFN_ROWS = 512
FFN_CHUNK = 256
RET_ROWS = 512
PROJ_ROWS = 512
ATT_Q = LANES
ATT_K = 512
KV_ROWS = DSA_KV_RANK + BF16_ROWS
MASKED = -1e30
SEARCH_CAP = 400
SEARCH_UNCHECKED = 9
FINISH_RANKS = 4
DENOM_FLOOR = 2.0 ** -80
REDUCE_WAYS = 4
LOOP_GROUP = 4
COLUMN_GROUPS = 4


def _resident(shape):
    nd = len(shape)
    return pl.BlockSpec(shape, lambda *_: (0,) * nd, pipeline_mode=pl.Buffered(1))


def _params(n_grid, flags=None):
    return pltpu.CompilerParams(
        dimension_semantics=("arbitrary",) * n_grid, vmem_limit_bytes=VMEM_LIMIT_BYTES, flags=flags)


def _rms(x, g):
    y = x * lax.rsqrt(jnp.mean(x * x, axis=-1, keepdims=True) + RMS_EPS)
    return y * g


def _dot(a, b):
    return jnp.dot(a, b, preferred_element_type=F32)


def _dot_nt(a, b):
    return lax.dot_general(a, b, (((1,), (1,)), ((), ())), preferred_element_type=F32)


def _dot_tn(a, b):
    return lax.dot_general(a, b, (((0,), (0,)), ((), ())), preferred_element_type=F32)


def _ffn_kernel(x_ref, g_ref, win_ref, wo_ref, *rest, final):
    if final:
        fg_ref, o_ref, acc_ref = rest
    else:
        o_ref, acc_ref = rest
    x = x_ref[...]
    h = _rms(x, g_ref[...]).astype(BF16)
    acc_ref[...] = jnp.zeros_like(acc_ref)
    for c in range(D_FF // FFN_CHUNK):
        cols = slice(c * FFN_CHUNK, (c + 1) * FFN_CHUNK)
        gate = _dot(h, win_ref[:, cols])
        up = _dot(h, win_ref[:, D_FF + c * FFN_CHUNK:D_FF + (c + 1) * FFN_CHUNK])
        a = (gate * jax.nn.sigmoid(gate) * up).astype(BF16)
        acc_ref[...] += _dot(a, wo_ref[cols, :])
    y = x + 0.5 * acc_ref[...]
    if final:
        y = _rms(y, fg_ref[...])
    o_ref[...] = y


def _ffn(x2, g, w_in, w_out, final_g=None):
    n, d = x2.shape
    win = w_in.astype(BF16)
    wo = w_out.astype(BF16)
    row = pl.BlockSpec((FFN_ROWS, d), lambda i: (i, 0))
    ins = [x2, g.reshape(1, d), win, wo]
    specs = [row, _resident((1, d)), _resident(win.shape), _resident(wo.shape)]
    if final_g is not None:
        ins.append(final_g.reshape(1, d))
        specs.append(_resident((1, d)))
    return pl.pallas_call(
        functools.partial(_ffn_kernel, final=final_g is not None),
        grid=(n // FFN_ROWS,),
        in_specs=specs,
        out_specs=row,
        out_shape=jax.ShapeDtypeStruct((n, d), F32),
        scratch_shapes=[pltpu.VMEM((FFN_ROWS, d), F32)],
        compiler_params=_params(1),
        name="ffn_final" if final_g is not None else "ffn",
    )(*ins)


def _ret_kernel(x_ref, g_ref, win_ref, wout_ref, cos_ref, sin_ref, inner_ref, qdec_ref, kdec_ref,
                cdec_ref, o_ref, q_s, k_s, v_s, gate_s, z_s, state_s):
    H, dk, dv, C = RET_HEADS, RET_DK, RET_DV, RET_CHUNK
    half = dk // 2

    @pl.when(pl.program_id(1) == 0)
    def _():
        state_s[...] = jnp.zeros_like(state_s)

    x = x_ref[...]
    h = _rms(x, g_ref[...]).astype(BF16)
    cos = cos_ref[...]
    sin = sin_ref[...]
    q = _dot(h, win_ref[:, 0:H * dk])
    k = _dot(h, win_ref[:, H * dk:2 * H * dk])
    for hh in range(H):
        q0 = q[:, hh * dk:hh * dk + half]
        q1 = q[:, hh * dk + half:(hh + 1) * dk]
        q_s[:, hh * dk:hh * dk + half] = (q0 * cos - q1 * sin).astype(BF16)
        q_s[:, hh * dk + half:(hh + 1) * dk] = (q1 * cos + q0 * sin).astype(BF16)
        k0 = k[:, hh * dk:hh * dk + half]
        k1 = k[:, hh * dk + half:(hh + 1) * dk]
        k_s[:, hh * dk:hh * dk + half] = (k0 * cos - k1 * sin) * (dk ** -0.5)
        k_s[:, hh * dk + half:(hh + 1) * dk] = (k1 * cos + k0 * sin) * (dk ** -0.5)
    v_s[...] = _dot(h, win_ref[:, 2 * H * dk:2 * H * dk + H * dv]).astype(BF16)
    gate_s[...] = _dot(h, win_ref[:, 2 * H * dk + H * dv:])

    def chunk(c, carry):
        r0 = pl.multiple_of(c * C, C)
        rows = pl.ds(r0, C)
        for hh in range(H):
            qc = q_s[rows, hh * dk:(hh + 1) * dk]
            kc = k_s[rows, hh * dk:(hh + 1) * dk]
            vc = v_s[rows, hh * dv:(hh + 1) * dv]
            st = state_s[hh]
            scores = _dot_nt(qc, kc.astype(BF16)) * inner_ref[hh]
            inner = _dot(scores.astype(BF16), vc)
            cross = _dot(qc, st.astype(BF16)) * qdec_ref[hh]
            kd = (kc * kdec_ref[hh]).astype(BF16)
            state_s[hh] = st * cdec_ref[hh] + _dot_tn(kd, vc)
            out = inner + cross
            out = out * lax.rsqrt(jnp.mean(out * out, axis=-1, keepdims=True) + RMS_EPS)
            gt = gate_s[rows, hh * dv:(hh + 1) * dv]
            z_s[rows, hh * dv:(hh + 1) * dv] = (gt * jax.nn.sigmoid(gt) * out).astype(BF16)
        return carry

    lax.fori_loop(0, x.shape[0] // C, chunk, 0, unroll=True)
    o_ref[...] = x + _dot(z_s[...], wout_ref[...])


def _ret_tables():
    H, C = RET_HEADS, RET_CHUNK
    log_gamma = jnp.log(1.0 - 2.0 ** (-5.0 - jnp.arange(H, dtype=F32)))
    idx = jnp.arange(C, dtype=F32)
    diff = idx[:, None] - idx[None, :]
    inner = jnp.where(diff[None] >= 0,
                      jnp.exp(jnp.maximum(diff, 0.0)[None] * log_gamma[:, None, None]), 0.0)
    qdec = jnp.exp((idx[None, :] + 1.0) * log_gamma[:, None])[:, :, None]
    kdec = jnp.exp((C - 1.0 - idx[None, :]) * log_gamma[:, None])[:, :, None]
    cdec = jnp.exp(C * log_gamma)[:, None, None]
    return inner, qdec, kdec, cdec


def _rot_tables(s):
    pos = jnp.arange(s, dtype=F32)
    freq = 1.0 / (ROT_BASE ** jnp.linspace(0.0, 1.0, RET_DK // 2, dtype=F32))
    ang = pos[:, None] * freq[None, :]
    return jnp.cos(ang), jnp.sin(ang)


def _pair_split_columns(w, heads, dim):
    d = w.shape[0]
    w = w.reshape(d, heads, dim // 2, 2)
    return jnp.concatenate([w[..., 0], w[..., 1]], axis=-1).reshape(d, heads * dim)


def _retention(x, g, w_in, w_out):
    b, s, d = x.shape
    H, dk, dv = RET_HEADS, RET_DK, RET_DV
    tb = min(RET_ROWS, s)
    wq = _pair_split_columns(w_in[:, :H * dk], H, dk)
    wk = _pair_split_columns(w_in[:, H * dk:2 * H * dk], H, dk)
    win = jnp.concatenate([wq, wk, w_in[:, 2 * H * dk:]], axis=1).astype(BF16)
    wout = w_out.astype(BF16)
    cos, sin = _rot_tables(s)
    inner, qdec, kdec, cdec = _ret_tables()
    row = pl.BlockSpec((None, tb, d), lambda bi, i: (bi, i, 0))
    rot = pl.BlockSpec((tb, dk // 2), lambda bi, i: (i, 0))
    return pl.pallas_call(
        _ret_kernel,
        grid=(b, s // tb),
        in_specs=[row, _resident((1, d)), _resident(win.shape), _resident(wout.shape), rot, rot,
                  _resident(inner.shape), _resident(qdec.shape), _resident(kdec.shape),
                  _resident(cdec.shape)],
        out_specs=row,
        out_shape=jax.ShapeDtypeStruct((b, s, d), F32),
        scratch_shapes=[
            pltpu.VMEM((tb, H * dk), BF16),
            pltpu.VMEM((tb, H * dk), F32),
            pltpu.VMEM((tb, H * dv), BF16),
            pltpu.VMEM((tb, H * dv), F32),
            pltpu.VMEM((tb, H * dv), BF16),
            pltpu.VMEM((H, dk, dv), F32),
        ],
        compiler_params=_params(2),
        name="retention",
    )(x, g.reshape(1, d), win, wout, cos, sin, inner, qdec, kdec, cdec)


def _dsa_proj_kernel(x_ref, g_ref, win_ref, winw_ref, qn_ref, kvn_ref, kg_ref, kb_ref, wqi_ref,
                     wuq_ref, wuk_ref, qi_ref, wt_ref, ql_ref, kidx_ref, ckv_ref, ckvt_ref):
    RQ, RKV, HI, H, dh, TQ = DSA_Q_RANK, DSA_KV_RANK, IDX_HEADS, DSA_HEADS, DSA_HEAD_DIM, ATT_Q
    n_blk = x_ref.shape[0] // TQ
    h = _rms(x_ref[...], g_ref[...]).astype(BF16)
    proj = _dot(h, win_ref[...])
    c_q = _rms(proj[:, :RQ], qn_ref[...])
    c_kv = _rms(proj[:, RQ:RQ + RKV], kvn_ref[...])
    kx = proj[:, RQ + RKV:RQ + RKV + LANES]
    lane = lax.broadcasted_iota(jnp.int32, kx.shape, 1)
    mu = jnp.sum(kx, axis=-1, keepdims=True) * (1.0 / IDX_DIM)
    cen = jnp.where(lane < IDX_DIM, kx - mu, 0.0)
    var = jnp.sum(cen * cen, axis=-1, keepdims=True) * (1.0 / IDX_DIM)
    kidx_ref[...] = (cen * lax.rsqrt(var + RMS_EPS) * kg_ref[...] + kb_ref[...]).astype(BF16)
    ckv_ref[...] = c_kv.astype(BF16)
    ckvt_ref[0:RKV, :] = c_kv.T.astype(BF16)
    ones_row = lax.broadcasted_iota(jnp.int32, (BF16_ROWS, x_ref.shape[0]), 0) == 0
    ckvt_ref[RKV:, :] = jnp.where(ones_row, 1.0, 0.0).astype(BF16)
    w_t = _dot_nt(winw_ref[...], h) * (HI ** -0.5)
    cq = c_q.astype(BF16)
    qi_t = (_dot_nt(wqi_ref[...], cq) * (IDX_DIM ** -0.5)).astype(BF16)
    q = _dot(cq, wuq_ref[...]).astype(BF16)
    ql_t = [_dot_nt(wuk_ref[hh], q[:, hh * dh:(hh + 1) * dh]).astype(BF16) for hh in range(H)]
    for u in range(n_blk):
        cols = slice(u * TQ, (u + 1) * TQ)
        wt_ref[u] = w_t[0:HI, cols]
        for hh in range(HI):
            qi_ref[u, :, hh * TQ:(hh + 1) * TQ] = qi_t[hh * LANES:(hh + 1) * LANES, cols]
        for hh in range(H):
            ql_ref[u, :, hh * TQ:(hh + 1) * TQ] = ql_t[hh][:, cols]


def _dsa_proj(x2, g, w_in, q_norm, kv_norm, w_uq, w_qidx, kidx_g, kidx_b, w_uk):
    n, d = x2.shape
    RQ, RKV, HI, DI, H, TQ = DSA_Q_RANK, DSA_KV_RANK, IDX_HEADS, IDX_DIM, DSA_HEADS, ATT_Q
    pad_k = jnp.zeros((d, LANES - DI), F32)
    win = jnp.concatenate([w_in[:, :RQ + RKV + DI], pad_k], axis=1).astype(BF16)
    winw = jnp.pad(w_in[:, RQ + RKV + DI:].T, ((0, BF16_ROWS - HI), (0, 0))).astype(BF16)
    kg = jnp.pad(kidx_g, (0, LANES - DI)).reshape(1, LANES)
    kb = jnp.pad(kidx_b, (0, LANES - DI)).reshape(1, LANES)
    wqi = jnp.pad(w_qidx.reshape(RQ, HI, DI), ((0, 0), (0, 0), (0, LANES - DI)))
    wqi = wqi.reshape(RQ, HI * LANES).T.astype(BF16)
    wuq = w_uq.astype(BF16)
    wuk = w_uk.transpose(1, 0, 2).astype(BF16)
    tm = min(PROJ_ROWS, n)
    nb = tm // TQ
    row = lambda w: pl.BlockSpec((tm, w), lambda i: (i, 0))
    blk = lambda r, c: pl.BlockSpec((nb, r, c), lambda i: (i, 0, 0))
    return pl.pallas_call(
        _dsa_proj_kernel,
        grid=(n // tm,),
        in_specs=[row(d), _resident((1, d)), _resident(win.shape), _resident(winw.shape),
                  _resident((1, RQ)), _resident((1, RKV)), _resident((1, LANES)),
                  _resident((1, LANES)), _resident(wqi.shape), _resident(wuq.shape),
                  _resident(wuk.shape)],
        out_specs=[blk(LANES, HI * TQ), blk(HI, TQ), blk(RKV, H * TQ), row(LANES), row(RKV),
                   pl.BlockSpec((KV_ROWS, tm), lambda i: (0, i))],
        out_shape=[
            jax.ShapeDtypeStruct((n // TQ, LANES, HI * TQ), BF16),
            jax.ShapeDtypeStruct((n // TQ, HI, TQ), F32),
            jax.ShapeDtypeStruct((n // TQ, RKV, H * TQ), BF16),
            jax.ShapeDtypeStruct((n, LANES), BF16),
            jax.ShapeDtypeStruct((n, RKV), BF16),
            jax.ShapeDtypeStruct((KV_ROWS, n), BF16),
        ],
        compiler_params=_params(1),
        name="dsa_proj",
    )(x2, g.reshape(1, d), win, winw, q_norm.reshape(1, RQ), kv_norm.reshape(1, RKV), kg, kb, wqi,
      wuq, wuk)


def _fold(t, op):
    groups = t.shape[0] // SUBLANES
    group = lambda r: t[r * SUBLANES:(r + 1) * SUBLANES, :]
    acc = [group(w) for w in range(REDUCE_WAYS)]
    for r in range(REDUCE_WAYS, groups, REDUCE_WAYS):
        acc = [op(acc[w], group(r + w)) for w in range(REDUCE_WAYS)]
    while len(acc) > 1:
        acc = [op(acc[2 * w], acc[2 * w + 1]) for w in range(len(acc) // 2)]
    return acc[0]


def _fori_grouped(n, body, init, group=LOOP_GROUP):
    def grouped(q, carry):
        for u in range(group):
            carry = body(group * q + u, carry)
        return carry
    carry = lax.fori_loop(0, n // group, grouped, init)
    return lax.fori_loop(group * (n // group), n, body, carry)


def _dsa_attn_kernel(x_ref, qi_ref, wt_ref, ql_ref, kidx_ref, ckv_ref, ckvt_ref, wuvt_ref,
                     wout_ref, cmax_ref, tri_ref, o_ref, sc_s, lg_a, lg_b, p_a, p_b, bias_s, acc_s, m_s, jcut_s, *, top_k):
    TQ, TK, TH = ATT_Q, ATT_K, ATT_K // 2
    HI, H, RKV = IDX_HEADS, DSA_HEADS, DSA_KV_RANK
    i = pl.program_id(1)
    n_tiles = (i * TQ + TQ + TK - 1) // TK
    qpos = i * TQ + lax.broadcasted_iota(jnp.int32, (1, TQ), 1)
    key0 = lax.broadcasted_iota(jnp.int32, (TK, 1), 0)
    keyh = lax.broadcasted_iota(jnp.int32, (TH, 1), 0)
    neg_inf = jnp.float32(-jnp.inf)
    kf = jnp.float32(top_k)

    def tile_rows(j):
        return pl.ds(pl.multiple_of(j * TK, TK), TK)

    w_t = wt_ref[...]
    last_half = 2 * n_tiles - 1

    def half_rows(jh):
        return pl.ds(pl.multiple_of(jh * TH, TH), TH)

    def score_dots(jh, buf):
        buf[...] = _dot(kidx_ref[half_rows(jh), :], qi_ref[...])

    def score_reduce(jh, buf, carry):
        rmax, rmin, c_ge0, c_gt0 = carry
        s = jnp.maximum(buf[:, 0:TQ], 0.0) * w_t[0:1, :]
        for hh in range(1, HI):
            s = s + jnp.maximum(buf[:, hh * TQ:(hh + 1) * TQ], 0.0) * w_t[hh:hh + 1, :]
        causal = keyh + jh * TH <= qpos
        s = jnp.where(causal, s, neg_inf)
        sc_s[half_rows(jh), :] = s
        rmax = jnp.maximum(rmax, _fold(s, jnp.maximum))
        rmin = jnp.minimum(rmin, _fold(jnp.where(causal, s, jnp.inf), jnp.minimum))
        c_ge0 = c_ge0 + _fold(jnp.where(s >= 0.0, 1.0, 0.0), jnp.add)
        c_gt0 = c_gt0 + _fold(jnp.where(s > 0.0, 1.0, 0.0), jnp.add)
        return rmax, rmin, c_ge0, c_gt0

    def score_pair(j, carry):
        score_dots(2 * j + 1, lg_b)
        carry = score_reduce(2 * j, lg_a, carry)
        score_dots(jnp.minimum(2 * j + 2, last_half), lg_a)
        return score_reduce(2 * j + 1, lg_b, carry)

    part = lambda v: jnp.full((SUBLANES, TQ), v, F32)
    score_dots(0, lg_a)
    rmax, rmin, c_ge0, c_gt0 = _fori_grouped(
        n_tiles, score_pair, (part(-jnp.inf), part(jnp.inf), part(0.0), part(0.0)))
    rmax = jnp.max(rmax, axis=0, keepdims=True)
    rmin = jnp.min(rmin, axis=0, keepdims=True)
    c_ge0 = jnp.sum(c_ge0, axis=0, keepdims=True)
    c_gt0 = jnp.sum(c_gt0, axis=0, keepdims=True)

    def count(pred):
        def body(j, acc):
            return acc + _fold(jnp.where(pred(sc_s[tile_rows(j), :], j), 1.0, 0.0), jnp.add)
        return jnp.sum(_fori_grouped(n_tiles, body, part(0.0)), axis=0, keepdims=True)

    n_valid = (qpos + 1).astype(F32)
    all_sel = n_valid <= kf
    zero_tie = (c_gt0 < kf) & (c_ge0 >= kf)
    positive = c_gt0 >= kf
    lo = jnp.where(positive | zero_tie, 0.0, rmin)
    hi = jnp.where(positive, 2.0 * rmax, 0.0)
    clo = jnp.where(positive | zero_tie, c_ge0, n_valid)
    chi = jnp.where(positive, 0.0, c_ge0)
    lo = jnp.where(all_sel, jnp.float32(jnp.finfo(jnp.float32).min), lo)
    clo = jnp.where(all_sel, n_valid, clo)
    done = all_sel | zero_tie | (clo == kf)
    one = jnp.ones((1, TQ), F32)
    log_target = math.log(top_k + 0.5)

    def pending(done):
        return jnp.sum(jnp.where(done, 0, 1))

    def search_cond(c):
        return (c[1] > 0) & (c[0] < SEARCH_CAP)

    def search_step(state):
        lo, hi, clo, chi, wl, wh, side, done_f = state
        done = done_f > 0.0
        mid = 0.5 * lo + 0.5 * hi
        fa = (jnp.log(clo) - log_target) * wl
        fb = (log_target - jnp.log(jnp.maximum(chi, 0.5))) * wh
        cand = lo + (hi - lo) * (fa / (fa + fb))
        cand = jnp.where((cand > lo) & (cand < hi), cand, mid)
        collapsed = (cand <= lo) | (cand >= hi)
        cm = count(lambda t, j: t >= cand)
        move = jnp.logical_not(done | collapsed)
        up = move & (cm >= kf)
        down = move & (cm < kf)
        wh = jnp.where(up, jnp.where(side > 0.0, 0.5 * wh, 1.0), jnp.where(down, 1.0, wh))
        wl = jnp.where(down, jnp.where(side < 0.0, 0.5 * wl, 1.0), jnp.where(up, 1.0, wl))
        side = jnp.where(up, 1.0, jnp.where(down, -1.0, side))
        lo = jnp.where(up, cand, lo)
        clo = jnp.where(up, cm, clo)
        hi = jnp.where(down, cand, hi)
        chi = jnp.where(down, cm, chi)
        done = done | collapsed | (clo == kf)
        return lo, hi, clo, chi, wl, wh, side, jnp.where(done, 1.0, 0.0)

    def search_body(c):
        state = search_step(c[2:])
        return (c[0] + 1, pending(state[-1] > 0.0)) + state

    state = (lo, hi, clo, chi, one, one, 0.0 * one, jnp.where(done, 1.0, 0.0))
    state = lax.fori_loop(0, SEARCH_UNCHECKED, lambda _, s: search_step(s), state)

    lo, hi, clo, chi, wl, wh, side, done_f = state
    inf = jnp.float32(jnp.inf)

    def insert(ranks, v):
        out = []
        for r in ranks:
            out.append(jnp.minimum(r, v))
            v = jnp.maximum(r, v)
        return out

    def smallest_body(j, ranks):
        t = sc_s[tile_rows(j), :]
        ranks = list(ranks)
        for g in range(TK // SUBLANES):
            v = t[g * SUBLANES:(g + 1) * SUBLANES, :]
            w = g % 2
            ranks[w] = tuple(insert(ranks[w], jnp.where(v >= lo, v, inf)))
        return tuple(ranks)

    empty = tuple(jnp.full((SUBLANES, TQ), inf, F32) for _ in range(FINISH_RANKS))
    ranks = lax.fori_loop(0, n_tiles, smallest_body, (empty, empty))
    final = [jnp.full((1, TQ), inf, F32) for _ in range(FINISH_RANKS)]
    for chain in ranks:
        for r in chain:
            for sub in range(SUBLANES):
                final = insert(final, r[sub:sub + 1, :])
    extra = clo - kf
    kth = final[0]
    below = -inf
    for e in range(1, FINISH_RANKS):
        kth = jnp.where(extra >= e, final[e], kth)
        below = jnp.where(extra == e, final[e - 1], below)
    near = (done_f <= 0.0) & (extra < FINISH_RANKS)
    lo = jnp.where(near, kth, lo)
    clo = jnp.where(near, jnp.where(below == kth, kf + 1.0, kf), clo)
    done_f = jnp.where(near, 1.0, done_f)
    state = (lo, hi, clo, chi, wl, wh, side, done_f)

    res = lax.while_loop(search_cond, search_body,
                         (jnp.int32(0), pending(state[-1] > 0.0)) + state)
    tau, clo = res[2], res[4]

    excess = clo > kf

    @pl.when(jnp.sum(jnp.where(excess, 1, 0)) > 0)
    def _():
        big = jnp.int32(sc_s.shape[0])
        ipart = lambda v: jnp.full((SUBLANES, TQ), v, jnp.int32)

        def tie_stats(j, carry):
            c_gt, c_eq, k_first, k_last = carry
            t = sc_s[tile_rows(j), :]
            tied = t == tau
            key = key0 + j * TK
            c_gt = c_gt + _fold(jnp.where(t > tau, 1.0, 0.0), jnp.add)
            c_eq = c_eq + _fold(jnp.where(tied, 1.0, 0.0), jnp.add)
            k_first = jnp.minimum(k_first, _fold(jnp.where(tied, key, big), jnp.minimum))
            k_last = jnp.maximum(k_last, _fold(jnp.where(tied, key, -1), jnp.maximum))
            return c_gt, c_eq, k_first, k_last

        c_gt, c_eq, k_first, k_last = lax.fori_loop(
            0, n_tiles, tie_stats, (part(0.0), part(0.0), ipart(big), ipart(-1)))
        need = kf - jnp.sum(c_gt, axis=0, keepdims=True)
        spare = jnp.sum(c_eq, axis=0, keepdims=True) - need
        k_first = jnp.min(k_first, axis=0, keepdims=True)
        k_last = jnp.max(k_last, axis=0, keepdims=True)
        keep_first_only = need == 1.0
        drop_last_only = spare == 1.0
        jcut_s[0:1, :] = jnp.where(keep_first_only, k_first, k_last - 1)
        general = excess & jnp.logical_not(keep_first_only | drop_last_only)

        @pl.when(jnp.sum(jnp.where(general, 1, 0)) > 0)
        def _():
            def rank_and_drop(j, seen):
                t = sc_s[tile_rows(j), :]
                tied = t == tau
                rank = seen + _dot(tri_ref[...], jnp.where(tied, 1.0, 0.0).astype(BF16))
                sc_s[tile_rows(j), :] = jnp.where(general & tied & (rank > need), neg_inf, t)
                return rank[TK - 1:TK, :]

            lax.fori_loop(0, n_tiles, rank_and_drop, jnp.zeros((1, TQ), F32))
            jcut_s[0:1, :] = jnp.where(general, big, jcut_s[0:1, :])

        j_cut = jcut_s[0:1, :]

        def fix(j, carry):
            t = sc_s[tile_rows(j), :]
            drop = excess & (t == tau) & (key0 + j * TK > j_cut)
            sc_s[tile_rows(j), :] = jnp.where(drop, neg_inf, t)
            return carry

        lax.fori_loop(0, n_tiles, fix, 0)

    c_exp = (DSA_HEAD_DIM ** -0.5) * math.log2(math.e)
    def attn_logits(jh, buf):
        buf[...] = _dot(ckv_ref[half_rows(jh), :], ql_ref[...])

    ql_f = ql_ref[...].astype(F32)
    bound = jnp.sqrt(jnp.sum(ql_f * ql_f, axis=0, keepdims=True)) * cmax_ref[...]

    def fast_values(jh, p_buf):
        cols_j = pl.ds(pl.multiple_of(jh * TH, TH), TH)
        acc_s[...] += _dot(ckvt_ref[:, cols_j], p_buf[...])

    def fast_half(jh, buf, p_buf, jh_next, buf_next, jh_prev, p_prev):
        keep = jnp.where(sc_s[half_rows(jh), :] >= tau, 1.0, 0.0).astype(BF16)
        keys_next = ckv_ref[half_rows(jh_next), :]
        vals_prev = ckvt_ref[:, pl.ds(pl.multiple_of(jh_prev * TH, TH), TH)]
        per_group = H // COLUMN_GROUPS
        for grp in range(COLUMN_GROUPS):
            gcols = slice(grp * per_group * TQ, (grp + 1) * per_group * TQ)
            buf_next[:, gcols] = _dot(keys_next, ql_ref[:, gcols])
            for hh in range(grp * per_group, (grp + 1) * per_group):
                cols = slice(hh * TQ, (hh + 1) * TQ)
                p = jnp.exp2((buf[:, cols] - bound[:, cols]) * c_exp).astype(BF16)
                p_buf[:, cols] = p * keep
            acc_s[:, gcols] += _dot(vals_prev, p_prev[:, gcols])

    def fast_pair(j, carry):
        fast_half(2 * j, lg_a, p_a, 2 * j + 1, lg_b, jnp.maximum(2 * j - 1, 0), p_b)
        fast_half(2 * j + 1, lg_b, p_b, jnp.minimum(2 * j + 2, last_half), lg_a, 2 * j, p_a)
        return carry

    acc_s[...] = jnp.zeros_like(acc_s)
    p_b[...] = jnp.zeros_like(p_b)
    attn_logits(0, lg_a)
    _fori_grouped(n_tiles, fast_pair, 0)
    fast_values(last_half, p_b)

    def write_output():
        o_lat = (acc_s[0:RKV, :] / acc_s[RKV:RKV + 1, :]).astype(BF16)
        o_t = jnp.concatenate(
            [_dot(wuvt_ref[hh], o_lat[:, hh * TQ:(hh + 1) * TQ]) for hh in range(H)], axis=0)
        o_ref[...] = x_ref[...] + _dot(o_t.T.astype(BF16), wout_ref[...])

    write_output()
    denom_bad = jnp.logical_not(acc_s[RKV:RKV + 1, :] >= DENOM_FLOOR)
    redo = jnp.sum(jnp.where(denom_bad, 1, 0)) > 0

    def attn_softmax(jh, buf, p_buf):
        bias_s[...] = jnp.where(sc_s[half_rows(jh), :] >= tau, 0.0, MASKED)
        alphas = []
        for hh in range(H):
            cols = slice(hh * TQ, (hh + 1) * TQ)
            m_old = m_s[hh:hh + 1, :]
            mx = _fold(buf[:, cols] + bias_s[...], jnp.maximum)
            m_new = jnp.maximum(m_old, jnp.max(mx, axis=0, keepdims=True))
            m_s[hh:hh + 1, :] = m_new
            alphas.append(jnp.exp2((m_old - m_new) * c_exp))
            p_buf[:, cols] = jnp.exp2((buf[:, cols] + bias_s[...] - m_new) * c_exp).astype(BF16)
        return jnp.concatenate(alphas, axis=1)

    def attn_values(jh, p_buf, alpha):
        cols_j = pl.ds(pl.multiple_of(jh * TH, TH), TH)
        acc_s[...] = acc_s[...] * alpha + _dot(ckvt_ref[:, cols_j], p_buf[...])

    def attn_pair(j, alpha_b):
        attn_logits(2 * j + 1, lg_b)
        alpha_a = attn_softmax(2 * j, lg_a, p_a)
        attn_values(jnp.maximum(2 * j - 1, 0), p_b, alpha_b)
        attn_logits(jnp.minimum(2 * j + 2, last_half), lg_a)
        alpha_b = attn_softmax(2 * j + 1, lg_b, p_b)
        attn_values(2 * j, p_a, alpha_a)
        return alpha_b

    @pl.when(redo)
    def _():
        m_s[...] = jnp.full_like(m_s, MASKED)
        acc_s[...] = jnp.zeros_like(acc_s)
        p_b[...] = jnp.zeros_like(p_b)
        attn_logits(0, lg_a)
        alpha_b = lax.fori_loop(0, n_tiles, attn_pair, jnp.ones((1, H * TQ), F32))
        attn_values(last_half, p_b, alpha_b)
        write_output()


def _dsa_attn(x, qi, wt, ql, kidx, ckv, ckvt, kv_norm, w_uv, w_out):
    b, s, d = x.shape
    H, RKV, HI, TQ = DSA_HEADS, DSA_KV_RANK, IDX_HEADS, ATT_Q
    nq = s // TQ
    top_k = min(TOPK_MAX, s // 4)
    wuvt = w_uv.transpose(1, 2, 0).astype(BF16)
    wout = w_out.astype(BF16)
    cmax = (math.sqrt(RKV) * jnp.max(jnp.abs(kv_norm))).reshape(1, 1).astype(F32)
    tri = jnp.tril(jnp.ones((ATT_K, ATT_K), F32)).astype(BF16)
    qrow = pl.BlockSpec((None, TQ, d), lambda bi, i: (bi, i, 0))
    qblk = lambda r, c: pl.BlockSpec((None, r, c), lambda bi, i: (bi * nq + i, 0, 0))
    seq = lambda width: pl.BlockSpec((s, width), lambda bi, i: (bi, 0))
    return pl.pallas_call(
        functools.partial(_dsa_attn_kernel, top_k=top_k),
        grid=(b, nq),
        in_specs=[qrow, qblk(LANES, HI * TQ), qblk(HI, TQ), qblk(RKV, H * TQ), seq(LANES), seq(RKV),
                  pl.BlockSpec((KV_ROWS, s), lambda bi, i: (0, bi)),
                  _resident(wuvt.shape), _resident(wout.shape), _resident((1, 1)),
                  _resident(tri.shape)],
        out_specs=qrow,
        out_shape=jax.ShapeDtypeStruct((b, s, d), F32),
        scratch_shapes=[
            pltpu.VMEM((s, TQ), F32),
            pltpu.VMEM((ATT_K // 2, H * TQ), F32),
            pltpu.VMEM((ATT_K // 2, H * TQ), F32),
            pltpu.VMEM((ATT_K // 2, H * TQ), BF16),
            pltpu.VMEM((ATT_K // 2, H * TQ), BF16),
            pltpu.VMEM((ATT_K // 2, TQ), F32),
            pltpu.VMEM((KV_ROWS, H * TQ), F32),
            pltpu.VMEM((H, TQ), F32),
            pltpu.VMEM((SUBLANES, TQ), jnp.int32),
        ],
        compiler_params=_params(2),
        name="dsa_attn",
    )(x, qi, wt, ql, kidx, ckv, ckvt, wuvt, wout, cmax, tri)


def _dsa(x, g, w_in, q_norm, kv_norm, w_uq, w_qidx, kidx_g, kidx_b, w_uk, w_uv, w_out):
    b, s, d = x.shape
    qi, wt, ql, kidx, ckv, ckvt = _dsa_proj(x.reshape(b * s, d), g, w_in, q_norm, kv_norm, w_uq,
                                            w_qidx, kidx_g, kidx_b, w_uk)
    return _dsa_attn(x, qi, wt, ql, kidx, ckv, ckvt, kv_norm, w_uv, w_out)


def kernel(x, norm_ffn1, w_ffn1_in, w_ffn1_out, norm_mix, norm_ffn2, w_ffn2_in, w_ffn2_out, ret_w_in,
           ret_w_out, dsa_w_in, dsa_q_norm, dsa_kv_norm, dsa_w_uq, dsa_w_qidx, dsa_kidx_g, dsa_kidx_b,
           dsa_w_uk, dsa_w_uv, dsa_w_out, final_norm):
    b, s, d = x.shape
    depth = norm_ffn1.shape[0]
    for layer in range(depth):
        x = _ffn(x.reshape(b * s, d), norm_ffn1[layer], w_ffn1_in[layer], w_ffn1_out[layer])
        x = x.reshape(b, s, d)
        j = layer // 2
        if layer % 2 == 0:
            x = _retention(x, norm_mix[layer], ret_w_in[j], ret_w_out[j])
        else:
            x = _dsa(x, norm_mix[layer], dsa_w_in[j], dsa_q_norm[j], dsa_kv_norm[j], dsa_w_uq[j],
                     dsa_w_qidx[j], dsa_kidx_g[j], dsa_kidx_b[j], dsa_w_uk[j], dsa_w_uv[j],
                     dsa_w_out[j])
        last = layer == depth - 1
        x = _ffn(x.reshape(b * s, d), norm_ffn2[layer], w_ffn2_in[layer], w_ffn2_out[layer],
                 final_g=final_norm if last else None)
        x = x.reshape(b, s, d)
    return x
```

```python
import functools
import math

import jax
import jax.numpy as jnp
import numpy as np
from jax import lax
from jax.experimental import pallas as pl
from jax.experimental.pallas import tpu as pltpu

F32 = jnp.float32
BF16 = jnp.bfloat16

D_MODEL = 1024
DEPTH = 4
D_FF = 2816
RMS_EPS = 1e-6

RET_HEADS = 4
RET_DK = D_MODEL // RET_HEADS
RET_DV = 2 * RET_DK
RET_CHUNK = 128
ROT_BASE = 10000.0

DSA_HEADS = 8
DSA_HEAD_DIM = D_MODEL // DSA_HEADS
DSA_Q_RANK = 256
DSA_KV_RANK = 256
IDX_HEADS = 8
IDX_DIM = 64
TOPK_MAX = 256

LANES = 128
SUBLANES = 8
BF16_ROWS = 16
VMEM_LIMIT_BYTES = 56 * 1024 * 1024

FFN_ROWS = 512
FFN_CHUNK = 256
RET_ROWS = 512
PROJ_ROWS = 512
ATT_Q = LANES
ATT_K = 512
KV_ROWS = DSA_KV_RANK + BF16_ROWS
MASKED = -1e30
SEARCH_CAP = 400
SEARCH_UNCHECKED = 9
FINISH_RANKS = 4
DENOM_FLOOR = 2.0 ** -80
REDUCE_WAYS = 4
LOOP_GROUP = 4
COLUMN_GROUPS = 4


def _resident(shape):
    nd = len(shape)
    return pl.BlockSpec(shape, lambda *_: (0,) * nd, pipeline_mode=pl.Buffered(1))


def _params(n_grid, flags=None):
    return pltpu.CompilerParams(
        dimension_semantics=("arbitrary",) * n_grid, vmem_limit_bytes=VMEM_LIMIT_BYTES, flags=flags)


def _rms(x, g):
    y = x * lax.rsqrt(jnp.mean(x * x, axis=-1, keepdims=True) + RMS_EPS)
    return y * g


def _dot(a, b):
    return jnp.dot(a, b, preferred_element_type=F32)


def _dot_nt(a, b):
    return lax.dot_general(a, b, (((1,), (1,)), ((), ())), preferred_element_type=F32)


def _dot_tn(a, b):
    return lax.dot_general(a, b, (((0,), (0,)), ((), ())), preferred_element_type=F32)


def _ffn_kernel(x_ref, g_ref, win_ref, wo_ref, *rest, final):
    if final:
        fg_ref, o_ref, acc_ref = rest
    else:
        o_ref, acc_ref = rest
    x = x_ref[...]
    h = _rms(x, g_ref[...]).astype(BF16)
    acc_ref[...] = jnp.zeros_like(acc_ref)
    for c in range(D_FF // FFN_CHUNK):
        cols = slice(c * FFN_CHUNK, (c + 1) * FFN_CHUNK)
        gate = _dot(h, win_ref[:, cols])
        up = _dot(h, win_ref[:, D_FF + c * FFN_CHUNK:D_FF + (c + 1) * FFN_CHUNK])
        a = (gate * jax.nn.sigmoid(gate) * up).astype(BF16)
        acc_ref[...] += _dot(a, wo_ref[cols, :])
    y = x + 0.5 * acc_ref[...]
    if final:
        y = _rms(y, fg_ref[...])
    o_ref[...] = y


def _ffn(x2, g, w_in, w_out, final_g=None):
    n, d = x2.shape
    win = w_in.astype(BF16)
    wo = w_out.astype(BF16)
    row = pl.BlockSpec((FFN_ROWS, d), lambda i: (i, 0))
    ins = [x2, g.reshape(1, d), win, wo]
    specs = [row, _resident((1, d)), _resident(win.shape), _resident(wo.shape)]
    if final_g is not None:
        ins.append(final_g.reshape(1, d))
        specs.append(_resident((1, d)))
    return pl.pallas_call(
        functools.partial(_ffn_kernel, final=final_g is not None),
        grid=(n // FFN_ROWS,),
        in_specs=specs,
        out_specs=row,
        out_shape=jax.ShapeDtypeStruct((n, d), F32),
        scratch_shapes=[pltpu.VMEM((FFN_ROWS, d), F32)],
        compiler_params=_params(1),
        name="ffn_final" if final_g is not None else "ffn",
    )(*ins)


def _ret_kernel(x_ref, g_ref, win_ref, wout_ref, cos_ref, sin_ref, inner_ref, qdec_ref, kdec_ref,
                cdec_ref, o_ref, q_s, k_s, v_s, gate_s, z_s, state_s):
    H, dk, dv, C = RET_HEADS, RET_DK, RET_DV, RET_CHUNK
    half = dk // 2

    @pl.when(pl.program_id(1) == 0)
    def _():
        state_s[...] = jnp.zeros_like(state_s)

    x = x_ref[...]
    h = _rms(x, g_ref[...]).astype(BF16)
    cos = cos_ref[...]
    sin = sin_ref[...]
    q = _dot(h, win_ref[:, 0:H * dk])
    k = _dot(h, win_ref[:, H * dk:2 * H * dk])
    for hh in range(H):
        q0 = q[:, hh * dk:hh * dk + half]
        q1 = q[:, hh * dk + half:(hh + 1) * dk]
        q_s[:, hh * dk:hh * dk + half] = (q0 * cos - q1 * sin).astype(BF16)
        q_s[:, hh * dk + half:(hh + 1) * dk] = (q1 * cos + q0 * sin).astype(BF16)
        k0 = k[:, hh * dk:hh * dk + half]
        k1 = k[:, hh * dk + half:(hh + 1) * dk]
        k_s[:, hh * dk:hh * dk + half] = (k0 * cos - k1 * sin) * (dk ** -0.5)
        k_s[:, hh * dk + half:(hh + 1) * dk] = (k1 * cos + k0 * sin) * (dk ** -0.5)
    v_s[...] = _dot(h, win_ref[:, 2 * H * dk:2 * H * dk + H * dv]).astype(BF16)
    gate_s[...] = _dot(h, win_ref[:, 2 * H * dk + H * dv:])

    def chunk(c, carry):
        r0 = pl.multiple_of(c * C, C)
        rows = pl.ds(r0, C)
        for hh in range(H):
            qc = q_s[rows, hh * dk:(hh + 1) * dk]
            kc = k_s[rows, hh * dk:(hh + 1) * dk]
            vc = v_s[rows, hh * dv:(hh + 1) * dv]
            st = state_s[hh]
            scores = _dot_nt(qc, kc.astype(BF16)) * inner_ref[hh]
            inner = _dot(scores.astype(BF16), vc)
            cross = _dot(qc, st.astype(BF16)) * qdec_ref[hh]
            kd = (kc * kdec_ref[hh]).astype(BF16)
            state_s[hh] = st * cdec_ref[hh] + _dot_tn(kd, vc)
            out = inner + cross
            out = out * lax.rsqrt(jnp.mean(out * out, axis=-1, keepdims=True) + RMS_EPS)
            gt = gate_s[rows, hh * dv:(hh + 1) * dv]
            z_s[rows, hh * dv:(hh + 1) * dv] = (gt * jax.nn.sigmoid(gt) * out).astype(BF16)
        return carry

    lax.fori_loop(0, x.shape[0] // C, chunk, 0, unroll=True)
    o_ref[...] = x + _dot(z_s[...], wout_ref[...])


def _ret_tables():
    H, C = RET_HEADS, RET_CHUNK
    log_gamma = jnp.log(1.0 - 2.0 ** (-5.0 - jnp.arange(H, dtype=F32)))
    idx = jnp.arange(C, dtype=F32)
    diff = idx[:, None] - idx[None, :]
    inner = jnp.where(diff[None] >= 0,
                      jnp.exp(jnp.maximum(diff, 0.0)[None] * log_gamma[:, None, None]), 0.0)
    qdec = jnp.exp((idx[None, :] + 1.0) * log_gamma[:, None])[:, :, None]
    kdec = jnp.exp((C - 1.0 - idx[None, :]) * log_gamma[:, None])[:, :, None]
    cdec = jnp.exp(C * log_gamma)[:, None, None]
    return inner, qdec, kdec, cdec


def _rot_tables(s):
    pos = jnp.arange(s, dtype=F32)
    freq = 1.0 / (ROT_BASE ** jnp.linspace(0.0, 1.0, RET_DK // 2, dtype=F32))
    ang = pos[:, None] * freq[None, :]
    return jnp.cos(ang), jnp.sin(ang)


def _pair_split_columns(w, heads, dim):
    d = w.shape[0]
    w = w.reshape(d, heads, dim // 2, 2)
    return jnp.concatenate([w[..., 0], w[..., 1]], axis=-1).reshape(d, heads * dim)


def _retention(x, g, w_in, w_out):
    b, s, d = x.shape
    H, dk, dv = RET_HEADS, RET_DK, RET_DV
    tb = min(RET_ROWS, s)
    wq = _pair_split_columns(w_in[:, :H * dk], H, dk)
    wk = _pair_split_columns(w_in[:, H * dk:2 * H * dk], H, dk)
    win = jnp.concatenate([wq, wk, w_in[:, 2 * H * dk:]], axis=1).astype(BF16)
    wout = w_out.astype(BF16)
    cos, sin = _rot_tables(s)
    inner, qdec, kdec, cdec = _ret_tables()
    row = pl.BlockSpec((None, tb, d), lambda bi, i: (bi, i, 0))
    rot = pl.BlockSpec((tb, dk // 2), lambda bi, i: (i, 0))
    return pl.pallas_call(
        _ret_kernel,
        grid=(b, s // tb),
        in_specs=[row, _resident((1, d)), _resident(win.shape), _resident(wout.shape), rot, rot,
                  _resident(inner.shape), _resident(qdec.shape), _resident(kdec.shape),
                  _resident(cdec.shape)],
        out_specs=row,
        out_shape=jax.ShapeDtypeStruct((b, s, d), F32),
        scratch_shapes=[
            pltpu.VMEM((tb, H * dk), BF16),
            pltpu.VMEM((tb, H * dk), F32),
            pltpu.VMEM((tb, H * dv), BF16),
            pltpu.VMEM((tb, H * dv), F32),
            pltpu.VMEM((tb, H * dv), BF16),
            pltpu.VMEM((H, dk, dv), F32),
        ],
        compiler_params=_params(2),
        name="retention",
    )(x, g.reshape(1, d), win, wout, cos, sin, inner, qdec, kdec, cdec)


def _dsa_proj_kernel(x_ref, g_ref, win_ref, winw_ref, qn_ref, kvn_ref, kg_ref, kb_ref, wqi_ref,
                     wuq_ref, wuk_ref, qi_ref, wt_ref, ql_ref, kidx_ref, ckv_ref, ckvt_ref):
    RQ, RKV, HI, H, dh, TQ = DSA_Q_RANK, DSA_KV_RANK, IDX_HEADS, DSA_HEADS, DSA_HEAD_DIM, ATT_Q
    n_blk = x_ref.shape[0] // TQ
    h = _rms(x_ref[...], g_ref[...]).astype(BF16)
    proj = _dot(h, win_ref[...])
    c_q = _rms(proj[:, :RQ], qn_ref[...])
    c_kv = _rms(proj[:, RQ:RQ + RKV], kvn_ref[...])
    kx = proj[:, RQ + RKV:RQ + RKV + LANES]
    lane = lax.broadcasted_iota(jnp.int32, kx.shape, 1)
    mu = jnp.sum(kx, axis=-1, keepdims=True) * (1.0 / IDX_DIM)
    cen = jnp.where(lane < IDX_DIM, kx - mu, 0.0)
    var = jnp.sum(cen * cen, axis=-1, keepdims=True) * (1.0 / IDX_DIM)
    kidx_ref[...] = (cen * lax.rsqrt(var + RMS_EPS) * kg_ref[...] + kb_ref[...]).astype(BF16)
    ckv_ref[...] = c_kv.astype(BF16)
    ckvt_ref[0:RKV, :] = c_kv.T.astype(BF16)
    ones_row = lax.broadcasted_iota(jnp.int32, (BF16_ROWS, x_ref.shape[0]), 0) == 0
    ckvt_ref[RKV:, :] = jnp.where(ones_row, 1.0, 0.0).astype(BF16)
    w_t = _dot_nt(winw_ref[...], h) * (HI ** -0.5)
    cq = c_q.astype(BF16)
    qi_t = (_dot_nt(wqi_ref[...], cq) * (IDX_DIM ** -0.5)).astype(BF16)
    q = _dot(cq, wuq_ref[...]).astype(BF16)
    ql_t = [_dot_nt(wuk_ref[hh], q[:, hh * dh:(hh + 1) * dh]).astype(BF16) for hh in range(H)]
    for u in range(n_blk):
        cols = slice(u * TQ, (u + 1) * TQ)
        wt_ref[u] = w_t[0:HI, cols]
        for hh in range(HI):
            qi_ref[u, :, hh * TQ:(hh + 1) * TQ] = qi_t[hh * LANES:(hh + 1) * LANES, cols]
        for hh in range(H):
            ql_ref[u, :, hh * TQ:(hh + 1) * TQ] = ql_t[hh][:, cols]


def _dsa_proj(x2, g, w_in, q_norm, kv_norm, w_uq, w_qidx, kidx_g, kidx_b, w_uk):
    n, d = x2.shape
    RQ, RKV, HI, DI, H, TQ = DSA_Q_RANK, DSA_KV_RANK, IDX_HEADS, IDX_DIM, DSA_HEADS, ATT_Q
    pad_k = jnp.zeros((d, LANES - DI), F32)
    win = jnp.concatenate([w_in[:, :RQ + RKV + DI], pad_k], axis=1).astype(BF16)
    winw = jnp.pad(w_in[:, RQ + RKV + DI:].T, ((0, BF16_ROWS - HI), (0, 0))).astype(BF16)
    kg = jnp.pad(kidx_g, (0, LANES - DI)).reshape(1, LANES)
    kb = jnp.pad(kidx_b, (0, LANES - DI)).reshape(1, LANES)
    wqi = jnp.pad(w_qidx.reshape(RQ, HI, DI), ((0, 0), (0, 0), (0, LANES - DI)))
    wqi = wqi.reshape(RQ, HI * LANES).T.astype(BF16)
    wuq = w_uq.astype(BF16)
    wuk = w_uk.transpose(1, 0, 2).astype(BF16)
    tm = min(PROJ_ROWS, n)
    nb = tm // TQ
    row = lambda w: pl.BlockSpec((tm, w), lambda i: (i, 0))
    blk = lambda r, c: pl.BlockSpec((nb, r, c), lambda i: (i, 0, 0))
    return pl.pallas_call(
        _dsa_proj_kernel,
        grid=(n // tm,),
        in_specs=[row(d), _resident((1, d)), _resident(win.shape), _resident(winw.shape),
                  _resident((1, RQ)), _resident((1, RKV)), _resident((1, LANES)),
                  _resident((1, LANES)), _resident(wqi.shape), _resident(wuq.shape),
                  _resident(wuk.shape)],
        out_specs=[blk(LANES, HI * TQ), blk(HI, TQ), blk(RKV, H * TQ), row(LANES), row(RKV),
                   pl.BlockSpec((KV_ROWS, tm), lambda i: (0, i))],
        out_shape=[
            jax.ShapeDtypeStruct((n // TQ, LANES, HI * TQ), BF16),
            jax.ShapeDtypeStruct((n // TQ, HI, TQ), F32),
            jax.ShapeDtypeStruct((n // TQ, RKV, H * TQ), BF16),
            jax.ShapeDtypeStruct((n, LANES), BF16),
            jax.ShapeDtypeStruct((n, RKV), BF16),
            jax.ShapeDtypeStruct((KV_ROWS, n), BF16),
        ],
        compiler_params=_params(1),
        name="dsa_proj",
    )(x2, g.reshape(1, d), win, winw, q_norm.reshape(1, RQ), kv_norm.reshape(1, RKV), kg, kb, wqi,
      wuq, wuk)


def _fold(t, op):
    groups = t.shape[0] // SUBLANES
    group = lambda r: t[r * SUBLANES:(r + 1) * SUBLANES, :]
    acc = [group(w) for w in range(REDUCE_WAYS)]
    for r in range(REDUCE_WAYS, groups, REDUCE_WAYS):
        acc = [op(acc[w], group(r + w)) for w in range(REDUCE_WAYS)]
    while len(acc) > 1:
        acc = [op(acc[2 * w], acc[2 * w + 1]) for w in range(len(acc) // 2)]
    return acc[0]


def _fori_grouped(n, body, init, group=LOOP_GROUP):
    def grouped(q, carry):
        for u in range(group):
            carry = body(group * q + u, carry)
        return carry
    carry = lax.fori_loop(0, n // group, grouped, init)
    return lax.fori_loop(group * (n // group), n, body, carry)


def _dsa_attn_kernel(x_ref, qi_ref, wt_ref, ql_ref, kidx_ref, ckv_ref, ckvt_ref, wuvt_ref,
                     wout_ref, cmax_ref, tri_ref, o_ref, sc_s, lg_a, lg_b, p_a, p_b, bias_s, acc_s, m_s, *, top_k):
    TQ, TK, TH = ATT_Q, ATT_K, ATT_K // 2
    HI, H, RKV = IDX_HEADS, DSA_HEADS, DSA_KV_RANK
    i = pl.program_id(1)
    n_tiles = (i * TQ + TQ + TK - 1) // TK
    qpos = i * TQ + lax.broadcasted_iota(jnp.int32, (1, TQ), 1)
    key0 = lax.broadcasted_iota(jnp.int32, (TK, 1), 0)
    keyh = lax.broadcasted_iota(jnp.int32, (TH, 1), 0)
    neg_inf = jnp.float32(-jnp.inf)
    kf = jnp.float32(top_k)

    def tile_rows(j):
        return pl.ds(pl.multiple_of(j * TK, TK), TK)

    w_t = wt_ref[...]
    last_half = 2 * n_tiles - 1

    def half_rows(jh):
        return pl.ds(pl.multiple_of(jh * TH, TH), TH)

    def score_dots(jh, buf):
        buf[...] = _dot(kidx_ref[half_rows(jh), :], qi_ref[...])

    def score_reduce(jh, buf, carry):
        rmax, rmin, c_ge0, c_gt0 = carry
        s = jnp.maximum(buf[:, 0:TQ], 0.0) * w_t[0:1, :]
        for hh in range(1, HI):
            s = s + jnp.maximum(buf[:, hh * TQ:(hh + 1) * TQ], 0.0) * w_t[hh:hh + 1, :]
        causal = keyh + jh * TH <= qpos
        s = jnp.where(causal, s, neg_inf)
        sc_s[half_rows(jh), :] = s
        rmax = jnp.maximum(rmax, _fold(s, jnp.maximum))
        rmin = jnp.minimum(rmin, _fold(jnp.where(causal, s, jnp.inf), jnp.minimum))
        c_ge0 = c_ge0 + _fold(jnp.where(s >= 0.0, 1.0, 0.0), jnp.add)
        c_gt0 = c_gt0 + _fold(jnp.where(s > 0.0, 1.0, 0.0), jnp.add)
        return rmax, rmin, c_ge0, c_gt0

    def score_pair(j, carry):
        score_dots(2 * j + 1, lg_b)
        carry = score_reduce(2 * j, lg_a, carry)
        score_dots(jnp.minimum(2 * j + 2, last_half), lg_a)
        return score_reduce(2 * j + 1, lg_b, carry)

    part = lambda v: jnp.full((SUBLANES, TQ), v, F32)
    score_dots(0, lg_a)
    rmax, rmin, c_ge0, c_gt0 = _fori_grouped(
        n_tiles, score_pair, (part(-jnp.inf), part(jnp.inf), part(0.0), part(0.0)))
    rmax = jnp.max(rmax, axis=0, keepdims=True)
    rmin = jnp.min(rmin, axis=0, keepdims=True)
    c_ge0 = jnp.sum(c_ge0, axis=0, keepdims=True)
    c_gt0 = jnp.sum(c_gt0, axis=0, keepdims=True)

    def count(pred):
        def body(j, acc):
            return acc + _fold(jnp.where(pred(sc_s[tile_rows(j), :], j), 1.0, 0.0), jnp.add)
        return jnp.sum(_fori_grouped(n_tiles, body, part(0.0)), axis=0, keepdims=True)

    n_valid = (qpos + 1).astype(F32)
    all_sel = n_valid <= kf
    zero_tie = (c_gt0 < kf) & (c_ge0 >= kf)
    positive = c_gt0 >= kf
    lo = jnp.where(positive | zero_tie, 0.0, rmin)
    hi = jnp.where(positive, 2.0 * rmax, 0.0)
    clo = jnp.where(positive | zero_tie, c_ge0, n_valid)
    chi = jnp.where(positive, 0.0, c_ge0)
    lo = jnp.where(all_sel, jnp.float32(jnp.finfo(jnp.float32).min), lo)
    clo = jnp.where(all_sel, n_valid, clo)
    done = all_sel | zero_tie | (clo == kf)
    one = jnp.ones((1, TQ), F32)
    log_target = math.log(top_k + 0.5)

    def pending(done):
        return jnp.sum(jnp.where(done, 0, 1))

    def search_cond(c):
        return (c[1] > 0) & (c[0] < SEARCH_CAP)

    def search_step(state):
        lo, hi, clo, chi, wl, wh, side, done_f = state
        done = done_f > 0.0
        mid = 0.5 * lo + 0.5 * hi
        fa = (jnp.log(clo) - log_target) * wl
        fb = (log_target - jnp.log(jnp.maximum(chi, 0.5))) * wh
        cand = lo + (hi - lo) * (fa / (fa + fb))
        cand = jnp.where((cand > lo) & (cand < hi), cand, mid)
        collapsed = (cand <= lo) | (cand >= hi)
        cm = count(lambda t, j: t >= cand)
        move = jnp.logical_not(done | collapsed)
        up = move & (cm >= kf)
        down = move & (cm < kf)
        wh = jnp.where(up, jnp.where(side > 0.0, 0.5 * wh, 1.0), jnp.where(down, 1.0, wh))
        wl = jnp.where(down, jnp.where(side < 0.0, 0.5 * wl, 1.0), jnp.where(up, 1.0, wl))
        side = jnp.where(up, 1.0, jnp.where(down, -1.0, side))
        lo = jnp.where(up, cand, lo)
        clo = jnp.where(up, cm, clo)
        hi = jnp.where(down, cand, hi)
        chi = jnp.where(down, cm, chi)
        done = done | collapsed | (clo == kf)
        return lo, hi, clo, chi, wl, wh, side, jnp.where(done, 1.0, 0.0)

    def search_body(c):
        state = search_step(c[2:])
        return (c[0] + 1, pending(state[-1] > 0.0)) + state

    state = (lo, hi, clo, chi, one, one, 0.0 * one, jnp.where(done, 1.0, 0.0))
    state = lax.fori_loop(0, SEARCH_UNCHECKED, lambda _, s: search_step(s), state)

    lo, hi, clo, chi, wl, wh, side, done_f = state
    inf = jnp.float32(jnp.inf)

    def insert(ranks, v):
        out = []
        for r in ranks:
            out.append(jnp.minimum(r, v))
            v = jnp.maximum(r, v)
        return out

    def smallest_body(j, ranks):
        t = sc_s[tile_rows(j), :]
        ranks = list(ranks)
        for g in range(TK // SUBLANES):
            v = t[g * SUBLANES:(g + 1) * SUBLANES, :]
            w = g % 2
            ranks[w] = tuple(insert(ranks[w], jnp.where(v >= lo, v, inf)))
        return tuple(ranks)

    empty = tuple(jnp.full((SUBLANES, TQ), inf, F32) for _ in range(FINISH_RANKS))
    ranks = lax.fori_loop(0, n_tiles, smallest_body, (empty, empty))
    final = [jnp.full((1, TQ), inf, F32) for _ in range(FINISH_RANKS)]
    for chain in ranks:
        for r in chain:
            for sub in range(SUBLANES):
                final = insert(final, r[sub:sub + 1, :])
    extra = clo - kf
    kth = final[0]
    below = -inf
    for e in range(1, FINISH_RANKS):
        kth = jnp.where(extra >= e, final[e], kth)
        below = jnp.where(extra == e, final[e - 1], below)
    near = (done_f <= 0.0) & (extra < FINISH_RANKS)
    lo = jnp.where(near, kth, lo)
    clo = jnp.where(near, jnp.where(below == kth, kf + 1.0, kf), clo)
    done_f = jnp.where(near, 1.0, done_f)
    state = (lo, hi, clo, chi, wl, wh, side, done_f)

    res = lax.while_loop(search_cond, search_body,
                         (jnp.int32(0), pending(state[-1] > 0.0)) + state)
    tau, clo = res[2], res[4]

    excess = clo > kf

    @pl.when(jnp.sum(jnp.where(excess, 1, 0)) > 0)
    def _():
        need = kf - count(lambda t, j: t > tau)

        def rank_and_drop(j, seen):
            t = sc_s[tile_rows(j), :]
            tied = t == tau
            rank = seen + _dot(tri_ref[...], jnp.where(tied, 1.0, 0.0).astype(BF16))
            sc_s[tile_rows(j), :] = jnp.where(excess & tied & (rank > need), neg_inf, t)
            return rank[TK - 1:TK, :]

        lax.fori_loop(0, n_tiles, rank_and_drop, jnp.zeros((1, TQ), F32))

    c_exp = (DSA_HEAD_DIM ** -0.5) * math.log2(math.e)
    def attn_logits(jh, buf):
        buf[...] = _dot(ckv_ref[half_rows(jh), :], ql_ref[...])

    ql_f = ql_ref[...].astype(F32)
    bound = jnp.sqrt(jnp.sum(ql_f * ql_f, axis=0, keepdims=True)) * cmax_ref[...]

    def fast_values(jh, p_buf):
        cols_j = pl.ds(pl.multiple_of(jh * TH, TH), TH)
        acc_s[...] += _dot(ckvt_ref[:, cols_j], p_buf[...])

    def fast_half(jh, buf, p_buf, jh_next, buf_next, jh_prev, p_prev):
        keep = jnp.where(sc_s[half_rows(jh), :] >= tau, 1.0, 0.0).astype(BF16)
        keys_next = ckv_ref[half_rows(jh_next), :]
        vals_prev = ckvt_ref[:, pl.ds(pl.multiple_of(jh_prev * TH, TH), TH)]
        per_group = H // COLUMN_GROUPS
        for grp in range(COLUMN_GROUPS):
            gcols = slice(grp * per_group * TQ, (grp + 1) * per_group * TQ)
            buf_next[:, gcols] = _dot(keys_next, ql_ref[:, gcols])
            for hh in range(grp * per_group, (grp + 1) * per_group):
                cols = slice(hh * TQ, (hh + 1) * TQ)
                p = jnp.exp2((buf[:, cols] - bound[:, cols]) * c_exp).astype(BF16)
                p_buf[:, cols] = p * keep
            acc_s[:, gcols] += _dot(vals_prev, p_prev[:, gcols])

    def fast_pair(j, carry):
        fast_half(2 * j, lg_a, p_a, 2 * j + 1, lg_b, jnp.maximum(2 * j - 1, 0), p_b)
        fast_half(2 * j + 1, lg_b, p_b, jnp.minimum(2 * j + 2, last_half), lg_a, 2 * j, p_a)
        return carry

    acc_s[...] = jnp.zeros_like(acc_s)
    p_b[...] = jnp.zeros_like(p_b)
    attn_logits(0, lg_a)
    _fori_grouped(n_tiles, fast_pair, 0)
    fast_values(last_half, p_b)

    def write_output():
        o_lat = (acc_s[0:RKV, :] / acc_s[RKV:RKV + 1, :]).astype(BF16)
        o_t = jnp.concatenate(
            [_dot(wuvt_ref[hh], o_lat[:, hh * TQ:(hh + 1) * TQ]) for hh in range(H)], axis=0)
        o_ref[...] = x_ref[...] + _dot(o_t.T.astype(BF16), wout_ref[...])

    write_output()
    denom_bad = jnp.logical_not(acc_s[RKV:RKV + 1, :] >= DENOM_FLOOR)
    redo = jnp.sum(jnp.where(denom_bad, 1, 0)) > 0

    def attn_softmax(jh, buf, p_buf):
        bias_s[...] = jnp.where(sc_s[half_rows(jh), :] >= tau, 0.0, MASKED)
        alphas = []
        for hh in range(H):
            cols = slice(hh * TQ, (hh + 1) * TQ)
            m_old = m_s[hh:hh + 1, :]
            mx = _fold(buf[:, cols] + bias_s[...], jnp.maximum)
            m_new = jnp.maximum(m_old, jnp.max(mx, axis=0, keepdims=True))
            m_s[hh:hh + 1, :] = m_new
            alphas.append(jnp.exp2((m_old - m_new) * c_exp))
            p_buf[:, cols] = jnp.exp2((buf[:, cols] + bias_s[...] - m_new) * c_exp).astype(BF16)
        return jnp.concatenate(alphas, axis=1)

    def attn_values(jh, p_buf, alpha):
        cols_j = pl.ds(pl.multiple_of(jh * TH, TH), TH)
        acc_s[...] = acc_s[...] * alpha + _dot(ckvt_ref[:, cols_j], p_buf[...])

    def attn_pair(j, alpha_b):
        attn_logits(2 * j + 1, lg_b)
        alpha_a = attn_softmax(2 * j, lg_a, p_a)
        attn_values(jnp.maximum(2 * j - 1, 0), p_b, alpha_b)
        attn_logits(jnp.minimum(2 * j + 2, last_half), lg_a)
        alpha_b = attn_softmax(2 * j + 1, lg_b, p_b)
        attn_values(2 * j, p_a, alpha_a)
        return alpha_b

    @pl.when(redo)
    def _():
        m_s[...] = jnp.full_like(m_s, MASKED)
        acc_s[...] = jnp.zeros_like(acc_s)
        p_b[...] = jnp.zeros_like(p_b)
        attn_logits(0, lg_a)
        alpha_b = lax.fori_loop(0, n_tiles, attn_pair, jnp.ones((1, H * TQ), F32))
        attn_values(last_half, p_b, alpha_b)
        write_output()


def _dsa_attn(x, qi, wt, ql, kidx, ckv, ckvt, kv_norm, w_uv, w_out):
    b, s, d = x.shape
    H, RKV, HI, TQ = DSA_HEADS, DSA_KV_RANK, IDX_HEADS, ATT_Q
    nq = s // TQ
    top_k = min(TOPK_MAX, s // 4)
    wuvt = w_uv.transpose(1, 2, 0).astype(BF16)
    wout = w_out.astype(BF16)
    cmax = (math.sqrt(RKV) * jnp.max(jnp.abs(kv_norm))).reshape(1, 1).astype(F32)
    tri = jnp.tril(jnp.ones((ATT_K, ATT_K), F32)).astype(BF16)
    qrow = pl.BlockSpec((None, TQ, d), lambda bi, i: (bi, i, 0))
    qblk = lambda r, c: pl.BlockSpec((None, r, c), lambda bi, i: (bi * nq + i, 0, 0))
    seq = lambda width: pl.BlockSpec((s, width), lambda bi, i: (bi, 0))
    return pl.pallas_call(
        functools.partial(_dsa_attn_kernel, top_k=top_k),
        grid=(b, nq),
        in_specs=[qrow, qblk(LANES, HI * TQ), qblk(HI, TQ), qblk(RKV, H * TQ), seq(LANES), seq(RKV),
                  pl.BlockSpec((KV_ROWS, s), lambda bi, i: (0, bi)),
                  _resident(wuvt.shape), _resident(wout.shape), _resident((1, 1)),
                  _resident(tri.shape)],
        out_specs=qrow,
        out_shape=jax.ShapeDtypeStruct((b, s, d), F32),
        scratch_shapes=[
            pltpu.VMEM((s, TQ), F32),
            pltpu.VMEM((ATT_K // 2, H * TQ), F32),
            pltpu.VMEM((ATT_K // 2, H * TQ), F32),
            pltpu.VMEM((ATT_K // 2, H * TQ), BF16),
            pltpu.VMEM((ATT_K // 2, H * TQ), BF16),
            pltpu.VMEM((ATT_K // 2, TQ), F32),
            pltpu.VMEM((KV_ROWS, H * TQ), F32),
            pltpu.VMEM((H, TQ), F32),
        ],
        compiler_params=_params(2),
        name="dsa_attn",
    )(x, qi, wt, ql, kidx, ckv, ckvt, wuvt, wout, cmax, tri)


def _dsa(x, g, w_in, q_norm, kv_norm, w_uq, w_qidx, kidx_g, kidx_b, w_uk, w_uv, w_out):
    b, s, d = x.shape
    qi, wt, ql, kidx, ckv, ckvt = _dsa_proj(x.reshape(b * s, d), g, w_in, q_norm, kv_norm, w_uq,
                                            w_qidx, kidx_g, kidx_b, w_uk)
    return _dsa_attn(x, qi, wt, ql, kidx, ckv, ckvt, kv_norm, w_uv, w_out)


def kernel(x, norm_ffn1, w_ffn1_in, w_ffn1_out, norm_mix, norm_ffn2, w_ffn2_in, w_ffn2_out, ret_w_in,
           ret_w_out, dsa_w_in, dsa_q_norm, dsa_kv_norm, dsa_w_uq, dsa_w_qidx, dsa_kidx_g, dsa_kidx_b,
           dsa_w_uk, dsa_w_uv, dsa_w_out, final_norm):
    b, s, d = x.shape
    depth = norm_ffn1.shape[0]
    for layer in range(depth):
        x = _ffn(x.reshape(b * s, d), norm_ffn1[layer], w_ffn1_in[layer], w_ffn1_out[layer])
        x = x.reshape(b, s, d)
        j = layer // 2
        if layer % 2 == 0:
            x = _retention(x, norm_mix[layer], ret_w_in[j], ret_w_out[j])
        else:
            x = _dsa(x, norm_mix[layer], dsa_w_in[j], dsa_q_norm[j], dsa_kv_norm[j], dsa_w_uq[j],
                     dsa_w_qidx[j], dsa_kidx_g[j], dsa_kidx_b[j], dsa_w_uk[j], dsa_w_uv[j],
                     dsa_w_out[j])
        last = layer == depth - 1
        x = _ffn(x.reshape(b * s, d), norm_ffn2[layer], w_ffn2_in[layer], w_ffn2_out[layer],
                 final_g=final_norm if last else None)
        x = x.reshape(b, s, d)
    return x
```

```python
import functools
import math

import jax
import jax.numpy as jnp
from jax import lax
from jax.experimental import pallas as pl
from jax.experimental.pallas import tpu as pltpu

F32 = jnp.float32
BF16 = jnp.bfloat16

D_MODEL = 1024
D_FF = 2816
RMS_EPS = 1e-6

RET_HEADS = 4
RET_DK = D_MODEL // RET_HEADS
RET_DV = 2 * RET_DK
RET_CHUNK = 128
ROT_BASE = 10000.0

DSA_HEADS = 8
DSA_HEAD_DIM = D_MODEL // DSA_HEADS
DSA_Q_RANK = 256
DSA_KV_RANK = 256
IDX_HEADS = 8
IDX_DIM = 64
TOPK_MAX = 256

LANES = 128
SUBLANES = 8
BF16_ROWS = 16
VMEM_LIMIT_BYTES = 56 * 1024 * 1024

FFN_ROWS = 512
FFN_CHUNK = 256
RET_ROWS = 512
PROJ_ROWS = 512
ATT_Q = LANES
ATT_K = 512
KV_ROWS = DSA_KV_RANK + BF16_ROWS
MASKED = -1e30
SEARCH_CAP = 400
SEARCH_UNCHECKED = 9
FINISH_RANKS = 4
DENOM_FLOOR = 2.0 ** -80
REDUCE_WAYS = 4
LOOP_GROUP = 4
COLUMN_GROUPS = 4


def _resident(shape):
    nd = len(shape)
    return pl.BlockSpec(shape, lambda *_: (0,) * nd, pipeline_mode=pl.Buffered(1))


def _params(n_grid, flags=None):
    return pltpu.CompilerParams(
        dimension_semantics=("arbitrary",) * n_grid, vmem_limit_bytes=VMEM_LIMIT_BYTES, flags=flags)


def _rms(x, g):
    y = x * lax.rsqrt(jnp.mean(x * x, axis=-1, keepdims=True) + RMS_EPS)
    return y * g


def _dot(a, b):
    return jnp.dot(a, b, preferred_element_type=F32)


def _dot_nt(a, b):
    return lax.dot_general(a, b, (((1,), (1,)), ((), ())), preferred_element_type=F32)


def _dot_tn(a, b):
    return lax.dot_general(a, b, (((0,), (0,)), ((), ())), preferred_element_type=F32)


def _ffn_kernel(x_ref, g_ref, win_ref, wo_ref, *rest, final):
    if final:
        fg_ref, o_ref, acc_ref = rest
    else:
        o_ref, acc_ref = rest
    x = x_ref[...]
    h = _rms(x, g_ref[...]).astype(BF16)
    acc_ref[...] = jnp.zeros_like(acc_ref)
    for c in range(D_FF // FFN_CHUNK):
        cols = slice(c * FFN_CHUNK, (c + 1) * FFN_CHUNK)
        gate = _dot(h, win_ref[:, cols])
        up = _dot(h, win_ref[:, D_FF + c * FFN_CHUNK:D_FF + (c + 1) * FFN_CHUNK])
        a = (gate * jax.nn.sigmoid(gate) * up).astype(BF16)
        acc_ref[...] += _dot(a, wo_ref[cols, :])
    y = x + 0.5 * acc_ref[...]
    if final:
        y = _rms(y, fg_ref[...])
    o_ref[...] = y


def _ffn(x2, g, w_in, w_out, final_g=None):
    n, d = x2.shape
    win = w_in.astype(BF16)
    wo = w_out.astype(BF16)
    row = pl.BlockSpec((FFN_ROWS, d), lambda i: (i, 0))
    ins = [x2, g.reshape(1, d), win, wo]
    specs = [row, _resident((1, d)), _resident(win.shape), _resident(wo.shape)]
    if final_g is not None:
        ins.append(final_g.reshape(1, d))
        specs.append(_resident((1, d)))
    return pl.pallas_call(
        functools.partial(_ffn_kernel, final=final_g is not None),
        grid=(n // FFN_ROWS,),
        in_specs=specs,
        out_specs=row,
        out_shape=jax.ShapeDtypeStruct((n, d), F32),
        scratch_shapes=[pltpu.VMEM((FFN_ROWS, d), F32)],
        compiler_params=_params(1),
        name="ffn_final" if final_g is not None else "ffn",
    )(*ins)


def _ret_kernel(x_ref, g_ref, win_ref, wout_ref, cos_ref, sin_ref, inner_ref, qdec_ref, kdec_ref,
                cdec_ref, o_ref, q_s, k_s, v_s, gate_s, z_s, state_s):
    H, dk, dv, C = RET_HEADS, RET_DK, RET_DV, RET_CHUNK
    half = dk // 2

    @pl.when(pl.program_id(1) == 0)
    def _():
        state_s[...] = jnp.zeros_like(state_s)

    x = x_ref[...]
    h = _rms(x, g_ref[...]).astype(BF16)
    cos = cos_ref[...]
    sin = sin_ref[...]
    q = _dot(h, win_ref[:, 0:H * dk])
    k = _dot(h, win_ref[:, H * dk:2 * H * dk])
    for hh in range(H):
        q0 = q[:, hh * dk:hh * dk + half]
        q1 = q[:, hh * dk + half:(hh + 1) * dk]
        q_s[:, hh * dk:hh * dk + half] = (q0 * cos - q1 * sin).astype(BF16)
        q_s[:, hh * dk + half:(hh + 1) * dk] = (q1 * cos + q0 * sin).astype(BF16)
        k0 = k[:, hh * dk:hh * dk + half]
        k1 = k[:, hh * dk + half:(hh + 1) * dk]
        k_s[:, hh * dk:hh * dk + half] = (k0 * cos - k1 * sin) * (dk ** -0.5)
        k_s[:, hh * dk + half:(hh + 1) * dk] = (k1 * cos + k0 * sin) * (dk ** -0.5)
    v_s[...] = _dot(h, win_ref[:, 2 * H * dk:2 * H * dk + H * dv]).astype(BF16)
    gate_s[...] = _dot(h, win_ref[:, 2 * H * dk + H * dv:])

    def chunk(c, carry):
        r0 = pl.multiple_of(c * C, C)
        rows = pl.ds(r0, C)
        for hh in range(H):
            qc = q_s[rows, hh * dk:(hh + 1) * dk]
            kc = k_s[rows, hh * dk:(hh + 1) * dk]
            vc = v_s[rows, hh * dv:(hh + 1) * dv]
            st = state_s[hh]
            scores = _dot_nt(qc, kc.astype(BF16)) * inner_ref[hh]
            inner = _dot(scores.astype(BF16), vc)
            cross = _dot(qc, st.astype(BF16)) * qdec_ref[hh]
            kd = (kc * kdec_ref[hh]).astype(BF16)
            state_s[hh] = st * cdec_ref[hh] + _dot_tn(kd, vc)
            out = inner + cross
            out = out * lax.rsqrt(jnp.mean(out * out, axis=-1, keepdims=True) + RMS_EPS)
            gt = gate_s[rows, hh * dv:(hh + 1) * dv]
            z_s[rows, hh * dv:(hh + 1) * dv] = (gt * jax.nn.sigmoid(gt) * out).astype(BF16)
        return carry

    lax.fori_loop(0, x.shape[0] // C, chunk, 0, unroll=True)
    o_ref[...] = x + _dot(z_s[...], wout_ref[...])


def _ret_tables():
    H, C = RET_HEADS, RET_CHUNK
    log_gamma = jnp.log(1.0 - 2.0 ** (-5.0 - jnp.arange(H, dtype=F32)))
    idx = jnp.arange(C, dtype=F32)
    diff = idx[:, None] - idx[None, :]
    inner = jnp.where(diff[None] >= 0,
                      jnp.exp(jnp.maximum(diff, 0.0)[None] * log_gamma[:, None, None]), 0.0)
    qdec = jnp.exp((idx[None, :] + 1.0) * log_gamma[:, None])[:, :, None]
    kdec = jnp.exp((C - 1.0 - idx[None, :]) * log_gamma[:, None])[:, :, None]
    cdec = jnp.exp(C * log_gamma)[:, None, None]
    return inner, qdec, kdec, cdec


def _rot_tables(s):
    pos = jnp.arange(s, dtype=F32)
    freq = 1.0 / (ROT_BASE ** jnp.linspace(0.0, 1.0, RET_DK // 2, dtype=F32))
    ang = pos[:, None] * freq[None, :]
    return jnp.cos(ang), jnp.sin(ang)


def _pair_split_columns(w, heads, dim):
    d = w.shape[0]
    w = w.reshape(d, heads, dim // 2, 2)
    return jnp.concatenate([w[..., 0], w[..., 1]], axis=-1).reshape(d, heads * dim)


def _retention(x, g, w_in, w_out):
    b, s, d = x.shape
    H, dk, dv = RET_HEADS, RET_DK, RET_DV
    tb = min(RET_ROWS, s)
    wq = _pair_split_columns(w_in[:, :H * dk], H, dk)
    wk = _pair_split_columns(w_in[:, H * dk:2 * H * dk], H, dk)
    win = jnp.concatenate([wq, wk, w_in[:, 2 * H * dk:]], axis=1).astype(BF16)
    wout = w_out.astype(BF16)
    cos, sin = _rot_tables(s)
    inner, qdec, kdec, cdec = _ret_tables()
    row = pl.BlockSpec((None, tb, d), lambda bi, i: (bi, i, 0))
    rot = pl.BlockSpec((tb, dk // 2), lambda bi, i: (i, 0))
    return pl.pallas_call(
        _ret_kernel,
        grid=(b, s // tb),
        in_specs=[row, _resident((1, d)), _resident(win.shape), _resident(wout.shape), rot, rot,
                  _resident(inner.shape), _resident(qdec.shape), _resident(kdec.shape),
                  _resident(cdec.shape)],
        out_specs=row,
        out_shape=jax.ShapeDtypeStruct((b, s, d), F32),
        scratch_shapes=[
            pltpu.VMEM((tb, H * dk), BF16),
            pltpu.VMEM((tb, H * dk), F32),
            pltpu.VMEM((tb, H * dv), BF16),
            pltpu.VMEM((tb, H * dv), F32),
            pltpu.VMEM((tb, H * dv), BF16),
            pltpu.VMEM((H, dk, dv), F32),
        ],
        compiler_params=_params(2),
        name="retention",
    )(x, g.reshape(1, d), win, wout, cos, sin, inner, qdec, kdec, cdec)


def _dsa_proj_kernel(x_ref, g_ref, win_ref, winw_ref, qn_ref, kvn_ref, kg_ref, kb_ref, wqi_ref,
                     wuq_ref, wuk_ref, qi_ref, wt_ref, ql_ref, kidx_ref, ckv_ref, ckvt_ref):
    RQ, RKV, HI, H, dh, TQ = DSA_Q_RANK, DSA_KV_RANK, IDX_HEADS, DSA_HEADS, DSA_HEAD_DIM, ATT_Q
    n_blk = x_ref.shape[0] // TQ
    h = _rms(x_ref[...], g_ref[...]).astype(BF16)
    proj = _dot(h, win_ref[...])
    c_q = _rms(proj[:, :RQ], qn_ref[...])
    c_kv = _rms(proj[:, RQ:RQ + RKV], kvn_ref[...])
    kx = proj[:, RQ + RKV:RQ + RKV + LANES]
    lane = lax.broadcasted_iota(jnp.int32, kx.shape, 1)
    mu = jnp.sum(kx, axis=-1, keepdims=True) * (1.0 / IDX_DIM)
    cen = jnp.where(lane < IDX_DIM, kx - mu, 0.0)
    var = jnp.sum(cen * cen, axis=-1, keepdims=True) * (1.0 / IDX_DIM)
    kidx_ref[...] = (cen * lax.rsqrt(var + RMS_EPS) * kg_ref[...] + kb_ref[...]).astype(BF16)
    ckv_ref[...] = c_kv.astype(BF16)
    ckvt_ref[0:RKV, :] = c_kv.T.astype(BF16)
    ones_row = lax.broadcasted_iota(jnp.int32, (BF16_ROWS, x_ref.shape[0]), 0) == 0
    ckvt_ref[RKV:, :] = jnp.where(ones_row, 1.0, 0.0).astype(BF16)
    w_t = _dot_nt(winw_ref[...], h) * (HI ** -0.5)
    cq = c_q.astype(BF16)
    qi_t = (_dot_nt(wqi_ref[...], cq) * (IDX_DIM ** -0.5)).astype(BF16)
    q = _dot(cq, wuq_ref[...]).astype(BF16)
    ql_t = [_dot_nt(wuk_ref[hh], q[:, hh * dh:(hh + 1) * dh]).astype(BF16) for hh in range(H)]
    for u in range(n_blk):
        cols = slice(u * TQ, (u + 1) * TQ)
        wt_ref[u] = w_t[0:HI, cols]
        for hh in range(HI):
            qi_ref[u, :, hh * TQ:(hh + 1) * TQ] = qi_t[hh * LANES:(hh + 1) * LANES, cols]
        for hh in range(H):
            ql_ref[u, :, hh * TQ:(hh + 1) * TQ] = ql_t[hh][:, cols]


def _dsa_proj(x2, g, w_in, q_norm, kv_norm, w_uq, w_qidx, kidx_g, kidx_b, w_uk):
    n, d = x2.shape
    RQ, RKV, HI, DI, H, TQ = DSA_Q_RANK, DSA_KV_RANK, IDX_HEADS, IDX_DIM, DSA_HEADS, ATT_Q
    pad_k = jnp.zeros((d, LANES - DI), F32)
    win = jnp.concatenate([w_in[:, :RQ + RKV + DI], pad_k], axis=1).astype(BF16)
    winw = jnp.pad(w_in[:, RQ + RKV + DI:].T, ((0, BF16_ROWS - HI), (0, 0))).astype(BF16)
    kg = jnp.pad(kidx_g, (0, LANES - DI)).reshape(1, LANES)
    kb = jnp.pad(kidx_b, (0, LANES - DI)).reshape(1, LANES)
    wqi = jnp.pad(w_qidx.reshape(RQ, HI, DI), ((0, 0), (0, 0), (0, LANES - DI)))
    wqi = wqi.reshape(RQ, HI * LANES).T.astype(BF16)
    wuq = w_uq.astype(BF16)
    wuk = w_uk.transpose(1, 0, 2).astype(BF16)
    tm = min(PROJ_ROWS, n)
    nb = tm // TQ
    row = lambda w: pl.BlockSpec((tm, w), lambda i: (i, 0))
    blk = lambda r, c: pl.BlockSpec((nb, r, c), lambda i: (i, 0, 0))
    return pl.pallas_call(
        _dsa_proj_kernel,
        grid=(n // tm,),
        in_specs=[row(d), _resident((1, d)), _resident(win.shape), _resident(winw.shape),
                  _resident((1, RQ)), _resident((1, RKV)), _resident((1, LANES)),
                  _resident((1, LANES)), _resident(wqi.shape), _resident(wuq.shape),
                  _resident(wuk.shape)],
        out_specs=[blk(LANES, HI * TQ), blk(HI, TQ), blk(RKV, H * TQ), row(LANES), row(RKV),
                   pl.BlockSpec((KV_ROWS, tm), lambda i: (0, i))],
        out_shape=[
            jax.ShapeDtypeStruct((n // TQ, LANES, HI * TQ), BF16),
            jax.ShapeDtypeStruct((n // TQ, HI, TQ), F32),
            jax.ShapeDtypeStruct((n // TQ, RKV, H * TQ), BF16),
            jax.ShapeDtypeStruct((n, LANES), BF16),
            jax.ShapeDtypeStruct((n, RKV), BF16),
            jax.ShapeDtypeStruct((KV_ROWS, n), BF16),
        ],
        compiler_params=_params(1),
        name="dsa_proj",
    )(x2, g.reshape(1, d), win, winw, q_norm.reshape(1, RQ), kv_norm.reshape(1, RKV), kg, kb, wqi,
      wuq, wuk)


def _fold(t, op):
    groups = t.shape[0] // SUBLANES
    group = lambda r: t[r * SUBLANES:(r + 1) * SUBLANES, :]
    acc = [group(w) for w in range(REDUCE_WAYS)]
    for r in range(REDUCE_WAYS, groups, REDUCE_WAYS):
        acc = [op(acc[w], group(r + w)) for w in range(REDUCE_WAYS)]
    while len(acc) > 1:
        acc = [op(acc[2 * w], acc[2 * w + 1]) for w in range(len(acc) // 2)]
    return acc[0]


def _fori_grouped(n, body, init, group=LOOP_GROUP):
    def grouped(q, carry):
        for u in range(group):
            carry = body(group * q + u, carry)
        return carry
    carry = lax.fori_loop(0, n // group, grouped, init)
    return lax.fori_loop(group * (n // group), n, body, carry)


def _dsa_attn_kernel(x_ref, qi_ref, wt_ref, ql_ref, kidx_ref, ckv_ref, ckvt_ref, wuvt_ref,
                     wout_ref, cmax_ref, tri_ref, o_ref, sc_s, lg_a, lg_b, p_a, p_b, bias_s, acc_s, m_s, *, top_k):
    TQ, TK, TH = ATT_Q, ATT_K, ATT_K // 2
    HI, H, RKV = IDX_HEADS, DSA_HEADS, DSA_KV_RANK
    i = pl.program_id(1)
    n_tiles = (i * TQ + TQ + TK - 1) // TK
    qpos = i * TQ + lax.broadcasted_iota(jnp.int32, (1, TQ), 1)
    keyh = lax.broadcasted_iota(jnp.int32, (TH, 1), 0)
    neg_inf = jnp.float32(-jnp.inf)
    kf = jnp.float32(top_k)

    def tile_rows(j):
        return pl.ds(pl.multiple_of(j * TK, TK), TK)

    w_t = wt_ref[...]
    last_half = 2 * n_tiles - 1

    def half_rows(jh):
        return pl.ds(pl.multiple_of(jh * TH, TH), TH)

    def score_dots(jh, buf):
        buf[...] = _dot(kidx_ref[half_rows(jh), :], qi_ref[...])

    def score_reduce(jh, buf, carry):
        rmax, rmin, c_ge0, c_gt0 = carry
        s = jnp.maximum(buf[:, 0:TQ], 0.0) * w_t[0:1, :]
        for hh in range(1, HI):
            s = s + jnp.maximum(buf[:, hh * TQ:(hh + 1) * TQ], 0.0) * w_t[hh:hh + 1, :]
        causal = keyh + jh * TH <= qpos
        s = jnp.where(causal, s, neg_inf)
        sc_s[half_rows(jh), :] = s
        rmax = jnp.maximum(rmax, _fold(s, jnp.maximum))
        rmin = jnp.minimum(rmin, _fold(jnp.where(causal, s, jnp.inf), jnp.minimum))
        c_ge0 = c_ge0 + _fold(jnp.where(s >= 0.0, 1.0, 0.0), jnp.add)
        c_gt0 = c_gt0 + _fold(jnp.where(s > 0.0, 1.0, 0.0), jnp.add)
        return rmax, rmin, c_ge0, c_gt0

    def score_pair(j, carry):
        score_dots(2 * j + 1, lg_b)
        carry = score_reduce(2 * j, lg_a, carry)
        score_dots(jnp.minimum(2 * j + 2, last_half), lg_a)
        return score_reduce(2 * j + 1, lg_b, carry)

    part = lambda v: jnp.full((SUBLANES, TQ), v, F32)
    score_dots(0, lg_a)
    rmax, rmin, c_ge0, c_gt0 = _fori_grouped(
        n_tiles, score_pair, (part(-jnp.inf), part(jnp.inf), part(0.0), part(0.0)))
    rmax = jnp.max(rmax, axis=0, keepdims=True)
    rmin = jnp.min(rmin, axis=0, keepdims=True)
    c_ge0 = jnp.sum(c_ge0, axis=0, keepdims=True)
    c_gt0 = jnp.sum(c_gt0, axis=0, keepdims=True)

    def count(pred):
        def body(j, acc):
            return acc + _fold(jnp.where(pred(sc_s[tile_rows(j), :], j), 1.0, 0.0), jnp.add)
        return jnp.sum(_fori_grouped(n_tiles, body, part(0.0)), axis=0, keepdims=True)

    n_valid = (qpos + 1).astype(F32)
    all_sel = n_valid <= kf
    zero_tie = (c_gt0 < kf) & (c_ge0 >= kf)
    positive = c_gt0 >= kf
    lo = jnp.where(positive | zero_tie, 0.0, rmin)
    hi = jnp.where(positive, 2.0 * rmax, 0.0)
    clo = jnp.where(positive | zero_tie, c_ge0, n_valid)
    chi = jnp.where(positive, 0.0, c_ge0)
    lo = jnp.where(all_sel, jnp.float32(jnp.finfo(jnp.float32).min), lo)
    clo = jnp.where(all_sel, n_valid, clo)
    done = all_sel | zero_tie | (clo == kf)
    one = jnp.ones((1, TQ), F32)
    log_target = math.log(top_k + 0.5)

    def pending(done):
        return jnp.sum(jnp.where(done, 0, 1))

    def search_cond(c):
        return (c[1] > 0) & (c[0] < SEARCH_CAP)

    def search_step(state):
        lo, hi, clo, chi, wl, wh, side, done_f = state
        done = done_f > 0.0
        mid = 0.5 * lo + 0.5 * hi
        fa = (jnp.log(clo) - log_target) * wl
        fb = (log_target - jnp.log(jnp.maximum(chi, 0.5))) * wh
        cand = lo + (hi - lo) * (fa / (fa + fb))
        cand = jnp.where((cand > lo) & (cand < hi), cand, mid)
        collapsed = (cand <= lo) | (cand >= hi)
        cm = count(lambda t, j: t >= cand)
        move = jnp.logical_not(done | collapsed)
        up = move & (cm >= kf)
        down = move & (cm < kf)
        wh = jnp.where(up, jnp.where(side > 0.0, 0.5 * wh, 1.0), jnp.where(down, 1.0, wh))
        wl = jnp.where(down, jnp.where(side < 0.0, 0.5 * wl, 1.0), jnp.where(up, 1.0, wl))
        side = jnp.where(up, 1.0, jnp.where(down, -1.0, side))
        lo = jnp.where(up, cand, lo)
        clo = jnp.where(up, cm, clo)
        hi = jnp.where(down, cand, hi)
        chi = jnp.where(down, cm, chi)
        done = done | collapsed | (clo == kf)
        return lo, hi, clo, chi, wl, wh, side, jnp.where(done, 1.0, 0.0)

    def search_body(c):
        state = search_step(c[2:])
        return (c[0] + 1, pending(state[-1] > 0.0)) + state

    state = (lo, hi, clo, chi, one, one, 0.0 * one, jnp.where(done, 1.0, 0.0))
    state = lax.fori_loop(0, SEARCH_UNCHECKED, lambda _, s: search_step(s), state)

    lo, hi, clo, chi, wl, wh, side, done_f = state
    inf = jnp.float32(jnp.inf)

    def insert(ranks, v):
        out = []
        for r in ranks:
            out.append(jnp.minimum(r, v))
            v = jnp.maximum(r, v)
        return out

    def smallest_body(j, ranks):
        t = sc_s[tile_rows(j), :]
        ranks = list(ranks)
        for g in range(TK // SUBLANES):
            v = t[g * SUBLANES:(g + 1) * SUBLANES, :]
            w = g % 2
            ranks[w] = tuple(insert(ranks[w], jnp.where(v >= lo, v, inf)))
        return tuple(ranks)

    empty = tuple(jnp.full((SUBLANES, TQ), inf, F32) for _ in range(FINISH_RANKS))
    ranks = lax.fori_loop(0, n_tiles, smallest_body, (empty, empty))
    final = [jnp.full((1, TQ), inf, F32) for _ in range(FINISH_RANKS)]
    for chain in ranks:
        for r in chain:
            for sub in range(SUBLANES):
                final = insert(final, r[sub:sub + 1, :])
    extra = clo - kf
    kth = final[0]
    below = -inf
    for e in range(1, FINISH_RANKS):
        kth = jnp.where(extra >= e, final[e], kth)
        below = jnp.where(extra == e, final[e - 1], below)
    near = (done_f <= 0.0) & (extra < FINISH_RANKS)
    lo = jnp.where(near, kth, lo)
    clo = jnp.where(near, jnp.where(below == kth, kf + 1.0, kf), clo)
    done_f = jnp.where(near, 1.0, done_f)
    state = (lo, hi, clo, chi, wl, wh, side, done_f)

    res = lax.while_loop(search_cond, search_body,
                         (jnp.int32(0), pending(state[-1] > 0.0)) + state)
    tau, clo = res[2], res[4]

    excess = clo > kf

    @pl.when(jnp.sum(jnp.where(excess, 1, 0)) > 0)
    def _():
        need = kf - count(lambda t, j: t > tau)

        def rank_and_drop(j, seen):
            t = sc_s[tile_rows(j), :]
            tied = t == tau
            rank = seen + _dot(tri_ref[...], jnp.where(tied, 1.0, 0.0).astype(BF16))
            sc_s[tile_rows(j), :] = jnp.where(excess & tied & (rank > need), neg_inf, t)
            return rank[TK - 1:TK, :]

        lax.fori_loop(0, n_tiles, rank_and_drop, jnp.zeros((1, TQ), F32))

    c_exp = (DSA_HEAD_DIM ** -0.5) * math.log2(math.e)
    def attn_logits(jh, buf):
        buf[...] = _dot(ckv_ref[half_rows(jh), :], ql_ref[...])

    ql_f = ql_ref[...].astype(F32)
    bound = jnp.sqrt(jnp.sum(ql_f * ql_f, axis=0, keepdims=True)) * cmax_ref[...]

    def fast_values(jh, p_buf):
        cols_j = pl.ds(pl.multiple_of(jh * TH, TH), TH)
        acc_s[...] += _dot(ckvt_ref[:, cols_j], p_buf[...])

    def fast_half(jh, buf, p_buf, jh_next, buf_next, jh_prev, p_prev):
        keep = jnp.where(sc_s[half_rows(jh), :] >= tau, 1.0, 0.0).astype(BF16)
        keys_next = ckv_ref[half_rows(jh_next), :]
        vals_prev = ckvt_ref[:, pl.ds(pl.multiple_of(jh_prev * TH, TH), TH)]
        per_group = H // COLUMN_GROUPS
        for grp in range(COLUMN_GROUPS):
            gcols = slice(grp * per_group * TQ, (grp + 1) * per_group * TQ)
            buf_next[:, gcols] = _dot(keys_next, ql_ref[:, gcols])
            for hh in range(grp * per_group, (grp + 1) * per_group):
                cols = slice(hh * TQ, (hh + 1) * TQ)
                p = jnp.exp2((buf[:, cols] - bound[:, cols]) * c_exp).astype(BF16)
                p_buf[:, cols] = p * keep
            acc_s[:, gcols] += _dot(vals_prev, p_prev[:, gcols])

    def fast_pair(j, carry):
        fast_half(2 * j, lg_a, p_a, 2 * j + 1, lg_b, jnp.maximum(2 * j - 1, 0), p_b)
        fast_half(2 * j + 1, lg_b, p_b, jnp.minimum(2 * j + 2, last_half), lg_a, 2 * j, p_a)
        return carry

    acc_s[...] = jnp.zeros_like(acc_s)
    p_b[...] = jnp.zeros_like(p_b)
    attn_logits(0, lg_a)
    _fori_grouped(n_tiles, fast_pair, 0)
    fast_values(last_half, p_b)

    def write_output():
        o_lat = (acc_s[0:RKV, :] / acc_s[RKV:RKV + 1, :]).astype(BF16)
        o_t = jnp.concatenate(
            [_dot(wuvt_ref[hh], o_lat[:, hh * TQ:(hh + 1) * TQ]) for hh in range(H)], axis=0)
        o_ref[...] = x_ref[...] + _dot(o_t.T.astype(BF16), wout_ref[...])

    write_output()
    denom_bad = jnp.logical_not(acc_s[RKV:RKV + 1, :] >= DENOM_FLOOR)
    redo = jnp.sum(jnp.where(denom_bad, 1, 0)) > 0

    def attn_softmax(jh, buf, p_buf):
        bias_s[...] = jnp.where(sc_s[half_rows(jh), :] >= tau, 0.0, MASKED)
        alphas = []
        for hh in range(H):
            cols = slice(hh * TQ, (hh + 1) * TQ)
            m_old = m_s[hh:hh + 1, :]
            mx = _fold(buf[:, cols] + bias_s[...], jnp.maximum)
            m_new = jnp.maximum(m_old, jnp.max(mx, axis=0, keepdims=True))
            m_s[hh:hh + 1, :] = m_new
            alphas.append(jnp.exp2((m_old - m_new) * c_exp))
            p_buf[:, cols] = jnp.exp2((buf[:, cols] + bias_s[...] - m_new) * c_exp).astype(BF16)
        return jnp.concatenate(alphas, axis=1)

    def attn_values(jh, p_buf, alpha):
        cols_j = pl.ds(pl.multiple_of(jh * TH, TH), TH)
        acc_s[...] = acc_s[...] * alpha + _dot(ckvt_ref[:, cols_j], p_buf[...])

    def attn_pair(j, alpha_b):
        attn_logits(2 * j + 1, lg_b)
        alpha_a = attn_softmax(2 * j, lg_a, p_a)
        attn_values(jnp.maximum(2 * j - 1, 0), p_b, alpha_b)
        attn_logits(jnp.minimum(2 * j + 2, last_half), lg_a)
        alpha_b = attn_softmax(2 * j + 1, lg_b, p_b)
        attn_values(2 * j, p_a, alpha_a)
        return alpha_b

    @pl.when(redo)
    def _():
        m_s[...] = jnp.full_like(m_s, MASKED)
        acc_s[...] = jnp.zeros_like(acc_s)
        p_b[...] = jnp.zeros_like(p_b)
        attn_logits(0, lg_a)
        alpha_b = lax.fori_loop(0, n_tiles, attn_pair, jnp.ones((1, H * TQ), F32))
        attn_values(last_half, p_b, alpha_b)
        write_output()


def _dsa_attn(x, qi, wt, ql, kidx, ckv, ckvt, kv_norm, w_uv, w_out):
    b, s, d = x.shape
    H, RKV, HI, TQ = DSA_HEADS, DSA_KV_RANK, IDX_HEADS, ATT_Q
    nq = s // TQ
    top_k = min(TOPK_MAX, s // 4)
    wuvt = w_uv.transpose(1, 2, 0).astype(BF16)
    wout = w_out.astype(BF16)
    cmax = (math.sqrt(RKV) * jnp.max(jnp.abs(kv_norm))).reshape(1, 1).astype(F32)
    tri = jnp.tril(jnp.ones((ATT_K, ATT_K), F32)).astype(BF16)
    qrow = pl.BlockSpec((None, TQ, d), lambda bi, i: (bi, i, 0))
    qblk = lambda r, c: pl.BlockSpec((None, r, c), lambda bi, i: (bi * nq + i, 0, 0))
    seq = lambda width: pl.BlockSpec((s, width), lambda bi, i: (bi, 0))
    return pl.pallas_call(
        functools.partial(_dsa_attn_kernel, top_k=top_k),
        grid=(b, nq),
        in_specs=[qrow, qblk(LANES, HI * TQ), qblk(HI, TQ), qblk(RKV, H * TQ), seq(LANES), seq(RKV),
                  pl.BlockSpec((KV_ROWS, s), lambda bi, i: (0, bi)),
                  _resident(wuvt.shape), _resident(wout.shape), _resident((1, 1)),
                  _resident(tri.shape)],
        out_specs=qrow,
        out_shape=jax.ShapeDtypeStruct((b, s, d), F32),
        scratch_shapes=[
            pltpu.VMEM((s, TQ), F32),
            pltpu.VMEM((ATT_K // 2, H * TQ), F32),
            pltpu.VMEM((ATT_K // 2, H * TQ), F32),
            pltpu.VMEM((ATT_K // 2, H * TQ), BF16),
            pltpu.VMEM((ATT_K // 2, H * TQ), BF16),
            pltpu.VMEM((ATT_K // 2, TQ), F32),
            pltpu.VMEM((KV_ROWS, H * TQ), F32),
            pltpu.VMEM((H, TQ), F32),
        ],
        compiler_params=_params(2),
        name="dsa_attn",
    )(x, qi, wt, ql, kidx, ckv, ckvt, wuvt, wout, cmax, tri)


def _dsa(x, g, w_in, q_norm, kv_norm, w_uq, w_qidx, kidx_g, kidx_b, w_uk, w_uv, w_out):
    b, s, d = x.shape
    qi, wt, ql, kidx, ckv, ckvt = _dsa_proj(x.reshape(b * s, d), g, w_in, q_norm, kv_norm, w_uq,
                                            w_qidx, kidx_g, kidx_b, w_uk)
    return _dsa_attn(x, qi, wt, ql, kidx, ckv, ckvt, kv_norm, w_uv, w_out)


def kernel(x, norm_ffn1, w_ffn1_in, w_ffn1_out, norm_mix, norm_ffn2, w_ffn2_in, w_ffn2_out, ret_w_in,
           ret_w_out, dsa_w_in, dsa_q_norm, dsa_kv_norm, dsa_w_uq, dsa_w_qidx, dsa_kidx_g, dsa_kidx_b,
           dsa_w_uk, dsa_w_uv, dsa_w_out, final_norm):
    b, s, d = x.shape
    depth = norm_ffn1.shape[0]
    for layer in range(depth):
        x = _ffn(x.reshape(b * s, d), norm_ffn1[layer], w_ffn1_in[layer], w_ffn1_out[layer])
        x = x.reshape(b, s, d)
        j = layer // 2
        if layer % 2 == 0:
            x = _retention(x, norm_mix[layer], ret_w_in[j], ret_w_out[j])
        else:
            x = _dsa(x, norm_mix[layer], dsa_w_in[j], dsa_q_norm[j], dsa_kv_norm[j], dsa_w_uq[j],
                     dsa_w_qidx[j], dsa_kidx_g[j], dsa_kidx_b[j], dsa_w_uk[j], dsa_w_uv[j],
                     dsa_w_out[j])
        last = layer == depth - 1
        x = _ffn(x.reshape(b * s, d), norm_ffn2[layer], w_ffn2_in[layer], w_ffn2_out[layer],
                 final_g=final_norm if last else None)
        x = x.reshape(b, s, d)
    return x
```

```python
import functools
import math

import jax
import jax.numpy as jnp
from jax import lax
from jax.experimental import pallas as pl
from jax.experimental.pallas import tpu as pltpu

F32 = jnp.float32
BF16 = jnp.bfloat16

D_MODEL = 1024
D_FF = 2816
RMS_EPS = 1e-6

RET_HEADS = 4
RET_DK = D_MODEL // RET_HEADS
RET_DV = 2 * RET_DK
RET_CHUNK = 128
ROT_BASE = 10000.0

DSA_HEADS = 8
DSA_HEAD_DIM = D_MODEL // DSA_HEADS
DSA_Q_RANK = 256
DSA_KV_RANK = 256
IDX_HEADS = 8
IDX_DIM = 64
TOPK_MAX = 256

LANES = 128
SUBLANES = 8
BF16_ROWS = 16
VMEM_LIMIT_BYTES = 56 * 1024 * 1024

FFN_ROWS = 512
FFN_CHUNK = 256
RET_ROWS = 512
PROJ_ROWS = 512
ATT_Q = LANES
ATT_K = 512
KV_ROWS = DSA_KV_RANK + BF16_ROWS
MASKED = -1e30
SEARCH_CAP = 400
SEARCH_UNCHECKED = 9
FINISH_RANKS = 4
DENOM_FLOOR = 2.0 ** -80
REDUCE_WAYS = 4
LOOP_GROUP = 4
COLUMN_GROUPS = 4


def _resident(shape):
    nd = len(shape)
    return pl.BlockSpec(shape, lambda *_: (0,) * nd, pipeline_mode=pl.Buffered(1))


def _params(n_grid, flags=None):
    return pltpu.CompilerParams(
        dimension_semantics=("arbitrary",) * n_grid, vmem_limit_bytes=VMEM_LIMIT_BYTES, flags=flags)


def _rms(x, g):
    y = x * lax.rsqrt(jnp.mean(x * x, axis=-1, keepdims=True) + RMS_EPS)
    return y * g


def _dot(a, b):
    return jnp.dot(a, b, preferred_element_type=F32)


def _dot_nt(a, b):
    return lax.dot_general(a, b, (((1,), (1,)), ((), ())), preferred_element_type=F32)


def _dot_tn(a, b):
    return lax.dot_general(a, b, (((0,), (0,)), ((), ())), preferred_element_type=F32)


def _ffn_kernel(x_ref, g_ref, win_ref, wo_ref, *rest, final):
    if final:
        fg_ref, o_ref, acc_ref = rest
    else:
        o_ref, acc_ref = rest
    x = x_ref[...]
    h = _rms(x, g_ref[...]).astype(BF16)
    acc_ref[...] = jnp.zeros_like(acc_ref)
    for c in range(D_FF // FFN_CHUNK):
        cols = slice(c * FFN_CHUNK, (c + 1) * FFN_CHUNK)
        gate = _dot(h, win_ref[:, cols])
        up = _dot(h, win_ref[:, D_FF + c * FFN_CHUNK:D_FF + (c + 1) * FFN_CHUNK])
        a = (gate * jax.nn.sigmoid(gate) * up).astype(BF16)
        acc_ref[...] += _dot(a, wo_ref[cols, :])
    y = x + 0.5 * acc_ref[...]
    if final:
        y = _rms(y, fg_ref[...])
    o_ref[...] = y


def _ffn(x2, g, w_in, w_out, final_g=None):
    n, d = x2.shape
    win = w_in.astype(BF16)
    wo = w_out.astype(BF16)
    row = pl.BlockSpec((FFN_ROWS, d), lambda i: (i, 0))
    ins = [x2, g.reshape(1, d), win, wo]
    specs = [row, _resident((1, d)), _resident(win.shape), _resident(wo.shape)]
    if final_g is not None:
        ins.append(final_g.reshape(1, d))
        specs.append(_resident((1, d)))
    return pl.pallas_call(
        functools.partial(_ffn_kernel, final=final_g is not None),
        grid=(n // FFN_ROWS,),
        in_specs=specs,
        out_specs=row,
        out_shape=jax.ShapeDtypeStruct((n, d), F32),
        scratch_shapes=[pltpu.VMEM((FFN_ROWS, d), F32)],
        compiler_params=_params(1),
        name="ffn_final" if final_g is not None else "ffn",
    )(*ins)


def _ret_kernel(x_ref, g_ref, win_ref, wout_ref, cos_ref, sin_ref, inner_ref, qdec_ref, kdec_ref,
                cdec_ref, o_ref, q_s, k_s, v_s, gate_s, z_s, state_s):
    H, dk, dv, C = RET_HEADS, RET_DK, RET_DV, RET_CHUNK
    half = dk // 2

    @pl.when(pl.program_id(1) == 0)
    def _():
        state_s[...] = jnp.zeros_like(state_s)

    x = x_ref[...]
    h = _rms(x, g_ref[...]).astype(BF16)
    cos = cos_ref[...]
    sin = sin_ref[...]
    q = _dot(h, win_ref[:, 0:H * dk])
    k = _dot(h, win_ref[:, H * dk:2 * H * dk])
    for hh in range(H):
        q0 = q[:, hh * dk:hh * dk + half]
        q1 = q[:, hh * dk + half:(hh + 1) * dk]
        q_s[:, hh * dk:hh * dk + half] = (q0 * cos - q1 * sin).astype(BF16)
        q_s[:, hh * dk + half:(hh + 1) * dk] = (q1 * cos + q0 * sin).astype(BF16)
        k0 = k[:, hh * dk:hh * dk + half]
        k1 = k[:, hh * dk + half:(hh + 1) * dk]
        k_s[:, hh * dk:hh * dk + half] = (k0 * cos - k1 * sin) * (dk ** -0.5)
        k_s[:, hh * dk + half:(hh + 1) * dk] = (k1 * cos + k0 * sin) * (dk ** -0.5)
    v_s[...] = _dot(h, win_ref[:, 2 * H * dk:2 * H * dk + H * dv]).astype(BF16)
    gate_s[...] = _dot(h, win_ref[:, 2 * H * dk + H * dv:])

    def chunk(c, carry):
        r0 = pl.multiple_of(c * C, C)
        rows = pl.ds(r0, C)
        for hh in range(H):
            qc = q_s[rows, hh * dk:(hh + 1) * dk]
            kc = k_s[rows, hh * dk:(hh + 1) * dk]
            vc = v_s[rows, hh * dv:(hh + 1) * dv]
            st = state_s[hh]
            scores = _dot_nt(qc, kc.astype(BF16)) * inner_ref[hh]
            inner = _dot(scores.astype(BF16), vc)
            cross = _dot(qc, st.astype(BF16)) * qdec_ref[hh]
            kd = (kc * kdec_ref[hh]).astype(BF16)
            state_s[hh] = st * cdec_ref[hh] + _dot_tn(kd, vc)
            out = inner + cross
            out = out * lax.rsqrt(jnp.mean(out * out, axis=-1, keepdims=True) + RMS_EPS)
            gt = gate_s[rows, hh * dv:(hh + 1) * dv]
            z_s[rows, hh * dv:(hh + 1) * dv] = (gt * jax.nn.sigmoid(gt) * out).astype(BF16)
        return carry

    lax.fori_loop(0, x.shape[0] // C, chunk, 0, unroll=True)
    o_ref[...] = x + _dot(z_s[...], wout_ref[...])


def _ret_tables():
    H, C = RET_HEADS, RET_CHUNK
    log_gamma = jnp.log(1.0 - 2.0 ** (-5.0 - jnp.arange(H, dtype=F32)))
    idx = jnp.arange(C, dtype=F32)
    diff = idx[:, None] - idx[None, :]
    inner = jnp.where(diff[None] >= 0,
                      jnp.exp(jnp.maximum(diff, 0.0)[None] * log_gamma[:, None, None]), 0.0)
    qdec = jnp.exp((idx[None, :] + 1.0) * log_gamma[:, None])[:, :, None]
    kdec = jnp.exp((C - 1.0 - idx[None, :]) * log_gamma[:, None])[:, :, None]
    cdec = jnp.exp(C * log_gamma)[:, None, None]
    return inner, qdec, kdec, cdec


def _rot_tables(s):
    pos = jnp.arange(s, dtype=F32)
    freq = 1.0 / (ROT_BASE ** jnp.linspace(0.0, 1.0, RET_DK // 2, dtype=F32))
    ang = pos[:, None] * freq[None, :]
    return jnp.cos(ang), jnp.sin(ang)


def _pair_split_columns(w, heads, dim):
    d = w.shape[0]
    w = w.reshape(d, heads, dim // 2, 2)
    return jnp.concatenate([w[..., 0], w[..., 1]], axis=-1).reshape(d, heads * dim)


def _retention(x, g, w_in, w_out):
    b, s, d = x.shape
    H, dk, dv = RET_HEADS, RET_DK, RET_DV
    tb = min(RET_ROWS, s)
    wq = _pair_split_columns(w_in[:, :H * dk], H, dk)
    wk = _pair_split_columns(w_in[:, H * dk:2 * H * dk], H, dk)
    win = jnp.concatenate([wq, wk, w_in[:, 2 * H * dk:]], axis=1).astype(BF16)
    wout = w_out.astype(BF16)
    cos, sin = _rot_tables(s)
    inner, qdec, kdec, cdec = _ret_tables()
    row = pl.BlockSpec((None, tb, d), lambda bi, i: (bi, i, 0))
    rot = pl.BlockSpec((tb, dk // 2), lambda bi, i: (i, 0))
    return pl.pallas_call(
        _ret_kernel,
        grid=(b, s // tb),
        in_specs=[row, _resident((1, d)), _resident(win.shape), _resident(wout.shape), rot, rot,
                  _resident(inner.shape), _resident(qdec.shape), _resident(kdec.shape),
                  _resident(cdec.shape)],
        out_specs=row,
        out_shape=jax.ShapeDtypeStruct((b, s, d), F32),
        scratch_shapes=[
            pltpu.VMEM((tb, H * dk), BF16),
            pltpu.VMEM((tb, H * dk), F32),
            pltpu.VMEM((tb, H * dv), BF16),
            pltpu.VMEM((tb, H * dv), F32),
            pltpu.VMEM((tb, H * dv), BF16),
            pltpu.VMEM((H, dk, dv), F32),
        ],
        compiler_params=_params(2),
        name="retention",
    )(x, g.reshape(1, d), win, wout, cos, sin, inner, qdec, kdec, cdec)


def _dsa_proj_kernel(x_ref, g_ref, win_ref, winw_ref, qn_ref, kvn_ref, kg_ref, kb_ref, wqi_ref,
                     wuq_ref, wuk_ref, qi_ref, wt_ref, ql_ref, kidx_ref, ckv_ref, ckvt_ref):
    RQ, RKV, HI, H, dh, TQ = DSA_Q_RANK, DSA_KV_RANK, IDX_HEADS, DSA_HEADS, DSA_HEAD_DIM, ATT_Q
    n_blk = x_ref.shape[0] // TQ
    h = _rms(x_ref[...], g_ref[...]).astype(BF16)
    proj = _dot(h, win_ref[...])
    c_q = _rms(proj[:, :RQ], qn_ref[...])
    c_kv = _rms(proj[:, RQ:RQ + RKV], kvn_ref[...])
    kx = proj[:, RQ + RKV:RQ + RKV + LANES]
    lane = lax.broadcasted_iota(jnp.int32, kx.shape, 1)
    mu = jnp.sum(kx, axis=-1, keepdims=True) * (1.0 / IDX_DIM)
    cen = jnp.where(lane < IDX_DIM, kx - mu, 0.0)
    var = jnp.sum(cen * cen, axis=-1, keepdims=True) * (1.0 / IDX_DIM)
    kidx_ref[...] = (cen * lax.rsqrt(var + RMS_EPS) * kg_ref[...] + kb_ref[...]).astype(BF16)
    ckv_ref[...] = c_kv.astype(BF16)
    ckvt_ref[0:RKV, :] = c_kv.T.astype(BF16)
    ones_row = lax.broadcasted_iota(jnp.int32, (BF16_ROWS, x_ref.shape[0]), 0) == 0
    ckvt_ref[RKV:, :] = jnp.where(ones_row, 1.0, 0.0).astype(BF16)
    w_t = _dot_nt(winw_ref[...], h) * (HI ** -0.5)
    cq = c_q.astype(BF16)
    qi_t = (_dot_nt(wqi_ref[...], cq) * (IDX_DIM ** -0.5)).astype(BF16)
    q = _dot(cq, wuq_ref[...]).astype(BF16)
    ql_t = [_dot_nt(wuk_ref[hh], q[:, hh * dh:(hh + 1) * dh]).astype(BF16) for hh in range(H)]
    for u in range(n_blk):
        cols = slice(u * TQ, (u + 1) * TQ)
        wt_ref[u] = w_t[0:HI, cols]
        for hh in range(HI):
            qi_ref[u, :, hh * TQ:(hh + 1) * TQ] = qi_t[hh * LANES:(hh + 1) * LANES, cols]
        for hh in range(H):
            ql_ref[u, :, hh * TQ:(hh + 1) * TQ] = ql_t[hh][:, cols]


def _dsa_proj(x2, g, w_in, q_norm, kv_norm, w_uq, w_qidx, kidx_g, kidx_b, w_uk):
    n, d = x2.shape
    RQ, RKV, HI, DI, H, TQ = DSA_Q_RANK, DSA_KV_RANK, IDX_HEADS, IDX_DIM, DSA_HEADS, ATT_Q
    pad_k = jnp.zeros((d, LANES - DI), F32)
    win = jnp.concatenate([w_in[:, :RQ + RKV + DI], pad_k], axis=1).astype(BF16)
    winw = jnp.pad(w_in[:, RQ + RKV + DI:].T, ((0, BF16_ROWS - HI), (0, 0))).astype(BF16)
    kg = jnp.pad(kidx_g, (0, LANES - DI)).reshape(1, LANES)
    kb = jnp.pad(kidx_b, (0, LANES - DI)).reshape(1, LANES)
    wqi = jnp.pad(w_qidx.reshape(RQ, HI, DI), ((0, 0), (0, 0), (0, LANES - DI)))
    wqi = wqi.reshape(RQ, HI * LANES).T.astype(BF16)
    wuq = w_uq.astype(BF16)
    wuk = w_uk.transpose(1, 0, 2).astype(BF16)
    tm = min(PROJ_ROWS, n)
    nb = tm // TQ
    row = lambda w: pl.BlockSpec((tm, w), lambda i: (i, 0))
    blk = lambda r, c: pl.BlockSpec((nb, r, c), lambda i: (i, 0, 0))
    return pl.pallas_call(
        _dsa_proj_kernel,
        grid=(n // tm,),
        in_specs=[row(d), _resident((1, d)), _resident(win.shape), _resident(winw.shape),
                  _resident((1, RQ)), _resident((1, RKV)), _resident((1, LANES)),
                  _resident((1, LANES)), _resident(wqi.shape), _resident(wuq.shape),
                  _resident(wuk.shape)],
        out_specs=[blk(LANES, HI * TQ), blk(HI, TQ), blk(RKV, H * TQ), row(LANES), row(RKV),
                   pl.BlockSpec((KV_ROWS, tm), lambda i: (0, i))],
        out_shape=[
            jax.ShapeDtypeStruct((n // TQ, LANES, HI * TQ), BF16),
            jax.ShapeDtypeStruct((n // TQ, HI, TQ), F32),
            jax.ShapeDtypeStruct((n // TQ, RKV, H * TQ), BF16),
            jax.ShapeDtypeStruct((n, LANES), BF16),
            jax.ShapeDtypeStruct((n, RKV), BF16),
            jax.ShapeDtypeStruct((KV_ROWS, n), BF16),
        ],
        compiler_params=_params(1),
        name="dsa_proj",
    )(x2, g.reshape(1, d), win, winw, q_norm.reshape(1, RQ), kv_norm.reshape(1, RKV), kg, kb, wqi,
      wuq, wuk)


def _fold(t, op):
    groups = t.shape[0] // SUBLANES
    group = lambda r: t[r * SUBLANES:(r + 1) * SUBLANES, :]
    acc = [group(w) for w in range(REDUCE_WAYS)]
    for r in range(REDUCE_WAYS, groups, REDUCE_WAYS):
        acc = [op(acc[w], group(r + w)) for w in range(REDUCE_WAYS)]
    while len(acc) > 1:
        acc = [op(acc[2 * w], acc[2 * w + 1]) for w in range(len(acc) // 2)]
    return acc[0]


def _fori_grouped(n, body, init, group=LOOP_GROUP):
    def grouped(q, carry):
        for u in range(group):
            carry = body(group * q + u, carry)
        return carry
    carry = lax.fori_loop(0, n // group, grouped, init)
    return lax.fori_loop(group * (n // group), n, body, carry)


def _dsa_attn_kernel(x_ref, qi_ref, wt_ref, ql_ref, kidx_ref, ckv_ref, ckvt_ref, wuvt_ref,
                     wout_ref, cmax_ref, tri_ref, o_ref, sc_s, lg_a, lg_b, p_a, p_b, bias_s, acc_s, m_s, *, top_k):
    TQ, TK, TH = ATT_Q, ATT_K, ATT_K // 2
    HI, H, RKV = IDX_HEADS, DSA_HEADS, DSA_KV_RANK
    i = pl.program_id(1)
    n_tiles = (i * TQ + TQ + TK - 1) // TK
    qpos = i * TQ + lax.broadcasted_iota(jnp.int32, (1, TQ), 1)
    keyh = lax.broadcasted_iota(jnp.int32, (TH, 1), 0)
    neg_inf = jnp.float32(-jnp.inf)
    kf = jnp.float32(top_k)

    def tile_rows(j):
        return pl.ds(pl.multiple_of(j * TK, TK), TK)

    w_t = wt_ref[...]
    last_half = 2 * n_tiles - 1

    def half_rows(jh):
        return pl.ds(pl.multiple_of(jh * TH, TH), TH)

    def score_dots(jh, buf):
        buf[...] = _dot(kidx_ref[half_rows(jh), :], qi_ref[...])

    def score_reduce(jh, buf, carry):
        rmax, rmin, c_ge0, c_gt0 = carry
        s = jnp.maximum(buf[:, 0:TQ], 0.0) * w_t[0:1, :]
        for hh in range(1, HI):
            s = s + jnp.maximum(buf[:, hh * TQ:(hh + 1) * TQ], 0.0) * w_t[hh:hh + 1, :]
        causal = keyh + jh * TH <= qpos
        s = jnp.where(causal, s, neg_inf)
        sc_s[half_rows(jh), :] = s
        rmax = jnp.maximum(rmax, _fold(s, jnp.maximum))
        rmin = jnp.minimum(rmin, _fold(jnp.where(causal, s, jnp.inf), jnp.minimum))
        c_ge0 = c_ge0 + _fold(jnp.where(s >= 0.0, 1.0, 0.0), jnp.add)
        c_gt0 = c_gt0 + _fold(jnp.where(s > 0.0, 1.0, 0.0), jnp.add)
        return rmax, rmin, c_ge0, c_gt0

    def score_pair(j, carry):
        score_dots(2 * j + 1, lg_b)
        carry = score_reduce(2 * j, lg_a, carry)
        score_dots(jnp.minimum(2 * j + 2, last_half), lg_a)
        return score_reduce(2 * j + 1, lg_b, carry)

    part = lambda v: jnp.full((SUBLANES, TQ), v, F32)
    score_dots(0, lg_a)
    rmax, rmin, c_ge0, c_gt0 = _fori_grouped(
        n_tiles, score_pair, (part(-jnp.inf), part(jnp.inf), part(0.0), part(0.0)))
    rmax = jnp.max(rmax, axis=0, keepdims=True)
    rmin = jnp.min(rmin, axis=0, keepdims=True)
    c_ge0 = jnp.sum(c_ge0, axis=0, keepdims=True)
    c_gt0 = jnp.sum(c_gt0, axis=0, keepdims=True)

    def count(pred):
        def body(j, acc):
            return acc + _fold(jnp.where(pred(sc_s[tile_rows(j), :], j), 1.0, 0.0), jnp.add)
        return jnp.sum(_fori_grouped(n_tiles, body, part(0.0)), axis=0, keepdims=True)

    n_valid = (qpos + 1).astype(F32)
    all_sel = n_valid <= kf
    zero_tie = (c_gt0 < kf) & (c_ge0 >= kf)
    positive = c_gt0 >= kf
    lo = jnp.where(positive | zero_tie, 0.0, rmin)
    hi = jnp.where(positive, 2.0 * rmax, 0.0)
    clo = jnp.where(positive | zero_tie, c_ge0, n_valid)
    chi = jnp.where(positive, 0.0, c_ge0)
    lo = jnp.where(all_sel, jnp.float32(jnp.finfo(jnp.float32).min), lo)
    clo = jnp.where(all_sel, n_valid, clo)
    done = all_sel | zero_tie | (clo == kf)
    one = jnp.ones((1, TQ), F32)
    log_target = math.log(top_k + 0.5)

    def pending(done):
        return jnp.sum(jnp.where(done, 0, 1))

    def search_cond(c):
        return (c[1] > 0) & (c[0] < SEARCH_CAP)

    def search_step(state):
        lo, hi, clo, chi, wl, wh, side, done_f = state
        done = done_f > 0.0
        mid = 0.5 * lo + 0.5 * hi
        fa = (jnp.log(clo) - log_target) * wl
        fb = (log_target - jnp.log(jnp.maximum(chi, 0.5))) * wh
        cand = lo + (hi - lo) * (fa / (fa + fb))
        cand = jnp.where((cand > lo) & (cand < hi), cand, mid)
        collapsed = (cand <= lo) | (cand >= hi)
        cm = count(lambda t, j: t >= cand)
        move = jnp.logical_not(done | collapsed)
        up = move & (cm >= kf)
        down = move & (cm < kf)
        wh = jnp.where(up, jnp.where(side > 0.0, 0.5 * wh, 1.0), jnp.where(down, 1.0, wh))
        wl = jnp.where(down, jnp.where(side < 0.0, 0.5 * wl, 1.0), jnp.where(up, 1.0, wl))
        side = jnp.where(up, 1.0, jnp.where(down, -1.0, side))
        lo = jnp.where(up, cand, lo)
        clo = jnp.where(up, cm, clo)
        hi = jnp.where(down, cand, hi)
        chi = jnp.where(down, cm, chi)
        done = done | collapsed | (clo == kf)
        return lo, hi, clo, chi, wl, wh, side, jnp.where(done, 1.0, 0.0)

    def search_body(c):
        state = search_step(c[2:])
        return (c[0] + 1, pending(state[-1] > 0.0)) + state

    state = (lo, hi, clo, chi, one, one, 0.0 * one, jnp.where(done, 1.0, 0.0))
    state = lax.fori_loop(0, SEARCH_UNCHECKED, lambda _, s: search_step(s), state)

    lo, hi, clo, chi, wl, wh, side, done_f = state
    inf = jnp.float32(jnp.inf)

    def insert(ranks, v):
        out = []
        for r in ranks:
            out.append(jnp.minimum(r, v))
            v = jnp.maximum(r, v)
        return out

    def smallest_body(j, ranks):
        t = sc_s[tile_rows(j), :]
        ranks = list(ranks)
        for g in range(TK // SUBLANES):
            v = t[g * SUBLANES:(g + 1) * SUBLANES, :]
            w = g % 2
            ranks[w] = tuple(insert(ranks[w], jnp.where(v >= lo, v, inf)))
        return tuple(ranks)

    empty = tuple(jnp.full((SUBLANES, TQ), inf, F32) for _ in range(FINISH_RANKS))
    ranks = lax.fori_loop(0, n_tiles, smallest_body, (empty, empty))
    final = [jnp.full((1, TQ), inf, F32) for _ in range(FINISH_RANKS)]
    for chain in ranks:
        for r in chain:
            for sub in range(SUBLANES):
                final = insert(final, r[sub:sub + 1, :])
    extra = clo - kf
    kth = final[0]
    below = -inf
    for e in range(1, FINISH_RANKS):
        kth = jnp.where(extra >= e, final[e], kth)
        below = jnp.where(extra == e, final[e - 1], below)
    near = (done_f <= 0.0) & (extra < FINISH_RANKS)
    lo = jnp.where(near, kth, lo)
    clo = jnp.where(near, jnp.where(below == kth, kf + 1.0, kf), clo)
    done_f = jnp.where(near, 1.0, done_f)
    state = (lo, hi, clo, chi, wl, wh, side, done_f)

    res = lax.while_loop(search_cond, search_body,
                         (jnp.int32(0), pending(state[-1] > 0.0)) + state)
    tau, clo = res[2], res[4]

    excess = clo > kf

    @pl.when(jnp.sum(jnp.where(excess, 1, 0)) > 0)
    def _():
        need = kf - count(lambda t, j: t > tau)

        def rank_and_drop(j, seen):
            t = sc_s[tile_rows(j), :]
            tied = t == tau
            e = jnp.where(tied, 1.0, 0.0).astype(BF16)
            pre = _dot(tri_ref[...], jnp.concatenate([e[:TH], e[TH:]], axis=1))
            upper = seen + pre[:, :TQ]
            lower = upper[TH - 1:TH, :] + pre[:, TQ:]
            rank = jnp.concatenate([upper, lower], axis=0)
            sc_s[tile_rows(j), :] = jnp.where(excess & tied & (rank > need), neg_inf, t)
            return lower[TH - 1:TH, :]

        lax.fori_loop(0, n_tiles, rank_and_drop, jnp.zeros((1, TQ), F32))

    c_exp = (DSA_HEAD_DIM ** -0.5) * math.log2(math.e)
    def attn_logits(jh, buf):
        buf[...] = _dot(ckv_ref[half_rows(jh), :], ql_ref[...])

    ql_f = ql_ref[...].astype(F32)
    bound = jnp.sqrt(jnp.sum(ql_f * ql_f, axis=0, keepdims=True)) * cmax_ref[...]

    def fast_values(jh, p_buf):
        cols_j = pl.ds(pl.multiple_of(jh * TH, TH), TH)
        acc_s[...] += _dot(ckvt_ref[:, cols_j], p_buf[...])

    def fast_half(jh, buf, p_buf, jh_next, buf_next, jh_prev, p_prev):
        keep = jnp.where(sc_s[half_rows(jh), :] >= tau, 1.0, 0.0).astype(BF16)
        keys_next = ckv_ref[half_rows(jh_next), :]
        vals_prev = ckvt_ref[:, pl.ds(pl.multiple_of(jh_prev * TH, TH), TH)]
        per_group = H // COLUMN_GROUPS
        for grp in range(COLUMN_GROUPS):
            gcols = slice(grp * per_group * TQ, (grp + 1) * per_group * TQ)
            buf_next[:, gcols] = _dot(keys_next, ql_ref[:, gcols])
            for hh in range(grp * per_group, (grp + 1) * per_group):
                cols = slice(hh * TQ, (hh + 1) * TQ)
                p = jnp.exp2((buf[:, cols] - bound[:, cols]) * c_exp).astype(BF16)
                p_buf[:, cols] = p * keep
            acc_s[:, gcols] += _dot(vals_prev, p_prev[:, gcols])

    def fast_pair(j, carry):
        fast_half(2 * j, lg_a, p_a, 2 * j + 1, lg_b, jnp.maximum(2 * j - 1, 0), p_b)
        fast_half(2 * j + 1, lg_b, p_b, jnp.minimum(2 * j + 2, last_half), lg_a, 2 * j, p_a)
        return carry

    acc_s[...] = jnp.zeros_like(acc_s)
    p_b[...] = jnp.zeros_like(p_b)
    attn_logits(0, lg_a)
    _fori_grouped(n_tiles, fast_pair, 0)
    fast_values(last_half, p_b)

    def write_output():
        o_lat = (acc_s[0:RKV, :] / acc_s[RKV:RKV + 1, :]).astype(BF16)
        o_t = jnp.concatenate(
            [_dot(wuvt_ref[hh], o_lat[:, hh * TQ:(hh + 1) * TQ]) for hh in range(H)], axis=0)
        o_ref[...] = x_ref[...] + _dot(o_t.T.astype(BF16), wout_ref[...])

    write_output()
    denom_bad = jnp.logical_not(acc_s[RKV:RKV + 1, :] >= DENOM_FLOOR)
    redo = jnp.sum(jnp.where(denom_bad, 1, 0)) > 0

    def attn_softmax(jh, buf, p_buf):
        bias_s[...] = jnp.where(sc_s[half_rows(jh), :] >= tau, 0.0, MASKED)
        alphas = []
        for hh in range(H):
            cols = slice(hh * TQ, (hh + 1) * TQ)
            m_old = m_s[hh:hh + 1, :]
            mx = _fold(buf[:, cols] + bias_s[...], jnp.maximum)
            m_new = jnp.maximum(m_old, jnp.max(mx, axis=0, keepdims=True))
            m_s[hh:hh + 1, :] = m_new
            alphas.append(jnp.exp2((m_old - m_new) * c_exp))
            p_buf[:, cols] = jnp.exp2((buf[:, cols] + bias_s[...] - m_new) * c_exp).astype(BF16)
        return jnp.concatenate(alphas, axis=1)

    def attn_values(jh, p_buf, alpha):
        cols_j = pl.ds(pl.multiple_of(jh * TH, TH), TH)
        acc_s[...] = acc_s[...] * alpha + _dot(ckvt_ref[:, cols_j], p_buf[...])

    def attn_pair(j, alpha_b):
        attn_logits(2 * j + 1, lg_b)
        alpha_a = attn_softmax(2 * j, lg_a, p_a)
        attn_values(jnp.maximum(2 * j - 1, 0), p_b, alpha_b)
        attn_logits(jnp.minimum(2 * j + 2, last_half), lg_a)
        alpha_b = attn_softmax(2 * j + 1, lg_b, p_b)
        attn_values(2 * j, p_a, alpha_a)
        return alpha_b

    @pl.when(redo)
    def _():
        m_s[...] = jnp.full_like(m_s, MASKED)
        acc_s[...] = jnp.zeros_like(acc_s)
        p_b[...] = jnp.zeros_like(p_b)
        attn_logits(0, lg_a)
        alpha_b = lax.fori_loop(0, n_tiles, attn_pair, jnp.ones((1, H * TQ), F32))
        attn_values(last_half, p_b, alpha_b)
        write_output()


def _dsa_attn(x, qi, wt, ql, kidx, ckv, ckvt, kv_norm, w_uv, w_out):
    b, s, d = x.shape
    H, RKV, HI, TQ = DSA_HEADS, DSA_KV_RANK, IDX_HEADS, ATT_Q
    nq = s // TQ
    top_k = min(TOPK_MAX, s // 4)
    wuvt = w_uv.transpose(1, 2, 0).astype(BF16)
    wout = w_out.astype(BF16)
    cmax = (math.sqrt(RKV) * jnp.max(jnp.abs(kv_norm))).reshape(1, 1).astype(F32)
    tri = jnp.tril(jnp.ones((ATT_K // 2, ATT_K // 2), F32)).astype(BF16)
    qrow = pl.BlockSpec((None, TQ, d), lambda bi, i: (bi, i, 0))
    qblk = lambda r, c: pl.BlockSpec((None, r, c), lambda bi, i: (bi * nq + i, 0, 0))
    seq = lambda width: pl.BlockSpec((s, width), lambda bi, i: (bi, 0))
    return pl.pallas_call(
        functools.partial(_dsa_attn_kernel, top_k=top_k),
        grid=(b, nq),
        in_specs=[qrow, qblk(LANES, HI * TQ), qblk(HI, TQ), qblk(RKV, H * TQ), seq(LANES), seq(RKV),
                  pl.BlockSpec((KV_ROWS, s), lambda bi, i: (0, bi)),
                  _resident(wuvt.shape), _resident(wout.shape), _resident((1, 1)),
                  _resident(tri.shape)],
        out_specs=qrow,
        out_shape=jax.ShapeDtypeStruct((b, s, d), F32),
        scratch_shapes=[
            pltpu.VMEM((s, TQ), F32),
            pltpu.VMEM((ATT_K // 2, H * TQ), F32),
            pltpu.VMEM((ATT_K // 2, H * TQ), F32),
            pltpu.VMEM((ATT_K // 2, H * TQ), BF16),
            pltpu.VMEM((ATT_K // 2, H * TQ), BF16),
            pltpu.VMEM((ATT_K // 2, TQ), F32),
            pltpu.VMEM((KV_ROWS, H * TQ), F32),
            pltpu.VMEM((H, TQ), F32),
        ],
        compiler_params=_params(2),
        name="dsa_attn",
    )(x, qi, wt, ql, kidx, ckv, ckvt, wuvt, wout, cmax, tri)


def _dsa(x, g, w_in, q_norm, kv_norm, w_uq, w_qidx, kidx_g, kidx_b, w_uk, w_uv, w_out):
    b, s, d = x.shape
    qi, wt, ql, kidx, ckv, ckvt = _dsa_proj(x.reshape(b * s, d), g, w_in, q_norm, kv_norm, w_uq,
                                            w_qidx, kidx_g, kidx_b, w_uk)
    return _dsa_attn(x, qi, wt, ql, kidx, ckv, ckvt, kv_norm, w_uv, w_out)


def kernel(x, norm_ffn1, w_ffn1_in, w_ffn1_out, norm_mix, norm_ffn2, w_ffn2_in, w_ffn2_out, ret_w_in,
           ret_w_out, dsa_w_in, dsa_q_norm, dsa_kv_norm, dsa_w_uq, dsa_w_qidx, dsa_kidx_g, dsa_kidx_b,
           dsa_w_uk, dsa_w_uv, dsa_w_out, final_norm):
    b, s, d = x.shape
    depth = norm_ffn1.shape[0]
    for layer in range(depth):
        x = _ffn(x.reshape(b * s, d), norm_ffn1[layer], w_ffn1_in[layer], w_ffn1_out[layer])
        x = x.reshape(b, s, d)
        j = layer // 2
        if layer % 2 == 0:
            x = _retention(x, norm_mix[layer], ret_w_in[j], ret_w_out[j])
        else:
            x = _dsa(x, norm_mix[layer], dsa_w_in[j], dsa_q_norm[j], dsa_kv_norm[j], dsa_w_uq[j],
                     dsa_w_qidx[j], dsa_kidx_g[j], dsa_kidx_b[j], dsa_w_uk[j], dsa_w_uv[j],
                     dsa_w_out[j])
        last = layer == depth - 1
        x = _ffn(x.reshape(b * s, d), norm_ffn2[layer], w_ffn2_in[layer], w_ffn2_out[layer],
                 final_g=final_norm if last else None)
        x = x.reshape(b, s, d)
    return x
```

```python
import functools
import math

import jax
import jax.numpy as jnp
from jax import lax
from jax.experimental import pallas as pl
from jax.experimental.pallas import tpu as pltpu

F32 = jnp.float32
BF16 = jnp.bfloat16

D_MODEL = 1024
D_FF = 2816
RMS_EPS = 1e-6

RET_HEADS = 4
RET_DK = D_MODEL // RET_HEADS
RET_DV = 2 * RET_DK
RET_CHUNK = 128
ROT_BASE = 10000.0

DSA_HEADS = 8
DSA_HEAD_DIM = D_MODEL // DSA_HEADS
DSA_Q_RANK = 256
DSA_KV_RANK = 256
IDX_HEADS = 8
IDX_DIM = 64
TOPK_MAX = 256

LANES = 128
SUBLANES = 8
BF16_ROWS = 16
VMEM_LIMIT_BYTES = 56 * 1024 * 1024

FFN_ROWS = 512
FFN_CHUNK = 256
RET_ROWS = 512
PROJ_ROWS = 512
ATT_Q = LANES
ATT_K = 512
KV_ROWS = DSA_KV_RANK + BF16_ROWS
MASKED = -1e30
SEARCH_CAP = 400
SEARCH_UNCHECKED = 9
FINISH_RANKS = 4
DENOM_FLOOR = 2.0 ** -80
REDUCE_WAYS = 4
LOOP_GROUP = 4
COLUMN_GROUPS = 4


def _resident(shape):
    nd = len(shape)
    return pl.BlockSpec(shape, lambda *_: (0,) * nd, pipeline_mode=pl.Buffered(1))


def _params(n_grid, flags=None):
    return pltpu.CompilerParams(
        dimension_semantics=("arbitrary",) * n_grid, vmem_limit_bytes=VMEM_LIMIT_BYTES, flags=flags)


def _rms(x, g):
    y = x * lax.rsqrt(jnp.mean(x * x, axis=-1, keepdims=True) + RMS_EPS)
    return y * g


def _dot(a, b):
    return jnp.dot(a, b, preferred_element_type=F32)


def _dot_nt(a, b):
    return lax.dot_general(a, b, (((1,), (1,)), ((), ())), preferred_element_type=F32)


def _dot_tn(a, b):
    return lax.dot_general(a, b, (((0,), (0,)), ((), ())), preferred_element_type=F32)


def _ffn_kernel(x_ref, g_ref, win_ref, wo_ref, *rest, final):
    if final:
        fg_ref, o_ref, acc_ref = rest
    else:
        o_ref, acc_ref = rest
    x = x_ref[...]
    h = _rms(x, g_ref[...]).astype(BF16)
    acc_ref[...] = jnp.zeros_like(acc_ref)
    for c in range(D_FF // FFN_CHUNK):
        cols = slice(c * FFN_CHUNK, (c + 1) * FFN_CHUNK)
        gate = _dot(h, win_ref[:, cols])
        up = _dot(h, win_ref[:, D_FF + c * FFN_CHUNK:D_FF + (c + 1) * FFN_CHUNK])
        a = (gate * jax.nn.sigmoid(gate) * up).astype(BF16)
        acc_ref[...] += _dot(a, wo_ref[cols, :])
    y = x + 0.5 * acc_ref[...]
    if final:
        y = _rms(y, fg_ref[...])
    o_ref[...] = y


def _ffn(x2, g, w_in, w_out, final_g=None):
    n, d = x2.shape
    win = w_in.astype(BF16)
    wo = w_out.astype(BF16)
    row = pl.BlockSpec((FFN_ROWS, d), lambda i: (i, 0))
    ins = [x2, g.reshape(1, d), win, wo]
    specs = [row, _resident((1, d)), _resident(win.shape), _resident(wo.shape)]
    if final_g is not None:
        ins.append(final_g.reshape(1, d))
        specs.append(_resident((1, d)))
    return pl.pallas_call(
        functools.partial(_ffn_kernel, final=final_g is not None),
        grid=(n // FFN_ROWS,),
        in_specs=specs,
        out_specs=row,
        out_shape=jax.ShapeDtypeStruct((n, d), F32),
        scratch_shapes=[pltpu.VMEM((FFN_ROWS, d), F32)],
        compiler_params=_params(1),
        name="ffn_final" if final_g is not None else "ffn",
    )(*ins)


def _ret_kernel(x_ref, g_ref, win_ref, wout_ref, cos_ref, sin_ref, inner_ref, qdec_ref, kdec_ref,
                cdec_ref, o_ref, q_s, k_s, v_s, gate_s, z_s, state_s):
    H, dk, dv, C = RET_HEADS, RET_DK, RET_DV, RET_CHUNK
    half = dk // 2

    @pl.when(pl.program_id(1) == 0)
    def _():
        state_s[...] = jnp.zeros_like(state_s)

    x = x_ref[...]
    h = _rms(x, g_ref[...]).astype(BF16)
    cos = cos_ref[...]
    sin = sin_ref[...]
    q = _dot(h, win_ref[:, 0:H * dk])
    k = _dot(h, win_ref[:, H * dk:2 * H * dk])
    for hh in range(H):
        q0 = q[:, hh * dk:hh * dk + half]
        q1 = q[:, hh * dk + half:(hh + 1) * dk]
        q_s[:, hh * dk:hh * dk + half] = (q0 * cos - q1 * sin).astype(BF16)
        q_s[:, hh * dk + half:(hh + 1) * dk] = (q1 * cos + q0 * sin).astype(BF16)
        k0 = k[:, hh * dk:hh * dk + half]
        k1 = k[:, hh * dk + half:(hh + 1) * dk]
        k_s[:, hh * dk:hh * dk + half] = (k0 * cos - k1 * sin) * (dk ** -0.5)
        k_s[:, hh * dk + half:(hh + 1) * dk] = (k1 * cos + k0 * sin) * (dk ** -0.5)
    v_s[...] = _dot(h, win_ref[:, 2 * H * dk:2 * H * dk + H * dv]).astype(BF16)
    gate_s[...] = _dot(h, win_ref[:, 2 * H * dk + H * dv:])

    def chunk(c, carry):
        r0 = pl.multiple_of(c * C, C)
        rows = pl.ds(r0, C)
        for hh in range(H):
            qc = q_s[rows, hh * dk:(hh + 1) * dk]
            kc = k_s[rows, hh * dk:(hh + 1) * dk]
            vc = v_s[rows, hh * dv:(hh + 1) * dv]
            st = state_s[hh]
            scores = _dot_nt(qc, kc.astype(BF16)) * inner_ref[hh]
            inner = _dot(scores.astype(BF16), vc)
            cross = _dot(qc, st.astype(BF16)) * qdec_ref[hh]
            kd = (kc * kdec_ref[hh]).astype(BF16)
            state_s[hh] = st * cdec_ref[hh] + _dot_tn(kd, vc)
            out = inner + cross
            out = out * lax.rsqrt(jnp.mean(out * out, axis=-1, keepdims=True) + RMS_EPS)
            gt = gate_s[rows, hh * dv:(hh + 1) * dv]
            z_s[rows, hh * dv:(hh + 1) * dv] = (gt * jax.nn.sigmoid(gt) * out).astype(BF16)
        return carry

    lax.fori_loop(0, x.shape[0] // C, chunk, 0, unroll=True)
    o_ref[...] = x + _dot(z_s[...], wout_ref[...])


def _ret_tables():
    H, C = RET_HEADS, RET_CHUNK
    log_gamma = jnp.log(1.0 - 2.0 ** (-5.0 - jnp.arange(H, dtype=F32)))
    idx = jnp.arange(C, dtype=F32)
    diff = idx[:, None] - idx[None, :]
    inner = jnp.where(diff[None] >= 0,
                      jnp.exp(jnp.maximum(diff, 0.0)[None] * log_gamma[:, None, None]), 0.0)
    qdec = jnp.exp((idx[None, :] + 1.0) * log_gamma[:, None])[:, :, None]
    kdec = jnp.exp((C - 1.0 - idx[None, :]) * log_gamma[:, None])[:, :, None]
    cdec = jnp.exp(C * log_gamma)[:, None, None]
    return inner, qdec, kdec, cdec


def _rot_tables(s):
    pos = jnp.arange(s, dtype=F32)
    freq = 1.0 / (ROT_BASE ** jnp.linspace(0.0, 1.0, RET_DK // 2, dtype=F32))
    ang = pos[:, None] * freq[None, :]
    return jnp.cos(ang), jnp.sin(ang)


def _pair_split_columns(w, heads, dim):
    d = w.shape[0]
    w = w.reshape(d, heads, dim // 2, 2)
    return jnp.concatenate([w[..., 0], w[..., 1]], axis=-1).reshape(d, heads * dim)


def _retention(x, g, w_in, w_out):
    b, s, d = x.shape
    H, dk, dv = RET_HEADS, RET_DK, RET_DV
    tb = min(RET_ROWS, s)
    wq = _pair_split_columns(w_in[:, :H * dk], H, dk)
    wk = _pair_split_columns(w_in[:, H * dk:2 * H * dk], H, dk)
    win = jnp.concatenate([wq, wk, w_in[:, 2 * H * dk:]], axis=1).astype(BF16)
    wout = w_out.astype(BF16)
    cos, sin = _rot_tables(s)
    inner, qdec, kdec, cdec = _ret_tables()
    row = pl.BlockSpec((None, tb, d), lambda bi, i: (bi, i, 0))
    rot = pl.BlockSpec((tb, dk // 2), lambda bi, i: (i, 0))
    return pl.pallas_call(
        _ret_kernel,
        grid=(b, s // tb),
        in_specs=[row, _resident((1, d)), _resident(win.shape), _resident(wout.shape), rot, rot,
                  _resident(inner.shape), _resident(qdec.shape), _resident(kdec.shape),
                  _resident(cdec.shape)],
        out_specs=row,
        out_shape=jax.ShapeDtypeStruct((b, s, d), F32),
        scratch_shapes=[
            pltpu.VMEM((tb, H * dk), BF16),
            pltpu.VMEM((tb, H * dk), F32),
            pltpu.VMEM((tb, H * dv), BF16),
            pltpu.VMEM((tb, H * dv), F32),
            pltpu.VMEM((tb, H * dv), BF16),
            pltpu.VMEM((H, dk, dv), F32),
        ],
        compiler_params=_params(2),
        name="retention",
    )(x, g.reshape(1, d), win, wout, cos, sin, inner, qdec, kdec, cdec)


def _dsa_proj_kernel(x_ref, g_ref, win_ref, winw_ref, qn_ref, kvn_ref, kg_ref, kb_ref, wqi_ref,
                     wuq_ref, wuk_ref, qi_ref, wt_ref, ql_ref, kidx_ref, ckv_ref, ckvt_ref):
    RQ, RKV, HI, H, dh, TQ = DSA_Q_RANK, DSA_KV_RANK, IDX_HEADS, DSA_HEADS, DSA_HEAD_DIM, ATT_Q
    n_blk = x_ref.shape[0] // TQ
    h = _rms(x_ref[...], g_ref[...]).astype(BF16)
    proj = _dot(h, win_ref[...])
    c_q = _rms(proj[:, :RQ], qn_ref[...])
    c_kv = _rms(proj[:, RQ:RQ + RKV], kvn_ref[...])
    kx = proj[:, RQ + RKV:RQ + RKV + LANES]
    lane = lax.broadcasted_iota(jnp.int32, kx.shape, 1)
    mu = jnp.sum(kx, axis=-1, keepdims=True) * (1.0 / IDX_DIM)
    cen = jnp.where(lane < IDX_DIM, kx - mu, 0.0)
    var = jnp.sum(cen * cen, axis=-1, keepdims=True) * (1.0 / IDX_DIM)
    kidx_ref[...] = (cen * lax.rsqrt(var + RMS_EPS) * kg_ref[...] + kb_ref[...]).astype(BF16)
    ckv_ref[...] = c_kv.astype(BF16)
    ckvt_ref[0:RKV, :] = c_kv.T.astype(BF16)
    ones_row = lax.broadcasted_iota(jnp.int32, (BF16_ROWS, x_ref.shape[0]), 0) == 0
    ckvt_ref[RKV:, :] = jnp.where(ones_row, 1.0, 0.0).astype(BF16)
    w_t = _dot_nt(winw_ref[...], h) * (HI ** -0.5)
    cq = c_q.astype(BF16)
    qi_t = (_dot_nt(wqi_ref[...], cq) * (IDX_DIM ** -0.5)).astype(BF16)
    q = _dot(cq, wuq_ref[...]).astype(BF16)
    ql_t = [_dot_nt(wuk_ref[hh], q[:, hh * dh:(hh + 1) * dh]).astype(BF16) for hh in range(H)]
    for u in range(n_blk):
        cols = slice(u * TQ, (u + 1) * TQ)
        wt_ref[u] = w_t[0:HI, cols]
        for hh in range(HI):
            qi_ref[u, :, hh * TQ:(hh + 1) * TQ] = qi_t[hh * LANES:(hh + 1) * LANES, cols]
        for hh in range(H):
            ql_ref[u, :, hh * TQ:(hh + 1) * TQ] = ql_t[hh][:, cols]


def _dsa_proj(x2, g, w_in, q_norm, kv_norm, w_uq, w_qidx, kidx_g, kidx_b, w_uk):
    n, d = x2.shape
    RQ, RKV, HI, DI, H, TQ = DSA_Q_RANK, DSA_KV_RANK, IDX_HEADS, IDX_DIM, DSA_HEADS, ATT_Q
    pad_k = jnp.zeros((d, LANES - DI), F32)
    win = jnp.concatenate([w_in[:, :RQ + RKV + DI], pad_k], axis=1).astype(BF16)
    winw = jnp.pad(w_in[:, RQ + RKV + DI:].T, ((0, BF16_ROWS - HI), (0, 0))).astype(BF16)
    kg = jnp.pad(kidx_g, (0, LANES - DI)).reshape(1, LANES)
    kb = jnp.pad(kidx_b, (0, LANES - DI)).reshape(1, LANES)
    wqi = jnp.pad(w_qidx.reshape(RQ, HI, DI), ((0, 0), (0, 0), (0, LANES - DI)))
    wqi = wqi.reshape(RQ, HI * LANES).T.astype(BF16)
    wuq = w_uq.astype(BF16)
    wuk = w_uk.transpose(1, 0, 2).astype(BF16)
    tm = min(PROJ_ROWS, n)
    nb = tm // TQ
    row = lambda w: pl.BlockSpec((tm, w), lambda i: (i, 0))
    blk = lambda r, c: pl.BlockSpec((nb, r, c), lambda i: (i, 0, 0))
    return pl.pallas_call(
        _dsa_proj_kernel,
        grid=(n // tm,),
        in_specs=[row(d), _resident((1, d)), _resident(win.shape), _resident(winw.shape),
                  _resident((1, RQ)), _resident((1, RKV)), _resident((1, LANES)),
                  _resident((1, LANES)), _resident(wqi.shape), _resident(wuq.shape),
                  _resident(wuk.shape)],
        out_specs=[blk(LANES, HI * TQ), blk(HI, TQ), blk(RKV, H * TQ), row(LANES), row(RKV),
                   pl.BlockSpec((KV_ROWS, tm), lambda i: (0, i))],
        out_shape=[
            jax.ShapeDtypeStruct((n // TQ, LANES, HI * TQ), BF16),
            jax.ShapeDtypeStruct((n // TQ, HI, TQ), F32),
            jax.ShapeDtypeStruct((n // TQ, RKV, H * TQ), BF16),
            jax.ShapeDtypeStruct((n, LANES), BF16),
            jax.ShapeDtypeStruct((n, RKV), BF16),
            jax.ShapeDtypeStruct((KV_ROWS, n), BF16),
        ],
        compiler_params=_params(1),
        name="dsa_proj",
    )(x2, g.reshape(1, d), win, winw, q_norm.reshape(1, RQ), kv_norm.reshape(1, RKV), kg, kb, wqi,
      wuq, wuk)


def _fold(t, op):
    groups = t.shape[0] // SUBLANES
    group = lambda r: t[r * SUBLANES:(r + 1) * SUBLANES, :]
    acc = [group(w) for w in range(REDUCE_WAYS)]
    for r in range(REDUCE_WAYS, groups, REDUCE_WAYS):
        acc = [op(acc[w], group(r + w)) for w in range(REDUCE_WAYS)]
    while len(acc) > 1:
        acc = [op(acc[2 * w], acc[2 * w + 1]) for w in range(len(acc) // 2)]
    return acc[0]


def _fori_grouped(n, body, init, group=LOOP_GROUP):
    def grouped(q, carry):
        for u in range(group):
            carry = body(group * q + u, carry)
        return carry
    carry = lax.fori_loop(0, n // group, grouped, init)
    return lax.fori_loop(group * (n // group), n, body, carry)


def _dsa_attn_kernel(x_ref, qi_ref, wt_ref, ql_ref, kidx_ref, ckv_ref, ckvt_ref, wuvt_ref,
                     wout_ref, cmax_ref, tri_ref, o_ref, sc_s, lg_a, lg_b, p_a, p_b, bias_s, acc_s, m_s, *, top_k):
    TQ, TK, TH = ATT_Q, ATT_K, ATT_K // 2
    HI, H, RKV = IDX_HEADS, DSA_HEADS, DSA_KV_RANK
    i = pl.program_id(1)
    n_tiles = (i * TQ + TQ + TK - 1) // TK
    qpos = i * TQ + lax.broadcasted_iota(jnp.int32, (1, TQ), 1)
    keyh = lax.broadcasted_iota(jnp.int32, (TH, 1), 0)
    neg_inf = jnp.float32(-jnp.inf)
    kf = jnp.float32(top_k)

    def tile_rows(j):
        return pl.ds(pl.multiple_of(j * TK, TK), TK)

    w_t = wt_ref[...]
    last_half = 2 * n_tiles - 1

    def half_rows(jh):
        return pl.ds(pl.multiple_of(jh * TH, TH), TH)

    def score_dots(jh, buf):
        buf[...] = _dot(kidx_ref[half_rows(jh), :], qi_ref[...])

    def score_reduce(jh, buf, carry):
        rmax, rmin, c_ge0, c_gt0 = carry
        s = jnp.maximum(buf[:, 0:TQ], 0.0) * w_t[0:1, :]
        for hh in range(1, HI):
            s = s + jnp.maximum(buf[:, hh * TQ:(hh + 1) * TQ], 0.0) * w_t[hh:hh + 1, :]
        causal = keyh + jh * TH <= qpos
        s = jnp.where(causal, s, neg_inf)
        sc_s[half_rows(jh), :] = s
        rmax = jnp.maximum(rmax, _fold(s, jnp.maximum))
        rmin = jnp.minimum(rmin, _fold(jnp.where(causal, s, jnp.inf), jnp.minimum))
        c_ge0 = c_ge0 + _fold(jnp.where(s >= 0.0, 1.0, 0.0), jnp.add)
        c_gt0 = c_gt0 + _fold(jnp.where(s > 0.0, 1.0, 0.0), jnp.add)
        return rmax, rmin, c_ge0, c_gt0

    def score_pair(j, carry):
        score_dots(2 * j + 1, lg_b)
        carry = score_reduce(2 * j, lg_a, carry)
        score_dots(jnp.minimum(2 * j + 2, last_half), lg_a)
        return score_reduce(2 * j + 1, lg_b, carry)

    part = lambda v: jnp.full((SUBLANES, TQ), v, F32)
    score_dots(0, lg_a)
    rmax, rmin, c_ge0, c_gt0 = _fori_grouped(
        n_tiles, score_pair, (part(-jnp.inf), part(jnp.inf), part(0.0), part(0.0)))
    rmax = jnp.max(rmax, axis=0, keepdims=True)
    rmin = jnp.min(rmin, axis=0, keepdims=True)
    c_ge0 = jnp.sum(c_ge0, axis=0, keepdims=True)
    c_gt0 = jnp.sum(c_gt0, axis=0, keepdims=True)

    def count(pred):
        def body(j, acc):
            return acc + _fold(jnp.where(pred(sc_s[tile_rows(j), :], j), 1.0, 0.0), jnp.add)
        return jnp.sum(_fori_grouped(n_tiles, body, part(0.0)), axis=0, keepdims=True)

    n_valid = (qpos + 1).astype(F32)
    all_sel = n_valid <= kf
    zero_tie = (c_gt0 < kf) & (c_ge0 >= kf)
    positive = c_gt0 >= kf
    lo = jnp.where(positive | zero_tie, 0.0, rmin)
    hi = jnp.where(positive, 2.0 * rmax, 0.0)
    clo = jnp.where(positive | zero_tie, c_ge0, n_valid)
    chi = jnp.where(positive, 0.0, c_ge0)
    lo = jnp.where(all_sel, jnp.float32(jnp.finfo(jnp.float32).min), lo)
    clo = jnp.where(all_sel, n_valid, clo)
    done = all_sel | zero_tie | (clo == kf)
    one = jnp.ones((1, TQ), F32)
    log_target = math.log(top_k + 0.5)

    def pending(done):
        return jnp.sum(jnp.where(done, 0, 1))

    def search_cond(c):
        return (c[1] > 0) & (c[0] < SEARCH_CAP)

    def search_step(state):
        lo, hi, clo, chi, wl, wh, side, done_f = state
        done = done_f > 0.0
        mid = 0.5 * lo + 0.5 * hi
        fa = (jnp.log(clo) - log_target) * wl
        fb = (log_target - jnp.log(jnp.maximum(chi, 0.5))) * wh
        cand = lo + (hi - lo) * (fa / (fa + fb))
        cand = jnp.where((cand > lo) & (cand < hi), cand, mid)
        collapsed = (cand <= lo) | (cand >= hi)
        cm = count(lambda t, j: t >= cand)
        move = jnp.logical_not(done | collapsed)
        up = move & (cm >= kf)
        down = move & (cm < kf)
        wh = jnp.where(up, jnp.where(side > 0.0, 0.5 * wh, 1.0), jnp.where(down, 1.0, wh))
        wl = jnp.where(down, jnp.where(side < 0.0, 0.5 * wl, 1.0), jnp.where(up, 1.0, wl))
        side = jnp.where(up, 1.0, jnp.where(down, -1.0, side))
        lo = jnp.where(up, cand, lo)
        clo = jnp.where(up, cm, clo)
        hi = jnp.where(down, cand, hi)
        chi = jnp.where(down, cm, chi)
        done = done | collapsed | (clo == kf)
        return lo, hi, clo, chi, wl, wh, side, jnp.where(done, 1.0, 0.0)

    def search_body(c):
        state = search_step(c[2:])
        return (c[0] + 1, pending(state[-1] > 0.0)) + state

    state = (lo, hi, clo, chi, one, one, 0.0 * one, jnp.where(done, 1.0, 0.0))
    state = lax.fori_loop(0, SEARCH_UNCHECKED, lambda _, s: search_step(s), state)

    lo, hi, clo, chi, wl, wh, side, done_f = state
    inf = jnp.float32(jnp.inf)

    def insert(ranks, v):
        out = []
        for r in ranks:
            out.append(jnp.minimum(r, v))
            v = jnp.maximum(r, v)
        return out

    def smallest_body(j, ranks):
        t = sc_s[tile_rows(j), :]
        ranks = list(ranks)
        for g in range(TK // SUBLANES):
            v = t[g * SUBLANES:(g + 1) * SUBLANES, :]
            w = g % 2
            ranks[w] = tuple(insert(ranks[w], jnp.where(v >= lo, v, inf)))
        return tuple(ranks)

    empty = tuple(jnp.full((SUBLANES, TQ), inf, F32) for _ in range(FINISH_RANKS))
    ranks = lax.fori_loop(0, n_tiles, smallest_body, (empty, empty))
    final = [jnp.full((1, TQ), inf, F32) for _ in range(FINISH_RANKS)]
    for chain in ranks:
        for r in chain:
            for sub in range(SUBLANES):
                final = insert(final, r[sub:sub + 1, :])
    extra = clo - kf
    kth = final[0]
    below = -inf
    for e in range(1, FINISH_RANKS):
        kth = jnp.where(extra >= e, final[e], kth)
        below = jnp.where(extra == e, final[e - 1], below)
    near = (done_f <= 0.0) & (extra < FINISH_RANKS)
    lo = jnp.where(near, kth, lo)
    clo = jnp.where(near, jnp.where(below == kth, kf + 1.0, kf), clo)
    done_f = jnp.where(near, 1.0, done_f)
    state = (lo, hi, clo, chi, wl, wh, side, done_f)

    res = lax.while_loop(search_cond, search_body,
                         (jnp.int32(0), pending(state[-1] > 0.0)) + state)
    tau, clo = res[2], res[4]

    excess = clo > kf

    @pl.when(jnp.sum(jnp.where(excess, 1, 0)) > 0)
    def _():
        need = kf - count(lambda t, j: t > tau)

        def rank_and_drop(j, seen):
            t = sc_s[tile_rows(j), :]
            tied = t == tau
            ef = jnp.where(tied, 1.0, 0.0)
            e = ef.astype(BF16)
            pre = _dot(tri_ref[...], jnp.concatenate([e[:TH], e[TH:]], axis=1))
            seen_mid = seen + jnp.sum(_fold(ef[:TH], jnp.add), axis=0, keepdims=True)
            rank = jnp.concatenate([seen + pre[:, :TQ], seen_mid + pre[:, TQ:]], axis=0)
            sc_s[tile_rows(j), :] = jnp.where(excess & tied & (rank > need), neg_inf, t)
            return seen_mid + jnp.sum(_fold(ef[TH:], jnp.add), axis=0, keepdims=True)

        _fori_grouped(n_tiles, rank_and_drop, jnp.zeros((1, TQ), F32))

    c_exp = (DSA_HEAD_DIM ** -0.5) * math.log2(math.e)
    def attn_logits(jh, buf):
        buf[...] = _dot(ckv_ref[half_rows(jh), :], ql_ref[...])

    ql_f = ql_ref[...].astype(F32)
    bound = jnp.sqrt(jnp.sum(ql_f * ql_f, axis=0, keepdims=True)) * cmax_ref[...]

    def fast_values(jh, p_buf):
        cols_j = pl.ds(pl.multiple_of(jh * TH, TH), TH)
        acc_s[...] += _dot(ckvt_ref[:, cols_j], p_buf[...])

    def fast_half(jh, buf, p_buf, jh_next, buf_next, jh_prev, p_prev):
        keep = jnp.where(sc_s[half_rows(jh), :] >= tau, 1.0, 0.0).astype(BF16)
        keys_next = ckv_ref[half_rows(jh_next), :]
        vals_prev = ckvt_ref[:, pl.ds(pl.multiple_of(jh_prev * TH, TH), TH)]
        per_group = H // COLUMN_GROUPS
        for grp in range(COLUMN_GROUPS):
            gcols = slice(grp * per_group * TQ, (grp + 1) * per_group * TQ)
            buf_next[:, gcols] = _dot(keys_next, ql_ref[:, gcols])
            for hh in range(grp * per_group, (grp + 1) * per_group):
                cols = slice(hh * TQ, (hh + 1) * TQ)
                p = jnp.exp2((buf[:, cols] - bound[:, cols]) * c_exp).astype(BF16)
                p_buf[:, cols] = p * keep
            acc_s[:, gcols] += _dot(vals_prev, p_prev[:, gcols])

    def fast_pair(j, carry):
        fast_half(2 * j, lg_a, p_a, 2 * j + 1, lg_b, jnp.maximum(2 * j - 1, 0), p_b)
        fast_half(2 * j + 1, lg_b, p_b, jnp.minimum(2 * j + 2, last_half), lg_a, 2 * j, p_a)
        return carry

    acc_s[...] = jnp.zeros_like(acc_s)
    p_b[...] = jnp.zeros_like(p_b)
    attn_logits(0, lg_a)
    _fori_grouped(n_tiles, fast_pair, 0)
    fast_values(last_half, p_b)

    def write_output():
        o_lat = (acc_s[0:RKV, :] / acc_s[RKV:RKV + 1, :]).astype(BF16)
        o_t = jnp.concatenate(
            [_dot(wuvt_ref[hh], o_lat[:, hh * TQ:(hh + 1) * TQ]) for hh in range(H)], axis=0)
        o_ref[...] = x_ref[...] + _dot(o_t.T.astype(BF16), wout_ref[...])

    write_output()
    denom_bad = jnp.logical_not(acc_s[RKV:RKV + 1, :] >= DENOM_FLOOR)
    redo = jnp.sum(jnp.where(denom_bad, 1, 0)) > 0

    def attn_softmax(jh, buf, p_buf):
        bias_s[...] = jnp.where(sc_s[half_rows(jh), :] >= tau, 0.0, MASKED)
        alphas = []
        for hh in range(H):
            cols = slice(hh * TQ, (hh + 1) * TQ)
            m_old = m_s[hh:hh + 1, :]
            mx = _fold(buf[:, cols] + bias_s[...], jnp.maximum)
            m_new = jnp.maximum(m_old, jnp.max(mx, axis=0, keepdims=True))
            m_s[hh:hh + 1, :] = m_new
            alphas.append(jnp.exp2((m_old - m_new) * c_exp))
            p_buf[:, cols] = jnp.exp2((buf[:, cols] + bias_s[...] - m_new) * c_exp).astype(BF16)
        return jnp.concatenate(alphas, axis=1)

    def attn_values(jh, p_buf, alpha):
        cols_j = pl.ds(pl.multiple_of(jh * TH, TH), TH)
        acc_s[...] = acc_s[...] * alpha + _dot(ckvt_ref[:, cols_j], p_buf[...])

    def attn_pair(j, alpha_b):
        attn_logits(2 * j + 1, lg_b)
        alpha_a = attn_softmax(2 * j, lg_a, p_a)
        attn_values(jnp.maximum(2 * j - 1, 0), p_b, alpha_b)
        attn_logits(jnp.minimum(2 * j + 2, last_half), lg_a)
        alpha_b = attn_softmax(2 * j + 1, lg_b, p_b)
        attn_values(2 * j, p_a, alpha_a)
        return alpha_b

    @pl.when(redo)
    def _():
        m_s[...] = jnp.full_like(m_s, MASKED)
        acc_s[...] = jnp.zeros_like(acc_s)
        p_b[...] = jnp.zeros_like(p_b)
        attn_logits(0, lg_a)
        alpha_b = lax.fori_loop(0, n_tiles, attn_pair, jnp.ones((1, H * TQ), F32))
        attn_values(last_half, p_b, alpha_b)
        write_output()


def _dsa_attn(x, qi, wt, ql, kidx, ckv, ckvt, kv_norm, w_uv, w_out):
    b, s, d = x.shape
    H, RKV, HI, TQ = DSA_HEADS, DSA_KV_RANK, IDX_HEADS, ATT_Q
    nq = s // TQ
    top_k = min(TOPK_MAX, s // 4)
    wuvt = w_uv.transpose(1, 2, 0).astype(BF16)
    wout = w_out.astype(BF16)
    cmax = (math.sqrt(RKV) * jnp.max(jnp.abs(kv_norm))).reshape(1, 1).astype(F32)
    tri = jnp.tril(jnp.ones((ATT_K // 2, ATT_K // 2), F32)).astype(BF16)
    qrow = pl.BlockSpec((None, TQ, d), lambda bi, i: (bi, i, 0))
    qblk = lambda r, c: pl.BlockSpec((None, r, c), lambda bi, i: (bi * nq + i, 0, 0))
    seq = lambda width: pl.BlockSpec((s, width), lambda bi, i: (bi, 0))
    return pl.pallas_call(
        functools.partial(_dsa_attn_kernel, top_k=top_k),
        grid=(b, nq),
        in_specs=[qrow, qblk(LANES, HI * TQ), qblk(HI, TQ), qblk(RKV, H * TQ), seq(LANES), seq(RKV),
                  pl.BlockSpec((KV_ROWS, s), lambda bi, i: (0, bi)),
                  _resident(wuvt.shape), _resident(wout.shape), _resident((1, 1)),
                  _resident(tri.shape)],
        out_specs=qrow,
        out_shape=jax.ShapeDtypeStruct((b, s, d), F32),
        scratch_shapes=[
            pltpu.VMEM((s, TQ), F32),
            pltpu.VMEM((ATT_K // 2, H * TQ), F32),
            pltpu.VMEM((ATT_K // 2, H * TQ), F32),
            pltpu.VMEM((ATT_K // 2, H * TQ), BF16),
            pltpu.VMEM((ATT_K // 2, H * TQ), BF16),
            pltpu.VMEM((ATT_K // 2, TQ), F32),
            pltpu.VMEM((KV_ROWS, H * TQ), F32),
            pltpu.VMEM((H, TQ), F32),
        ],
        compiler_params=_params(2),
        name="dsa_attn",
    )(x, qi, wt, ql, kidx, ckv, ckvt, wuvt, wout, cmax, tri)


def _dsa(x, g, w_in, q_norm, kv_norm, w_uq, w_qidx, kidx_g, kidx_b, w_uk, w_uv, w_out):
    b, s, d = x.shape
    qi, wt, ql, kidx, ckv, ckvt = _dsa_proj(x.reshape(b * s, d), g, w_in, q_norm, kv_norm, w_uq,
                                            w_qidx, kidx_g, kidx_b, w_uk)
    return _dsa_attn(x, qi, wt, ql, kidx, ckv, ckvt, kv_norm, w_uv, w_out)


def kernel(x, norm_ffn1, w_ffn1_in, w_ffn1_out, norm_mix, norm_ffn2, w_ffn2_in, w_ffn2_out, ret_w_in,
           ret_w_out, dsa_w_in, dsa_q_norm, dsa_kv_norm, dsa_w_uq, dsa_w_qidx, dsa_kidx_g, dsa_kidx_b,
           dsa_w_uk, dsa_w_uv, dsa_w_out, final_norm):
    b, s, d = x.shape
    depth = norm_ffn1.shape[0]
    for layer in range(depth):
        x = _ffn(x.reshape(b * s, d), norm_ffn1[layer], w_ffn1_in[layer], w_ffn1_out[layer])
        x = x.reshape(b, s, d)
        j = layer // 2
        if layer % 2 == 0:
            x = _retention(x, norm_mix[layer], ret_w_in[j], ret_w_out[j])
        else:
            x = _dsa(x, norm_mix[layer], dsa_w_in[j], dsa_q_norm[j], dsa_kv_norm[j], dsa_w_uq[j],
                     dsa_w_qidx[j], dsa_kidx_g[j], dsa_kidx_b[j], dsa_w_uk[j], dsa_w_uv[j],
                     dsa_w_out[j])
        last = layer == depth - 1
        x = _ffn(x.reshape(b * s, d), norm_ffn2[layer], w_ffn2_in[layer], w_ffn2_out[layer],
                 final_g=final_norm if last else None)
        x = x.reshape(b, s, d)
    return x
```

```python
import functools
import math

import jax
import jax.numpy as jnp
from jax import lax
from jax.experimental import pallas as pl
from jax.experimental.pallas import tpu as pltpu

F32 = jnp.float32
BF16 = jnp.bfloat16

D_MODEL = 1024
D_FF = 2816
RMS_EPS = 1e-6

RET_HEADS = 4
RET_DK = D_MODEL // RET_HEADS
RET_DV = 2 * RET_DK
RET_CHUNK = 128
ROT_BASE = 10000.0

DSA_HEADS = 8
DSA_HEAD_DIM = D_MODEL // DSA_HEADS
DSA_Q_RANK = 256
DSA_KV_RANK = 256
IDX_HEADS = 8
IDX_DIM = 64
TOPK_MAX = 256

LANES = 128
SUBLANES = 8
BF16_ROWS = 16
VMEM_LIMIT_BYTES = 56 * 1024 * 1024

FFN_ROWS = 512
FFN_CHUNK = 256
RET_ROWS = 512
PROJ_ROWS = 512
ATT_Q = LANES
ATT_K = 512
KV_ROWS = DSA_KV_RANK + BF16_ROWS
MASKED = -1e30
SEARCH_CAP = 400
SEARCH_UNCHECKED = 9
FINISH_RANKS = 4
DENOM_FLOOR = 2.0 ** -80
REDUCE_WAYS = 4
LOOP_GROUP = 4
COLUMN_GROUPS = 4


def _resident(shape):
    nd = len(shape)
    return pl.BlockSpec(shape, lambda *_: (0,) * nd, pipeline_mode=pl.Buffered(1))


def _params(n_grid, flags=None):
    return pltpu.CompilerParams(
        dimension_semantics=("arbitrary",) * n_grid, vmem_limit_bytes=VMEM_LIMIT_BYTES, flags=flags)


def _rms(x, g):
    y = x * lax.rsqrt(jnp.mean(x * x, axis=-1, keepdims=True) + RMS_EPS)
    return y * g


def _dot(a, b):
    return jnp.dot(a, b, preferred_element_type=F32)


def _dot_nt(a, b):
    return lax.dot_general(a, b, (((1,), (1,)), ((), ())), preferred_element_type=F32)


def _dot_tn(a, b):
    return lax.dot_general(a, b, (((0,), (0,)), ((), ())), preferred_element_type=F32)


def _ffn_kernel(x_ref, g_ref, win_ref, wo_ref, *rest, final):
    if final:
        fg_ref, o_ref, acc_ref = rest
    else:
        o_ref, acc_ref = rest
    x = x_ref[...]
    h = _rms(x, g_ref[...]).astype(BF16)
    acc_ref[...] = jnp.zeros_like(acc_ref)
    for c in range(D_FF // FFN_CHUNK):
        cols = slice(c * FFN_CHUNK, (c + 1) * FFN_CHUNK)
        gate = _dot(h, win_ref[:, cols])
        up = _dot(h, win_ref[:, D_FF + c * FFN_CHUNK:D_FF + (c + 1) * FFN_CHUNK])
        a = (gate * jax.nn.sigmoid(gate) * up).astype(BF16)
        acc_ref[...] += _dot(a, wo_ref[cols, :])
    y = x + 0.5 * acc_ref[...]
    if final:
        y = _rms(y, fg_ref[...])
    o_ref[...] = y


def _ffn(x2, g, w_in, w_out, final_g=None):
    n, d = x2.shape
    win = w_in.astype(BF16)
    wo = w_out.astype(BF16)
    row = pl.BlockSpec((FFN_ROWS, d), lambda i: (i, 0))
    ins = [x2, g.reshape(1, d), win, wo]
    specs = [row, _resident((1, d)), _resident(win.shape), _resident(wo.shape)]
    if final_g is not None:
        ins.append(final_g.reshape(1, d))
        specs.append(_resident((1, d)))
    return pl.pallas_call(
        functools.partial(_ffn_kernel, final=final_g is not None),
        grid=(n // FFN_ROWS,),
        in_specs=specs,
        out_specs=row,
        out_shape=jax.ShapeDtypeStruct((n, d), F32),
        scratch_shapes=[pltpu.VMEM((FFN_ROWS, d), F32)],
        compiler_params=_params(1),
        name="ffn_final" if final_g is not None else "ffn",
    )(*ins)


def _ret_kernel(x_ref, g_ref, win_ref, wout_ref, cos_ref, sin_ref, inner_ref, qdec_ref, kdec_ref,
                cdec_ref, o_ref, q_s, k_s, v_s, gate_s, z_s, state_s):
    H, dk, dv, C = RET_HEADS, RET_DK, RET_DV, RET_CHUNK
    half = dk // 2

    @pl.when(pl.program_id(1) == 0)
    def _():
        state_s[...] = jnp.zeros_like(state_s)

    x = x_ref[...]
    h = _rms(x, g_ref[...]).astype(BF16)
    cos = cos_ref[...]
    sin = sin_ref[...]
    q = _dot(h, win_ref[:, 0:H * dk])
    k = _dot(h, win_ref[:, H * dk:2 * H * dk])
    for hh in range(H):
        q0 = q[:, hh * dk:hh * dk + half]
        q1 = q[:, hh * dk + half:(hh + 1) * dk]
        q_s[:, hh * dk:hh * dk + half] = (q0 * cos - q1 * sin).astype(BF16)
        q_s[:, hh * dk + half:(hh + 1) * dk] = (q1 * cos + q0 * sin).astype(BF16)
        k0 = k[:, hh * dk:hh * dk + half]
        k1 = k[:, hh * dk + half:(hh + 1) * dk]
        k_s[:, hh * dk:hh * dk + half] = (k0 * cos - k1 * sin) * (dk ** -0.5)
        k_s[:, hh * dk + half:(hh + 1) * dk] = (k1 * cos + k0 * sin) * (dk ** -0.5)
    v_s[...] = _dot(h, win_ref[:, 2 * H * dk:2 * H * dk + H * dv]).astype(BF16)
    gate_s[...] = _dot(h, win_ref[:, 2 * H * dk + H * dv:])

    def chunk(c, carry):
        r0 = pl.multiple_of(c * C, C)
        rows = pl.ds(r0, C)
        for hh in range(H):
            qc = q_s[rows, hh * dk:(hh + 1) * dk]
            kc = k_s[rows, hh * dk:(hh + 1) * dk]
            vc = v_s[rows, hh * dv:(hh + 1) * dv]
            st = state_s[hh]
            scores = _dot_nt(qc, kc.astype(BF16)) * inner_ref[hh]
            inner = _dot(scores.astype(BF16), vc)
            cross = _dot(qc, st.astype(BF16)) * qdec_ref[hh]
            kd = (kc * kdec_ref[hh]).astype(BF16)
            state_s[hh] = st * cdec_ref[hh] + _dot_tn(kd, vc)
            out = inner + cross
            out = out * lax.rsqrt(jnp.mean(out * out, axis=-1, keepdims=True) + RMS_EPS)
            gt = gate_s[rows, hh * dv:(hh + 1) * dv]
            z_s[rows, hh * dv:(hh + 1) * dv] = (gt * jax.nn.sigmoid(gt) * out).astype(BF16)
        return carry

    lax.fori_loop(0, x.shape[0] // C, chunk, 0, unroll=True)
    o_ref[...] = x + _dot(z_s[...], wout_ref[...])


def _ret_tables():
    H, C = RET_HEADS, RET_CHUNK
    log_gamma = jnp.log(1.0 - 2.0 ** (-5.0 - jnp.arange(H, dtype=F32)))
    idx = jnp.arange(C, dtype=F32)
    diff = idx[:, None] - idx[None, :]
    inner = jnp.where(diff[None] >= 0,
                      jnp.exp(jnp.maximum(diff, 0.0)[None] * log_gamma[:, None, None]), 0.0)
    qdec = jnp.exp((idx[None, :] + 1.0) * log_gamma[:, None])[:, :, None]
    kdec = jnp.exp((C - 1.0 - idx[None, :]) * log_gamma[:, None])[:, :, None]
    cdec = jnp.exp(C * log_gamma)[:, None, None]
    return inner, qdec, kdec, cdec


def _rot_tables(s):
    pos = jnp.arange(s, dtype=F32)
    freq = 1.0 / (ROT_BASE ** jnp.linspace(0.0, 1.0, RET_DK // 2, dtype=F32))
    ang = pos[:, None] * freq[None, :]
    return jnp.cos(ang), jnp.sin(ang)


def _pair_split_columns(w, heads, dim):
    d = w.shape[0]
    w = w.reshape(d, heads, dim // 2, 2)
    return jnp.concatenate([w[..., 0], w[..., 1]], axis=-1).reshape(d, heads * dim)


def _retention(x, g, w_in, w_out):
    b, s, d = x.shape
    H, dk, dv = RET_HEADS, RET_DK, RET_DV
    tb = min(RET_ROWS, s)
    wq = _pair_split_columns(w_in[:, :H * dk], H, dk)
    wk = _pair_split_columns(w_in[:, H * dk:2 * H * dk], H, dk)
    win = jnp.concatenate([wq, wk, w_in[:, 2 * H * dk:]], axis=1).astype(BF16)
    wout = w_out.astype(BF16)
    cos, sin = _rot_tables(s)
    inner, qdec, kdec, cdec = _ret_tables()
    row = pl.BlockSpec((None, tb, d), lambda bi, i: (bi, i, 0))
    rot = pl.BlockSpec((tb, dk // 2), lambda bi, i: (i, 0))
    return pl.pallas_call(
        _ret_kernel,
        grid=(b, s // tb),
        in_specs=[row, _resident((1, d)), _resident(win.shape), _resident(wout.shape), rot, rot,
                  _resident(inner.shape), _resident(qdec.shape), _resident(kdec.shape),
                  _resident(cdec.shape)],
        out_specs=row,
        out_shape=jax.ShapeDtypeStruct((b, s, d), F32),
        scratch_shapes=[
            pltpu.VMEM((tb, H * dk), BF16),
            pltpu.VMEM((tb, H * dk), F32),
            pltpu.VMEM((tb, H * dv), BF16),
            pltpu.VMEM((tb, H * dv), F32),
            pltpu.VMEM((tb, H * dv), BF16),
            pltpu.VMEM((H, dk, dv), F32),
        ],
        compiler_params=_params(2),
        name="retention",
    )(x, g.reshape(1, d), win, wout, cos, sin, inner, qdec, kdec, cdec)


def _dsa_proj_kernel(x_ref, g_ref, win_ref, winw_ref, qn_ref, kvn_ref, kg_ref, kb_ref, wqi_ref,
                     wuq_ref, wuk_ref, qi_ref, wt_ref, ql_ref, kidx_ref, ckv_ref, ckvt_ref, qlen_ref):
    RQ, RKV, HI, H, dh, TQ = DSA_Q_RANK, DSA_KV_RANK, IDX_HEADS, DSA_HEADS, DSA_HEAD_DIM, ATT_Q
    n_blk = x_ref.shape[0] // TQ
    h = _rms(x_ref[...], g_ref[...]).astype(BF16)
    proj = _dot(h, win_ref[...])
    c_q = _rms(proj[:, :RQ], qn_ref[...])
    c_kv = _rms(proj[:, RQ:RQ + RKV], kvn_ref[...])
    kx = proj[:, RQ + RKV:RQ + RKV + LANES]
    lane = lax.broadcasted_iota(jnp.int32, kx.shape, 1)
    mu = jnp.sum(kx, axis=-1, keepdims=True) * (1.0 / IDX_DIM)
    cen = jnp.where(lane < IDX_DIM, kx - mu, 0.0)
    var = jnp.sum(cen * cen, axis=-1, keepdims=True) * (1.0 / IDX_DIM)
    kidx_ref[...] = (cen * lax.rsqrt(var + RMS_EPS) * kg_ref[...] + kb_ref[...]).astype(BF16)
    ckv_ref[...] = c_kv.astype(BF16)
    ckvt_ref[0:RKV, :] = c_kv.T.astype(BF16)
    ones_row = lax.broadcasted_iota(jnp.int32, (BF16_ROWS, x_ref.shape[0]), 0) == 0
    ckvt_ref[RKV:, :] = jnp.where(ones_row, 1.0, 0.0).astype(BF16)
    w_t = _dot_nt(winw_ref[...], h) * (HI ** -0.5)
    cq = c_q.astype(BF16)
    qi_t = (_dot_nt(wqi_ref[...], cq) * (IDX_DIM ** -0.5)).astype(BF16)
    q = _dot(cq, wuq_ref[...]).astype(BF16)
    ql_t = [_dot_nt(wuk_ref[hh], q[:, hh * dh:(hh + 1) * dh]).astype(BF16) for hh in range(H)]
    ql_norm = [jnp.sqrt(jnp.sum(jnp.square(t.astype(F32)), axis=0, keepdims=True)) for t in ql_t]
    for u in range(n_blk):
        cols = slice(u * TQ, (u + 1) * TQ)
        wt_ref[u] = w_t[0:HI, cols]
        for hh in range(HI):
            qi_ref[u, :, hh * TQ:(hh + 1) * TQ] = qi_t[hh * LANES:(hh + 1) * LANES, cols]
        for hh in range(H):
            ql_ref[u, :, hh * TQ:(hh + 1) * TQ] = ql_t[hh][:, cols]
            qlen_ref[u, :, hh * TQ:(hh + 1) * TQ] = ql_norm[hh][:, cols]


def _dsa_proj(x2, g, w_in, q_norm, kv_norm, w_uq, w_qidx, kidx_g, kidx_b, w_uk):
    n, d = x2.shape
    RQ, RKV, HI, DI, H, TQ = DSA_Q_RANK, DSA_KV_RANK, IDX_HEADS, IDX_DIM, DSA_HEADS, ATT_Q
    pad_k = jnp.zeros((d, LANES - DI), F32)
    win = jnp.concatenate([w_in[:, :RQ + RKV + DI], pad_k], axis=1).astype(BF16)
    winw = jnp.pad(w_in[:, RQ + RKV + DI:].T, ((0, BF16_ROWS - HI), (0, 0))).astype(BF16)
    kg = jnp.pad(kidx_g, (0, LANES - DI)).reshape(1, LANES)
    kb = jnp.pad(kidx_b, (0, LANES - DI)).reshape(1, LANES)
    wqi = jnp.pad(w_qidx.reshape(RQ, HI, DI), ((0, 0), (0, 0), (0, LANES - DI)))
    wqi = wqi.reshape(RQ, HI * LANES).T.astype(BF16)
    wuq = w_uq.astype(BF16)
    wuk = w_uk.transpose(1, 0, 2).astype(BF16)
    tm = min(PROJ_ROWS, n)
    nb = tm // TQ
    row = lambda w: pl.BlockSpec((tm, w), lambda i: (i, 0))
    blk = lambda r, c: pl.BlockSpec((nb, r, c), lambda i: (i, 0, 0))
    return pl.pallas_call(
        _dsa_proj_kernel,
        grid=(n // tm,),
        in_specs=[row(d), _resident((1, d)), _resident(win.shape), _resident(winw.shape),
                  _resident((1, RQ)), _resident((1, RKV)), _resident((1, LANES)),
                  _resident((1, LANES)), _resident(wqi.shape), _resident(wuq.shape),
                  _resident(wuk.shape)],
        out_specs=[blk(LANES, HI * TQ), blk(HI, TQ), blk(RKV, H * TQ), row(LANES), row(RKV),
                   pl.BlockSpec((KV_ROWS, tm), lambda i: (0, i)), blk(1, H * TQ)],
        out_shape=[
            jax.ShapeDtypeStruct((n // TQ, LANES, HI * TQ), BF16),
            jax.ShapeDtypeStruct((n // TQ, HI, TQ), F32),
            jax.ShapeDtypeStruct((n // TQ, RKV, H * TQ), BF16),
            jax.ShapeDtypeStruct((n, LANES), BF16),
            jax.ShapeDtypeStruct((n, RKV), BF16),
            jax.ShapeDtypeStruct((KV_ROWS, n), BF16),
            jax.ShapeDtypeStruct((n // TQ, 1, H * TQ), F32),
        ],
        compiler_params=_params(1),
        name="dsa_proj",
    )(x2, g.reshape(1, d), win, winw, q_norm.reshape(1, RQ), kv_norm.reshape(1, RKV), kg, kb, wqi,
      wuq, wuk)


def _fold(t, op):
    groups = t.shape[0] // SUBLANES
    group = lambda r: t[r * SUBLANES:(r + 1) * SUBLANES, :]
    acc = [group(w) for w in range(REDUCE_WAYS)]
    for r in range(REDUCE_WAYS, groups, REDUCE_WAYS):
        acc = [op(acc[w], group(r + w)) for w in range(REDUCE_WAYS)]
    while len(acc) > 1:
        acc = [op(acc[2 * w], acc[2 * w + 1]) for w in range(len(acc) // 2)]
    return acc[0]


def _fori_grouped(n, body, init, group=LOOP_GROUP):
    def grouped(q, carry):
        for u in range(group):
            carry = body(group * q + u, carry)
        return carry
    carry = lax.fori_loop(0, n // group, grouped, init)
    return lax.fori_loop(group * (n // group), n, body, carry)


def _dsa_attn_kernel(x_ref, qi_ref, wt_ref, ql_ref, qn_ref, kidx_ref, ckv_ref, ckvt_ref, wuvt_ref,
                     wout_ref, cmax_ref, tri_ref, o_ref, sc_s, lg_a, lg_b, p_a, p_b, bias_s, acc_s, m_s, *, top_k):
    TQ, TK, TH = ATT_Q, ATT_K, ATT_K // 2
    HI, H, RKV = IDX_HEADS, DSA_HEADS, DSA_KV_RANK
    i = pl.program_id(1)
    n_tiles = (i * TQ + TQ + TK - 1) // TK
    qpos = i * TQ + lax.broadcasted_iota(jnp.int32, (1, TQ), 1)
    keyh = lax.broadcasted_iota(jnp.int32, (TH, 1), 0)
    neg_inf = jnp.float32(-jnp.inf)
    kf = jnp.float32(top_k)

    def tile_rows(j):
        return pl.ds(pl.multiple_of(j * TK, TK), TK)

    w_t = wt_ref[...]
    last_half = 2 * n_tiles - 1

    def half_rows(jh):
        return pl.ds(pl.multiple_of(jh * TH, TH), TH)

    def score_dots(jh, buf):
        buf[...] = _dot(kidx_ref[half_rows(jh), :], qi_ref[...])

    def score_reduce(jh, buf, carry):
        rmax, rmin, c_ge0, c_gt0 = carry
        s = jnp.maximum(buf[:, 0:TQ], 0.0) * w_t[0:1, :]
        for hh in range(1, HI):
            s = s + jnp.maximum(buf[:, hh * TQ:(hh + 1) * TQ], 0.0) * w_t[hh:hh + 1, :]
        causal = keyh + jh * TH <= qpos
        s = jnp.where(causal, s, neg_inf)
        sc_s[half_rows(jh), :] = s
        rmax = jnp.maximum(rmax, _fold(s, jnp.maximum))
        rmin = jnp.minimum(rmin, _fold(jnp.where(causal, s, jnp.inf), jnp.minimum))
        c_ge0 = c_ge0 + _fold(jnp.where(s >= 0.0, 1.0, 0.0), jnp.add)
        c_gt0 = c_gt0 + _fold(jnp.where(s > 0.0, 1.0, 0.0), jnp.add)
        return rmax, rmin, c_ge0, c_gt0

    def score_pair(j, carry):
        score_dots(2 * j + 1, lg_b)
        carry = score_reduce(2 * j, lg_a, carry)
        score_dots(jnp.minimum(2 * j + 2, last_half), lg_a)
        return score_reduce(2 * j + 1, lg_b, carry)

    part = lambda v: jnp.full((SUBLANES, TQ), v, F32)
    score_dots(0, lg_a)
    rmax, rmin, c_ge0, c_gt0 = _fori_grouped(
        n_tiles, score_pair, (part(-jnp.inf), part(jnp.inf), part(0.0), part(0.0)))
    rmax = jnp.max(rmax, axis=0, keepdims=True)
    rmin = jnp.min(rmin, axis=0, keepdims=True)
    c_ge0 = jnp.sum(c_ge0, axis=0, keepdims=True)
    c_gt0 = jnp.sum(c_gt0, axis=0, keepdims=True)

    def count(pred):
        def body(j, acc):
            return acc + _fold(jnp.where(pred(sc_s[tile_rows(j), :], j), 1.0, 0.0), jnp.add)
        return jnp.sum(_fori_grouped(n_tiles, body, part(0.0)), axis=0, keepdims=True)

    n_valid = (qpos + 1).astype(F32)
    all_sel = n_valid <= kf
    zero_tie = (c_gt0 < kf) & (c_ge0 >= kf)
    positive = c_gt0 >= kf
    lo = jnp.where(positive | zero_tie, 0.0, rmin)
    hi = jnp.where(positive, 2.0 * rmax, 0.0)
    clo = jnp.where(positive | zero_tie, c_ge0, n_valid)
    chi = jnp.where(positive, 0.0, c_ge0)
    lo = jnp.where(all_sel, jnp.float32(jnp.finfo(jnp.float32).min), lo)
    clo = jnp.where(all_sel, n_valid, clo)
    done = all_sel | zero_tie | (clo == kf)
    one = jnp.ones((1, TQ), F32)
    log_target = math.log(top_k + 0.5)

    def pending(done):
        return jnp.sum(jnp.where(done, 0, 1))

    def search_cond(c):
        return (c[1] > 0) & (c[0] < SEARCH_CAP)

    def search_step(state):
        lo, hi, clo, chi, wl, wh, side, done_f = state
        done = done_f > 0.0
        mid = 0.5 * lo + 0.5 * hi
        fa = (jnp.log(clo) - log_target) * wl
        fb = (log_target - jnp.log(jnp.maximum(chi, 0.5))) * wh
        cand = lo + (hi - lo) * (fa / (fa + fb))
        cand = jnp.where((cand > lo) & (cand < hi), cand, mid)
        collapsed = (cand <= lo) | (cand >= hi)
        cm = count(lambda t, j: t >= cand)
        move = jnp.logical_not(done | collapsed)
        up = move & (cm >= kf)
        down = move & (cm < kf)
        wh = jnp.where(up, jnp.where(side > 0.0, 0.5 * wh, 1.0), jnp.where(down, 1.0, wh))
        wl = jnp.where(down, jnp.where(side < 0.0, 0.5 * wl, 1.0), jnp.where(up, 1.0, wl))
        side = jnp.where(up, 1.0, jnp.where(down, -1.0, side))
        lo = jnp.where(up, cand, lo)
        clo = jnp.where(up, cm, clo)
        hi = jnp.where(down, cand, hi)
        chi = jnp.where(down, cm, chi)
        done = done | collapsed | (clo == kf)
        return lo, hi, clo, chi, wl, wh, side, jnp.where(done, 1.0, 0.0)

    def search_body(c):
        state = search_step(c[2:])
        return (c[0] + 1, pending(state[-1] > 0.0)) + state

    state = (lo, hi, clo, chi, one, one, 0.0 * one, jnp.where(done, 1.0, 0.0))
    state = lax.fori_loop(0, SEARCH_UNCHECKED, lambda _, s: search_step(s), state)

    lo, hi, clo, chi, wl, wh, side, done_f = state
    inf = jnp.float32(jnp.inf)

    def insert(ranks, v):
        out = []
        for r in ranks:
            out.append(jnp.minimum(r, v))
            v = jnp.maximum(r, v)
        return out

    def smallest_body(j, ranks):
        t = sc_s[tile_rows(j), :]
        ranks = list(ranks)
        for g in range(TK // SUBLANES):
            v = t[g * SUBLANES:(g + 1) * SUBLANES, :]
            w = g % 2
            ranks[w] = tuple(insert(ranks[w], jnp.where(v >= lo, v, inf)))
        return tuple(ranks)

    empty = tuple(jnp.full((SUBLANES, TQ), inf, F32) for _ in range(FINISH_RANKS))
    ranks = lax.fori_loop(0, n_tiles, smallest_body, (empty, empty))
    final = [jnp.full((1, TQ), inf, F32) for _ in range(FINISH_RANKS)]
    for chain in ranks:
        for r in chain:
            for sub in range(SUBLANES):
                final = insert(final, r[sub:sub + 1, :])
    extra = clo - kf
    kth = final[0]
    below = -inf
    for e in range(1, FINISH_RANKS):
        kth = jnp.where(extra >= e, final[e], kth)
        below = jnp.where(extra == e, final[e - 1], below)
    near = (done_f <= 0.0) & (extra < FINISH_RANKS)
    lo = jnp.where(near, kth, lo)
    clo = jnp.where(near, jnp.where(below == kth, kf + 1.0, kf), clo)
    done_f = jnp.where(near, 1.0, done_f)
    state = (lo, hi, clo, chi, wl, wh, side, done_f)

    res = lax.while_loop(search_cond, search_body,
                         (jnp.int32(0), pending(state[-1] > 0.0)) + state)
    tau, clo = res[2], res[4]

    excess = clo > kf

    @pl.when(jnp.sum(jnp.where(excess, 1, 0)) > 0)
    def _():
        need = kf - count(lambda t, j: t > tau)

        def rank_and_drop(j, seen):
            t = sc_s[tile_rows(j), :]
            tied = t == tau
            ef = jnp.where(tied, 1.0, 0.0)
            e = ef.astype(BF16)
            pre = _dot(tri_ref[...], jnp.concatenate([e[:TH], e[TH:]], axis=1))
            seen_mid = seen + jnp.sum(_fold(ef[:TH], jnp.add), axis=0, keepdims=True)
            rank = jnp.concatenate([seen + pre[:, :TQ], seen_mid + pre[:, TQ:]], axis=0)
            sc_s[tile_rows(j), :] = jnp.where(excess & tied & (rank > need), neg_inf, t)
            return seen_mid + jnp.sum(_fold(ef[TH:], jnp.add), axis=0, keepdims=True)

        _fori_grouped(n_tiles, rank_and_drop, jnp.zeros((1, TQ), F32))

    c_exp = (DSA_HEAD_DIM ** -0.5) * math.log2(math.e)
    def attn_logits(jh, buf):
        buf[...] = _dot(ckv_ref[half_rows(jh), :], ql_ref[...])

    bound = qn_ref[...] * cmax_ref[...]

    def fast_values(jh, p_buf):
        cols_j = pl.ds(pl.multiple_of(jh * TH, TH), TH)
        acc_s[...] += _dot(ckvt_ref[:, cols_j], p_buf[...])

    def fast_half(jh, buf, p_buf, jh_next, buf_next, jh_prev, p_prev):
        keep = jnp.where(sc_s[half_rows(jh), :] >= tau, 1.0, 0.0).astype(BF16)
        keys_next = ckv_ref[half_rows(jh_next), :]
        vals_prev = ckvt_ref[:, pl.ds(pl.multiple_of(jh_prev * TH, TH), TH)]
        per_group = H // COLUMN_GROUPS
        for grp in range(COLUMN_GROUPS):
            gcols = slice(grp * per_group * TQ, (grp + 1) * per_group * TQ)
            buf_next[:, gcols] = _dot(keys_next, ql_ref[:, gcols])
            for hh in range(grp * per_group, (grp + 1) * per_group):
                cols = slice(hh * TQ, (hh + 1) * TQ)
                p = jnp.exp2((buf[:, cols] - bound[:, cols]) * c_exp).astype(BF16)
                p_buf[:, cols] = p * keep
            acc_s[:, gcols] += _dot(vals_prev, p_prev[:, gcols])

    def fast_pair(j, carry):
        fast_half(2 * j, lg_a, p_a, 2 * j + 1, lg_b, jnp.maximum(2 * j - 1, 0), p_b)
        fast_half(2 * j + 1, lg_b, p_b, jnp.minimum(2 * j + 2, last_half), lg_a, 2 * j, p_a)
        return carry

    acc_s[...] = jnp.zeros_like(acc_s)
    p_b[...] = jnp.zeros_like(p_b)
    attn_logits(0, lg_a)
    _fori_grouped(n_tiles, fast_pair, 0)
    fast_values(last_half, p_b)

    def write_output():
        o_lat = (acc_s[0:RKV, :] / acc_s[RKV:RKV + 1, :]).astype(BF16)
        o_t = jnp.concatenate(
            [_dot(wuvt_ref[hh], o_lat[:, hh * TQ:(hh + 1) * TQ]) for hh in range(H)], axis=0)
        o_ref[...] = x_ref[...] + _dot(o_t.T.astype(BF16), wout_ref[...])

    write_output()
    denom_bad = jnp.logical_not(acc_s[RKV:RKV + 1, :] >= DENOM_FLOOR)
    redo = jnp.sum(jnp.where(denom_bad, 1, 0)) > 0

    def attn_softmax(jh, buf, p_buf):
        bias_s[...] = jnp.where(sc_s[half_rows(jh), :] >= tau, 0.0, MASKED)
        alphas = []
        for hh in range(H):
            cols = slice(hh * TQ, (hh + 1) * TQ)
            m_old = m_s[hh:hh + 1, :]
            mx = _fold(buf[:, cols] + bias_s[...], jnp.maximum)
            m_new = jnp.maximum(m_old, jnp.max(mx, axis=0, keepdims=True))
            m_s[hh:hh + 1, :] = m_new
            alphas.append(jnp.exp2((m_old - m_new) * c_exp))
            p_buf[:, cols] = jnp.exp2((buf[:, cols] + bias_s[...] - m_new) * c_exp).astype(BF16)
        return jnp.concatenate(alphas, axis=1)

    def attn_values(jh, p_buf, alpha):
        cols_j = pl.ds(pl.multiple_of(jh * TH, TH), TH)
        acc_s[...] = acc_s[...] * alpha + _dot(ckvt_ref[:, cols_j], p_buf[...])

    def attn_pair(j, alpha_b):
        attn_logits(2 * j + 1, lg_b)
        alpha_a = attn_softmax(2 * j, lg_a, p_a)
        attn_values(jnp.maximum(2 * j - 1, 0), p_b, alpha_b)
        attn_logits(jnp.minimum(2 * j + 2, last_half), lg_a)
        alpha_b = attn_softmax(2 * j + 1, lg_b, p_b)
        attn_values(2 * j, p_a, alpha_a)
        return alpha_b

    @pl.when(redo)
    def _():
        m_s[...] = jnp.full_like(m_s, MASKED)
        acc_s[...] = jnp.zeros_like(acc_s)
        p_b[...] = jnp.zeros_like(p_b)
        attn_logits(0, lg_a)
        alpha_b = lax.fori_loop(0, n_tiles, attn_pair, jnp.ones((1, H * TQ), F32))
        attn_values(last_half, p_b, alpha_b)
        write_output()


def _dsa_attn(x, qi, wt, ql, kidx, ckv, ckvt, qn, kv_norm, w_uv, w_out):
    b, s, d = x.shape
    H, RKV, HI, TQ = DSA_HEADS, DSA_KV_RANK, IDX_HEADS, ATT_Q
    nq = s // TQ
    top_k = min(TOPK_MAX, s // 4)
    wuvt = w_uv.transpose(1, 2, 0).astype(BF16)
    wout = w_out.astype(BF16)
    cmax = (math.sqrt(RKV) * jnp.max(jnp.abs(kv_norm))).reshape(1, 1).astype(F32)
    tri = jnp.tril(jnp.ones((ATT_K // 2, ATT_K // 2), F32)).astype(BF16)
    qrow = pl.BlockSpec((None, TQ, d), lambda bi, i: (bi, i, 0))
    qblk = lambda r, c: pl.BlockSpec((None, r, c), lambda bi, i: (bi * nq + i, 0, 0))
    seq = lambda width: pl.BlockSpec((s, width), lambda bi, i: (bi, 0))
    return pl.pallas_call(
        functools.partial(_dsa_attn_kernel, top_k=top_k),
        grid=(b, nq),
        in_specs=[qrow, qblk(LANES, HI * TQ), qblk(HI, TQ), qblk(RKV, H * TQ), qblk(1, H * TQ),
                  seq(LANES), seq(RKV),
                  pl.BlockSpec((KV_ROWS, s), lambda bi, i: (0, bi)),
                  _resident(wuvt.shape), _resident(wout.shape), _resident((1, 1)),
                  _resident(tri.shape)],
        out_specs=qrow,
        out_shape=jax.ShapeDtypeStruct((b, s, d), F32),
        scratch_shapes=[
            pltpu.VMEM((s, TQ), F32),
            pltpu.VMEM((ATT_K // 2, H * TQ), F32),
            pltpu.VMEM((ATT_K // 2, H * TQ), F32),
            pltpu.VMEM((ATT_K // 2, H * TQ), BF16),
            pltpu.VMEM((ATT_K // 2, H * TQ), BF16),
            pltpu.VMEM((ATT_K // 2, TQ), F32),
            pltpu.VMEM((KV_ROWS, H * TQ), F32),
            pltpu.VMEM((H, TQ), F32),
        ],
        compiler_params=_params(2),
        name="dsa_attn",
    )(x, qi, wt, ql, qn, kidx, ckv, ckvt, wuvt, wout, cmax, tri)


def _dsa(x, g, w_in, q_norm, kv_norm, w_uq, w_qidx, kidx_g, kidx_b, w_uk, w_uv, w_out):
    b, s, d = x.shape
    qi, wt, ql, kidx, ckv, ckvt, qn = _dsa_proj(x.reshape(b * s, d), g, w_in, q_norm, kv_norm, w_uq,
                                                w_qidx, kidx_g, kidx_b, w_uk)
    return _dsa_attn(x, qi, wt, ql, kidx, ckv, ckvt, qn, kv_norm, w_uv, w_out)


def kernel(x, norm_ffn1, w_ffn1_in, w_ffn1_out, norm_mix, norm_ffn2, w_ffn2_in, w_ffn2_out, ret_w_in,
           ret_w_out, dsa_w_in, dsa_q_norm, dsa_kv_norm, dsa_w_uq, dsa_w_qidx, dsa_kidx_g, dsa_kidx_b,
           dsa_w_uk, dsa_w_uv, dsa_w_out, final_norm):
    b, s, d = x.shape
    depth = norm_ffn1.shape[0]
    for layer in range(depth):
        x = _ffn(x.reshape(b * s, d), norm_ffn1[layer], w_ffn1_in[layer], w_ffn1_out[layer])
        x = x.reshape(b, s, d)
        j = layer // 2
        if layer % 2 == 0:
            x = _retention(x, norm_mix[layer], ret_w_in[j], ret_w_out[j])
        else:
            x = _dsa(x, norm_mix[layer], dsa_w_in[j], dsa_q_norm[j], dsa_kv_norm[j], dsa_w_uq[j],
                     dsa_w_qidx[j], dsa_kidx_g[j], dsa_kidx_b[j], dsa_w_uk[j], dsa_w_uv[j],
                     dsa_w_out[j])
        last = layer == depth - 1
        x = _ffn(x.reshape(b * s, d), norm_ffn2[layer], w_ffn2_in[layer], w_ffn2_out[layer],
                 final_g=final_norm if last else None)
        x = x.reshape(b, s, d)
    return x
```
